```python
import math
import jax, jax.numpy as jnp
from jax import lax
import numpy as np

D_MODEL = 2048
BATCH = 4
SEQ = 4096
DEPTH = 2

GRID_W = 64
CTX_LEN = 256
HEAD_DIM = 64

SSD_HEADS = 8
SSD_HEAD_DIM = 64
SSD_D_INNER = SSD_HEADS * SSD_HEAD_DIM
SSD_STATE = 128
SSD_GROUPS = 2
SSD_CONV = 5
SSD_CHUNK = 128
SSD_XBC = SSD_D_INNER + 2 * SSD_GROUPS * SSD_STATE

NA_HEADS = 8
NA_WIN_ROWS = 8
NA_WIN_COLS = 16
NA_COL_BLOCK = 16
NA_WIDTH = NA_HEADS * HEAD_DIM

SWA_HEADS = 8
SWA_KV_HEADS = 2
SWA_WINDOW = 128
SWA_BLOCK = 128
SWA_WIDTH = SWA_HEADS * HEAD_DIM

FN_GROUPS = 4
FN_GROUP_DIM = 128
FN_WIDTH = FN_GROUPS * FN_GROUP_DIM

N_BRANCH = 4
BRANCH_WIDTH = 512
D_FF = 5632
FFN_CONV = 3
ROPE_BASE = 10000.0
LN_EPS = 1e-6
ALPHA = (2.0 * DEPTH) ** 0.25
BETA = (8.0 * DEPTH) ** -0.25

IN_SIZES = (SSD_D_INNER, SSD_XBC, 2 * SSD_HEADS, 3 * NA_WIDTH, SWA_WIDTH,
            2 * SWA_KV_HEADS * HEAD_DIM, FN_WIDTH, N_BRANCH * D_MODEL)
IN_COLS = sum(IN_SIZES)

kernel_name = "hybrid_dit_ssd_natten_swa_fnet"


def layer_norm(x, g=None, b=None):
    xf = x.astype(jnp.float32)
    mu = xf.mean(-1, keepdims=True)
    var = jnp.square(xf - mu).mean(-1, keepdims=True)
    y = (xf - mu) * lax.rsqrt(var + LN_EPS)
    if g is not None:
        y = y * g.astype(jnp.float32) + b.astype(jnp.float32)
    return y.astype(x.dtype)


def rms_norm(x, g):
    xf = x.astype(jnp.float32)
    y = xf * lax.rsqrt(jnp.mean(jnp.square(xf), -1, keepdims=True) + LN_EPS) * g.astype(jnp.float32)
    return y.astype(x.dtype)


def dwconv(x, w, b):
    k = w.shape[0]
    y = lax.conv_general_dilated(x, w[:, None, :].astype(x.dtype), window_strides=(1,),
                                 padding=[(k // 2, k // 2)],
                                 dimension_numbers=('NWC', 'WIO', 'NWC'),
                                 feature_group_count=x.shape[-1])
    return y + b.astype(x.dtype)


def grid_rope(n_tokens):
    t = jnp.arange(n_tokens)
    rows = (t // GRID_W).astype(jnp.float32)
    cols = (t % GRID_W).astype(jnp.float32)
    n_freq = HEAD_DIM // 4
    inv = ROPE_BASE ** (-jnp.arange(n_freq, dtype=jnp.float32) / n_freq)
    ang = jnp.stack([rows[:, None] * inv, cols[:, None] * inv], axis=1)
    return jnp.cos(ang), jnp.sin(ang)


def apply_rope(x, cos, sin):
    xa = x.reshape(*x.shape[:-1], 2, 2, HEAD_DIM // 4)
    x1, x2 = xa[..., 0, :], xa[..., 1, :]
    cs, sn = cos[:, None], sin[:, None]
    out = jnp.stack([x1 * cs - x2 * sn, x2 * cs + x1 * sn], axis=-2)
    return out.reshape(x.shape).astype(x.dtype)


def _heads(t):
    return t.reshape(*t.shape[:2], -1, HEAD_DIM)


def ctx_attention(qc, kc, vc, sink=None):
    nb, n, hq, d = qc.shape
    hkv = kc.shape[2]
    g = hq // hkv
    qg = qc.reshape(nb, n, hkv, g, d)
    s = jnp.einsum('bqhgd,bkhd->bhgqk', qg, kc).astype(jnp.float32) * (d ** -0.5)
    if sink is not None:
        s_sink = jnp.broadcast_to(sink.astype(jnp.float32).reshape(hkv, g)[None, :, :, None, None],
                                  s.shape[:-1] + (1,))
        p = jax.nn.softmax(jnp.concatenate([s, s_sink], -1), axis=-1)[..., :n]
    else:
        p = jax.nn.softmax(s, axis=-1)
    o = jnp.einsum('bhgqk,bkhd->bqhgd', p.astype(vc.dtype), vc)
    return o.reshape(nb, n, hq * d)


def ssd_states(x, dt, a, bm, h0):
    nb, n = x.shape[:2]
    nc, r = n // SSD_CHUNK, SSD_HEADS // SSD_GROUPS
    xq = x.reshape(nb, nc, SSD_CHUNK, SSD_GROUPS, r, SSD_HEAD_DIM)
    dtq = dt.reshape(nb, nc, SSD_CHUNK, SSD_GROUPS, r)
    bq = bm.reshape(nb, nc, SSD_CHUNK, SSD_GROUPS, SSD_STATE)
    a_cum = jnp.cumsum(dtq * a.reshape(SSD_GROUPS, r), axis=2)
    w_end = jnp.exp(a_cum[:, :, -1:] - a_cum) * dtq
    chunk_st = jnp.einsum('bcjgn,bcjgr,bcjgrp->bcgrpn', bq, w_end, xq)
    chunk_dec = jnp.exp(a_cum[:, :, -1])

    def step(h, inp):
        st, dec = inp
        return dec[..., None, None] * h + st, h

    h_fin, h_start = lax.scan(step, h0, (jnp.moveaxis(chunk_st, 1, 0), jnp.moveaxis(chunk_dec, 1, 0)))
    return a_cum, jnp.moveaxis(h_start, 0, 1), h_fin


def ssd_outputs(x, dt, cm, bm, a_cum, h_start):
    nb, n = x.shape[:2]
    nc, r = n // SSD_CHUNK, SSD_HEADS // SSD_GROUPS
    xq = x.reshape(nb, nc, SSD_CHUNK, SSD_GROUPS, r, SSD_HEAD_DIM)
    dtq = dt.reshape(nb, nc, SSD_CHUNK, SSD_GROUPS, r)
    bq = bm.reshape(nb, nc, SSD_CHUNK, SSD_GROUPS, SSD_STATE)
    cq = cm.reshape(nb, nc, SSD_CHUNK, SSD_GROUPS, SSD_STATE)
    seg = a_cum[:, :, :, None] - a_cum[:, :, None, :]
    lower = jnp.tril(jnp.ones((SSD_CHUNK, SSD_CHUNK), bool))[:, :, None, None]
    decay = jnp.exp(jnp.where(lower, seg, -jnp.inf))
    cb = jnp.einsum('bcign,bcjgn->bcijg', cq, bq)
    w = cb[..., None] * decay * dtq[:, :, None]
    y_diag = jnp.einsum('bcijgr,bcjgrp->bcigrp', w, xq)
    y_off = jnp.einsum('bcign,bcgrpn->bcigrp', cq, h_start) * jnp.exp(a_cum)[..., None]
    return (y_diag + y_off).reshape(x.shape)


def ssd_branch(z, xbc, dt_raw, zc, xbcc, dtc_raw, conv_w, conv_b, a_log, dt_bias, d_skip, norm_g, ctx_out):
    a = -jnp.exp(a_log.astype(jnp.float32))

    def prep(xbc_s, dt_s):
        u = jax.nn.silu(dwconv(xbc_s, conv_w, conv_b))
        xs, bs, cs = jnp.split(u, [SSD_D_INNER, SSD_D_INNER + SSD_GROUPS * SSD_STATE], axis=-1)
        nb, n = u.shape[:2]
        dt = jax.nn.softplus(dt_s.astype(jnp.float32).reshape(nb, n, 2, SSD_HEADS)
                             + dt_bias.astype(jnp.float32))
        return (xs.reshape(nb, n, SSD_HEADS, SSD_HEAD_DIM),
                bs.reshape(nb, n, SSD_GROUPS, SSD_STATE),
                cs.reshape(nb, n, SSD_GROUPS, SSD_STATE), dt)

    xl, bl, cl, dtl = prep(xbc, dt_raw)
    xc, bc, cc, dtc = prep(xbcc, dtc_raw)
    h0 = jnp.zeros((xc.shape[0], SSD_GROUPS, SSD_HEADS // SSD_GROUPS, SSD_HEAD_DIM, SSD_STATE),
                   jnp.float32)
    y_lat = d_skip[:, None] * xl
    y_ctx = d_skip[:, None] * xc if ctx_out else None
    for d in range(2):
        rev = (lambda t: jnp.flip(t, axis=1)) if d == 1 else (lambda t: t)
        xcd, bcd, dtcd = rev(xc), rev(bc), rev(dtc[:, :, d])
        a_cum_c, h_start_c, h_ctx = ssd_states(xcd, dtcd, a[d], bcd, h0)
        if ctx_out:
            y_ctx = y_ctx + rev(ssd_outputs(xcd, dtcd, rev(cc), bcd, a_cum_c, h_start_c))
        xld, bld, cld, dtld = rev(xl), rev(bl), rev(cl), rev(dtl[:, :, d])
        a_cum_l, h_start_l, _ = ssd_states(xld, dtld, a[d], bld, h_ctx)
        y_lat = y_lat + rev(ssd_outputs(xld, dtld, cld, bld, a_cum_l, h_start_l))

    def finish(y, zz):
        return rms_norm(y.reshape(zz.shape) * jax.nn.silu(zz), norm_g)

    return finish(y_lat, z), (finish(y_ctx, zc) if ctx_out else None)


def na_branch(q, k, v, qc, kc, vc, rpb, ctx_out):
    nb, n = q.shape[:2]
    rows = n // GRID_W
    kh = min(NA_WIN_ROWS, rows)
    nblk = GRID_W // NA_COL_BLOCK
    slab = 2 * NA_WIN_COLS
    scale = HEAD_DIM ** -0.5
    r = jnp.arange(rows)
    row_idx = jnp.clip(r - kh // 2, 0, rows - kh)[:, None] + jnp.arange(kh)
    col_start = jnp.clip(jnp.arange(nblk) * NA_COL_BLOCK - NA_WIN_COLS // 2, 0, GRID_W - slab)
    col_idx = col_start[:, None] + jnp.arange(slab)
    qcol = jnp.arange(nblk)[:, None] * NA_COL_BLOCK + jnp.arange(NA_COL_BLOCK)
    win_start = jnp.clip(qcol - NA_WIN_COLS // 2, 0, GRID_W - NA_WIN_COLS)
    rel = col_idx[:, None, :] - win_start[:, :, None]
    in_win = (rel >= 0) & (rel < NA_WIN_COLS)
    dcol = jnp.clip(col_idx[:, None, :] - qcol[:, :, None] + NA_WIN_COLS - 1, 0, 2 * NA_WIN_COLS - 2)
    drow = row_idx - r[:, None] + NA_WIN_ROWS - 1
    bias = rpb.astype(jnp.float32)[:, drow[:, None, None, :, None], dcol[None, :, :, None, :]]

    k_g = k.reshape(nb, rows, GRID_W, NA_HEADS, HEAD_DIM)
    v_g = v.reshape(nb, rows, GRID_W, NA_HEADS, HEAD_DIM)
    gi_r, gi_c = row_idx[:, None, :, None], col_idx[None, :, None, :]
    k_nb = k_g[:, gi_r, gi_c]
    v_nb = v_g[:, gi_r, gi_c]
    q_b = q.reshape(nb, rows, nblk, NA_COL_BLOCK, NA_HEADS, HEAD_DIM)
    s_lat = jnp.einsum('brjqhd,brjkwhd->bhrjqkw', q_b, k_nb).astype(jnp.float32) * scale + bias
    s_lat = jnp.where(in_win[:, :, None, :], s_lat, -jnp.inf)
    s_lat = s_lat.reshape(nb, NA_HEADS, rows, nblk, NA_COL_BLOCK, kh * slab)
    s_ctx = jnp.einsum('brjqhd,bkhd->bhrjqk', q_b, kc).astype(jnp.float32) * scale
    p = jax.nn.softmax(jnp.concatenate([s_lat, s_ctx], -1), axis=-1)
    p_lat = p[..., :kh * slab].reshape(nb, NA_HEADS, rows, nblk, NA_COL_BLOCK, kh, slab).astype(v.dtype)
    p_ctx = p[..., kh * slab:].astype(v.dtype)
    o = (jnp.einsum('bhrjqkw,brjkwhd->brjqhd', p_lat, v_nb)
         + jnp.einsum('bhrjqk,bkhd->brjqhd', p_ctx, vc))
    o = o.reshape(nb, n, NA_WIDTH)
    return o, (ctx_attention(qc, kc, vc) if ctx_out else None)


def swa_branch(q, k, v, qc, kc, vc, sink, ctx_out):
    nb, n = q.shape[:2]
    nblk = n // SWA_BLOCK
    g = SWA_HEADS // SWA_KV_HEADS
    scale = HEAD_DIM ** -0.5
    q_b = q.reshape(nb, nblk, SWA_BLOCK, SWA_KV_HEADS, g, HEAD_DIM)

    def band(t):
        tp = jnp.pad(t, ((0, 0), (SWA_BLOCK, SWA_BLOCK), (0, 0), (0, 0)))
        tb = tp.reshape(nb, nblk + 2, SWA_BLOCK, SWA_KV_HEADS, HEAD_DIM)
        return jnp.concatenate([tb[:, :-2], tb[:, 1:-1], tb[:, 2:]], axis=2)

    k_band, v_band = band(k), band(v)
    nk = 3 * SWA_BLOCK
    key_pos = jnp.arange(nblk)[:, None] * SWA_BLOCK - SWA_BLOCK + jnp.arange(nk)
    q_pos = jnp.arange(nblk)[:, None] * SWA_BLOCK + jnp.arange(SWA_BLOCK)
    kp = key_pos[:, None, :]
    valid = (jnp.abs(kp - q_pos[:, :, None]) <= SWA_WINDOW) & (kp >= 0) & (kp < n)
    s_lat = jnp.einsum('bnqhgd,bnkhd->bhgnqk', q_b, k_band).astype(jnp.float32) * scale
    s_lat = jnp.where(valid, s_lat, -jnp.inf)
    s_ctx = jnp.einsum('bnqhgd,bkhd->bhgnqk', q_b, kc).astype(jnp.float32) * scale
    s_sink = jnp.broadcast_to(sink.astype(jnp.float32).reshape(SWA_KV_HEADS, g)[None, :, :, None, None, None],
                              s_lat.shape[:-1] + (1,))
    p = jax.nn.softmax(jnp.concatenate([s_lat, s_ctx, s_sink], -1), axis=-1)
    n_ctx = kc.shape[1]
    p_lat = p[..., :nk].astype(v.dtype)
    p_ctx = p[..., nk:nk + n_ctx].astype(v.dtype)
    o = (jnp.einsum('bhgnqk,bnkhd->bnqhgd', p_lat, v_band)
         + jnp.einsum('bhgnqk,bkhd->bnqhgd', p_ctx, vc))
    o = o.reshape(nb, n, SWA_WIDTH)
    return o, (ctx_attention(qc, kc, vc, sink) if ctx_out else None)


def fourier_mix(h):
    hf = h.astype(jnp.float32).reshape(*h.shape[:2], FN_GROUPS, FN_GROUP_DIM)
    y = jnp.fft.fft2(hf, axes=(1, 3), norm="ortho").real
    return y.reshape(h.shape).astype(h.dtype)


def conv_ffn(h, w_up, conv_w, conv_b, w_down):
    gate, up = jnp.split(h @ w_up, 2, axis=-1)
    return (jax.nn.silu(dwconv(gate, conv_w, conv_b)) * up) @ w_down


def mixer_block(h, hc, cos, sin, ctx_out, w_in, b_gate, ssd_conv_w, ssd_conv_b, ssd_a_log,
                ssd_dt_bias, ssd_d, ssd_norm_g, na_rpb, swa_sink, w_branch, w_out):
    split_at = [int(s) for s in np.cumsum(IN_SIZES)[:-1]]
    z, xbc, dtr, na_qkv, swa_q, swa_kv, fn_in, gates = jnp.split(h @ w_in, split_at, axis=-1)
    zc, xbcc, dtrc, na_qkvc, swa_qc, swa_kvc, fn_inc, gates_c = jnp.split(hc @ w_in, split_at, axis=-1)

    y_ssd, y_ssd_c = ssd_branch(z, xbc, dtr, zc, xbcc, dtrc, ssd_conv_w, ssd_conv_b, ssd_a_log,
                                ssd_dt_bias, ssd_d, ssd_norm_g, ctx_out)

    qn, kn, vn = [_heads(t) for t in jnp.split(na_qkv, 3, axis=-1)]
    qnc, knc, vnc = [_heads(t) for t in jnp.split(na_qkvc, 3, axis=-1)]
    y_na, y_na_c = na_branch(qn, kn, vn, qnc, knc, vnc, na_rpb, ctx_out)

    qs = apply_rope(_heads(swa_q), cos, sin)
    ks, vs = [_heads(t) for t in jnp.split(swa_kv, 2, axis=-1)]
    ks = apply_rope(ks, cos, sin)
    qsc = _heads(swa_qc)
    ksc, vsc = [_heads(t) for t in jnp.split(swa_kvc, 2, axis=-1)]
    y_swa, y_swa_c = swa_branch(qs, ks, vs, qsc, ksc, vsc, swa_sink, ctx_out)

    def merge(ys, g_raw):
        g = jax.nn.sigmoid(g_raw + b_gate).reshape(*g_raw.shape[:-1], N_BRANCH, D_MODEL)
        acc = g[..., 0, :] * (ys[0] @ w_branch[0])
        for i in range(1, N_BRANCH):
            acc = acc + g[..., i, :] * (ys[i] @ w_branch[i])
        return acc @ w_out

    out = merge((y_ssd, y_na, y_swa, fourier_mix(fn_in)), gates)
    out_c = merge((y_ssd_c, y_na_c, y_swa_c, fourier_mix(fn_inc)), gates_c) if ctx_out else None
    return out, out_c


def setup_inputs(seed: int = 0) -> dict:
    key = jax.random.key(seed)
    ks = jax.random.split(key, 32)
    f32 = jnp.float32
    D = D_MODEL

    def nrm(k, shape, std):
        return jax.random.normal(k, shape, f32) * std

    dt0 = jnp.exp(jax.random.uniform(ks[11], (DEPTH, 2, SSD_HEADS), f32, math.log(1e-3), math.log(1e-1)))
    return {
        "x": nrm(ks[0], (BATCH, SEQ, D), 1.0),
        "c": nrm(ks[1], (BATCH, D), 1.0),
        "ctx": nrm(ks[2], (BATCH, CTX_LEN, D), 1.0),
        "c_ctx": nrm(ks[3], (D,), 1.0),
        "w_ada": nrm(ks[4], (DEPTH, D, 6 * D), 0.01),
        "b_ada": nrm(ks[5], (DEPTH, 6 * D), 0.02),
        "w_in": nrm(ks[6], (DEPTH, D, IN_COLS), D ** -0.5),
        "b_gate": nrm(ks[7], (DEPTH, N_BRANCH * D), 0.02),
        "ssd_conv_w": nrm(ks[8], (DEPTH, SSD_CONV, SSD_XBC), SSD_CONV ** -0.5),
        "ssd_conv_b": nrm(ks[9], (DEPTH, SSD_XBC), 0.02),
        "ssd_a_log": jnp.log(jax.random.uniform(ks[10], (DEPTH, 2, SSD_HEADS), f32, 1.0, 16.0)),
        "ssd_dt_bias": dt0 + jnp.log(-jnp.expm1(-dt0)),
        "ssd_d": 1.0 + nrm(ks[12], (DEPTH, SSD_HEADS), 0.1),
        "ssd_norm_g": 1.0 + nrm(ks[13], (DEPTH, SSD_D_INNER), 0.05),
        "na_rpb": nrm(ks[14], (DEPTH, NA_HEADS, 2 * NA_WIN_ROWS - 1, 2 * NA_WIN_COLS - 1), 0.02),
        "swa_sink": nrm(ks[15], (DEPTH, SWA_HEADS), 0.5),
        "w_branch": nrm(ks[16], (DEPTH, N_BRANCH, BRANCH_WIDTH, D), BRANCH_WIDTH ** -0.5),
        "w_out": nrm(ks[17], (DEPTH, D, D), D ** -0.5 * BETA),
        "ln1_g": 1.0 + nrm(ks[18], (DEPTH, D), 0.05),
        "ln1_b": nrm(ks[19], (DEPTH, D), 0.02),
        "ln2_g": 1.0 + nrm(ks[20], (DEPTH, D), 0.05),
        "ln2_b": nrm(ks[21], (DEPTH, D), 0.02),
        "ffn_w_up": nrm(ks[22], (DEPTH, D, 2 * D_FF), D ** -0.5),
        "ffn_conv_w": nrm(ks[23], (DEPTH, FFN_CONV, D_FF), FFN_CONV ** -0.5),
        "ffn_conv_b": nrm(ks[24], (DEPTH, D_FF), 0.02),
        "ffn_w_down": nrm(ks[25], (DEPTH, D_FF, D), D_FF ** -0.5 * BETA),
    }


def reference(x, c, ctx, c_ctx, w_ada, b_ada, w_in, b_gate, ssd_conv_w, ssd_conv_b, ssd_a_log,
              ssd_dt_bias, ssd_d, ssd_norm_g, na_rpb, swa_sink, w_branch, w_out, ln1_g, ln1_b,
              ln2_g, ln2_b, ffn_w_up, ffn_conv_w, ffn_conv_b, ffn_w_down):
    n = x.shape[1]
    cos, sin = grid_rope(n)
    silu_c = jax.nn.silu(c)
    silu_cc = jax.nn.silu(c_ctx)
    xc = ctx
    for l in range(DEPTH):
        ctx_out = l < DEPTH - 1
        sh1, sc1, g1, sh2, sc2, g2 = jnp.split((silu_c @ w_ada[l] + b_ada[l])[:, None, :], 6, axis=-1)
        csh1, csc1, cg1, csh2, csc2, cg2 = jnp.split((silu_cc @ w_ada[l] + b_ada[l])[None, None, :], 6, axis=-1)
        h = layer_norm(x) * (1 + sc1) + sh1
        hc = layer_norm(xc) * (1 + csc1) + csh1
        mix, mix_c = mixer_block(h, hc, cos, sin, ctx_out, w_in[l], b_gate[l], ssd_conv_w[l],
                                 ssd_conv_b[l], ssd_a_log[l], ssd_dt_bias[l], ssd_d[l], ssd_norm_g[l],
                                 na_rpb[l], swa_sink[l], w_branch[l], w_out[l])
        x = layer_norm(ALPHA * x + g1 * mix, ln1_g[l], ln1_b[l])
        h = layer_norm(x) * (1 + sc2) + sh2
        x = layer_norm(ALPHA * x + g2 * conv_ffn(h, ffn_w_up[l], ffn_conv_w[l], ffn_conv_b[l], ffn_w_down[l]),
                       ln2_g[l], ln2_b[l])
        if ctx_out:
            xc = layer_norm(ALPHA * xc + cg1 * mix_c, ln1_g[l], ln1_b[l])
            hc = layer_norm(xc) * (1 + csc2) + csh2
            xc = layer_norm(ALPHA * xc + cg2 * conv_ffn(hc, ffn_w_up[l], ffn_conv_w[l], ffn_conv_b[l],
                                                        ffn_w_down[l]), ln2_g[l], ln2_b[l])
    return x
```

```python
import functools
import math

import numpy as np
import jax
import jax.numpy as jnp
from jax import lax
from jax.experimental import pallas as pl
from jax.experimental.pallas import tpu as pltpu

F32 = jnp.float32
BF16 = jnp.bfloat16

GRID_W = 64
HEAD_DIM = 64
SSD_HEADS = 8
SSD_D_INNER = 512
SSD_STATE = 128
SSD_GROUPS = 2
SSD_CHUNK = 128
SSD_XBC = 1024
NA_WIN_ROWS = 8
NA_WIN_COLS = 16
NA_Q_ROWS = 4
NA_K_ROWS = 12
SWA_WINDOW = 128
SWA_Q = 256
SWA_K = 512
FN_GROUP_DIM = 128
N_BRANCH = 4
BRANCH_WIDTH = 512
ROPE_BASE = 10000.0
LN_EPS = 1e-6
NEG = -1e30
DFT_ROWS = 64

LANE = 128
VMEM_LIMIT = 56 * 1024 * 1024


def _pick(dim, prefs):
    for p in prefs:
        if p <= dim and dim % p == 0:
            return p
    return dim


def _params(sem):
    return pltpu.CompilerParams(dimension_semantics=sem, vmem_limit_bytes=VMEM_LIMIT)


def _layer_norm(v):
    mu = jnp.mean(v, axis=-1, keepdims=True)
    vc = v - mu
    var = jnp.mean(vc * vc, axis=-1, keepdims=True)
    return vc * lax.rsqrt(var + LN_EPS)


def _silu(v):
    return v * jax.nn.sigmoid(v)


def _ada_kernel(c_ref, w_ref, b_ref, o_ref):
    s = _silu(c_ref[...])
    o_ref[...] = jnp.dot(s.astype(BF16), w_ref[...].astype(BF16),
                         preferred_element_type=F32) + b_ref[...]


def _ada_mods(cvec, w_ada, b_ada):
    depth, d, n6 = w_ada.shape
    r = cvec.shape[0]
    tn = _pick(n6, (1024, 512, 256, 128))
    return pl.pallas_call(
        _ada_kernel,
        grid=(depth, n6 // tn),
        in_specs=[pl.BlockSpec((r, d), lambda l, j: (0, 0)),
                  pl.BlockSpec((None, d, tn), lambda l, j: (l, 0, j)),
                  pl.BlockSpec((None, 1, tn), lambda l, j: (l, 0, j))],
        out_specs=pl.BlockSpec((None, r, tn), lambda l, j: (l, 0, j)),
        out_shape=jax.ShapeDtypeStruct((depth, r, n6), F32),
        compiler_params=_params(("parallel", "parallel")),
        name="ada_mods",
    )(cvec, w_ada, b_ada.reshape(depth, 1, n6))


def _lnmod_mm_kernel(x_ref, sh_ref, sc_ref, w_ref, o_ref, h_ref):
    @pl.when(pl.program_id(1) == 0)
    def _():
        y = _layer_norm(x_ref[...])
        h_ref[...] = (y * (1.0 + sc_ref[...]) + sh_ref[...]).astype(BF16)

    o_ref[...] = jnp.dot(h_ref[...], w_ref[...], preferred_element_type=F32).astype(o_ref.dtype)


def _lnmod_matmul(x2, mod3, mod_row, sh_chunk, sc_chunk, w, tn):
    m, d = x2.shape
    n = w.shape[1]
    tm = mod_row.tm
    return pl.pallas_call(
        _lnmod_mm_kernel,
        grid=(m // tm, n // tn),
        in_specs=[pl.BlockSpec((tm, d), lambda i, j: (i, 0)),
                  pl.BlockSpec((None, 1, d), lambda i, j: (mod_row(i), 0, sh_chunk)),
                  pl.BlockSpec((None, 1, d), lambda i, j: (mod_row(i), 0, sc_chunk)),
                  pl.BlockSpec((d, tn), lambda i, j: (0, j))],
        out_specs=pl.BlockSpec((tm, tn), lambda i, j: (i, j)),
        out_shape=jax.ShapeDtypeStruct((m, n), BF16),
        scratch_shapes=[pltpu.VMEM((tm, d), BF16)],
        compiler_params=_params(("parallel", "arbitrary")),
        name="lnmod_matmul",
    )(x2, mod3, mod3, w)


class _ModRow:
    def __init__(self, tm, seq=None, fixed=None):
        self.tm, self.seq, self.fixed = tm, seq, fixed

    def __call__(self, i):
        if self.fixed is not None:
            return self.fixed
        return (i * self.tm) // self.seq


def _dwconv_rows(x, w_ref):
    n = x.shape[0]
    k = w_ref.shape[0]
    row = lax.broadcasted_iota(jnp.int32, x.shape, 0)
    acc = x * w_ref[k // 2:k // 2 + 1, :]
    for t in range(k):
        off = t - k // 2
        if off == 0:
            continue
        shifted = pltpu.roll(x, (-off) % n, 0)
        valid = (row + off >= 0) & (row + off < n)
        acc = acc + jnp.where(valid, shifted, 0.0) * w_ref[t:t + 1, :]
    return acc


def _conv_silu_kernel(x_ref, w_ref, b_ref, o_ref):
    y = _dwconv_rows(x_ref[...].astype(F32), w_ref) + b_ref[...]
    o_ref[...] = _silu(y).astype(o_ref.dtype)


def _conv_silu_mul_kernel(x_ref, m_ref, w_ref, b_ref, o_ref):
    y = _dwconv_rows(x_ref[...].astype(F32), w_ref) + b_ref[...]
    o_ref[...] = (_silu(y) * m_ref[...].astype(F32)).astype(o_ref.dtype)


def _conv_silu(src3, col0, width, conv_w, conv_b, mul_col0=None):
    b, n, _ = src3.shape
    tc = LANE
    nblk = width // tc
    c0 = col0 // tc
    in_specs = [pl.BlockSpec((None, n, tc), lambda bi, j: (bi, 0, c0 + j))]
    args = [src3]
    kern = _conv_silu_kernel
    if mul_col0 is not None:
        m0 = mul_col0 // tc
        in_specs.append(pl.BlockSpec((None, n, tc), lambda bi, j: (bi, 0, m0 + j)))
        args.append(src3)
        kern = _conv_silu_mul_kernel
    kk = conv_w.shape[0]
    in_specs += [pl.BlockSpec((kk, tc), lambda bi, j: (0, j)),
                 pl.BlockSpec((1, tc), lambda bi, j: (0, j))]
    args += [conv_w, conv_b.reshape(1, width)]
    return pl.pallas_call(
        kern,
        grid=(b, nblk),
        in_specs=in_specs,
        out_specs=pl.BlockSpec((None, n, tc), lambda bi, j: (bi, 0, j)),
        out_shape=jax.ShapeDtypeStruct((b, n, width), BF16),
        compiler_params=_params(("parallel", "parallel")),
        name="conv_silu",
    )(*args)


def _softplus(v):
    return jnp.maximum(v, 0.0) + jnp.log(1.0 + jnp.exp(-jnp.abs(v)))


def _ssd_direction(u_ref, dt_ref, a_row, dtb_row, s_ref, y_ref, d, reverse):
    q = SSD_CHUNK
    dt_all = _softplus(dt_ref[...].astype(F32) + dtb_row)
    cum = dt_all * a_row
    row = lax.broadcasted_iota(jnp.int32, (q, LANE), 0)
    s = 1
    while s < q:
        if reverse:
            cum = cum + jnp.where(row < q - s, pltpu.roll(cum, q - s, 0), 0.0)
        else:
            cum = cum + jnp.where(row >= s, pltpu.roll(cum, s, 0), 0.0)
        s *= 2
    tot = cum[0:1, :] if reverse else cum[q - 1:q, :]
    cum_t = cum.T
    dt_t = dt_all.T
    e_cum = jnp.exp(cum)
    w_end = jnp.exp(tot - cum) * dt_all
    e_tot = jnp.exp(tot)

    ri = lax.broadcasted_iota(jnp.int32, (q, q), 0)
    ci = lax.broadcasted_iota(jnp.int32, (q, q), 1)
    tri = (ri <= ci) if reverse else (ri >= ci)
    lane = lax.broadcasted_iota(jnp.int32, (q, LANE), 1)
    lo = lane < HEAD_DIM
    lane_row = lax.broadcasted_iota(jnp.int32, (1, LANE), 1)

    def pair_cols(v, c0, c1):
        return jnp.where(lo, jnp.broadcast_to(v[:, c0:c0 + 1], (q, LANE)),
                         jnp.broadcast_to(v[:, c1:c1 + 1], (q, LANE)))

    heads_per_group = SSD_HEADS // SSD_GROUPS
    for g in range(SSD_GROUPS):
        b0 = SSD_D_INNER + g * SSD_STATE
        c0 = SSD_D_INNER + SSD_GROUPS * SSD_STATE + g * SSD_STATE
        bg = u_ref[:, b0:b0 + SSD_STATE]
        cg = u_ref[:, c0:c0 + SSD_STATE]
        cb = lax.dot_general(cg, bg, (((1,), (1,)), ((), ())), preferred_element_type=F32)
        bg_t = bg.astype(F32).T.astype(BF16)
        for hp in range(heads_per_group // 2):
            h0 = g * heads_per_group + 2 * hp
            col0, col1 = d * SSD_HEADS + h0, d * SSD_HEADS + h0 + 1
            l0 = h0 * HEAD_DIM
            xp = u_ref[:, l0:l0 + LANE]
            ws = []
            for col in (col0, col1):
                seg = cum[:, col:col + 1] - cum_t[col:col + 1, :]
                dec = jnp.exp(jnp.where(tri, seg, NEG))
                ws.append((dec * cb * dt_t[col:col + 1, :]).astype(BF16))
            w_pair = jnp.concatenate(ws, axis=1)
            zero = jnp.zeros_like(xp)
            x_bd = jnp.concatenate([jnp.where(lo, xp, zero), jnp.where(lo, zero, xp)], axis=0)
            y_diag = jnp.dot(w_pair, x_bd, preferred_element_type=F32)
            st = s_ref[d, :, l0:l0 + LANE]
            y_off = jnp.dot(cg, st.astype(BF16), preferred_element_type=F32) * pair_cols(e_cum, col0, col1)
            y_ref[:, l0:l0 + LANE] = y_diag + y_off
            xw = (xp.astype(F32) * pair_cols(w_end, col0, col1)).astype(BF16)
            upd = jnp.dot(bg_t, xw, preferred_element_type=F32)
            tot_pair = jnp.where(lane_row < HEAD_DIM,
                                 jnp.broadcast_to(e_tot[:, col0:col0 + 1], (1, LANE)),
                                 jnp.broadcast_to(e_tot[:, col1:col1 + 1], (1, LANE)))
            s_ref[d, :, l0:l0 + LANE] = st * tot_pair + upd


def _ssd_kernel(uf_ref, ub_ref, dtf_ref, dtb_ref, alog_ref, dtbias_ref, h0_ref,
                yf_ref, yb_ref, hT_ref, s_ref):
    t = pl.program_id(1)

    @pl.when(t == 0)
    def _():
        s_ref[...] = h0_ref[...]

    a_row = -jnp.exp(alog_ref[...])
    dtb_row = dtbias_ref[...]
    _ssd_direction(uf_ref, dtf_ref, a_row, dtb_row, s_ref, yf_ref, 0, False)
    _ssd_direction(ub_ref, dtb_ref, a_row, dtb_row, s_ref, yb_ref, 1, True)

    @pl.when(t == pl.num_programs(1) - 1)
    def _():
        hT_ref[...] = s_ref[...]


def _ssd_scan(u3, p3, dt_col0, alog_row, dtbias_row, h0):
    b, n, _ = u3.shape
    q = SSD_CHUNK
    nt = n // q
    dtb = dt_col0 // LANE
    hp = SSD_D_INNER
    return pl.pallas_call(
        _ssd_kernel,
        grid=(b, nt),
        in_specs=[pl.BlockSpec((None, q, SSD_XBC), lambda bi, t: (bi, t, 0)),
                  pl.BlockSpec((None, q, SSD_XBC), lambda bi, t: (bi, nt - 1 - t, 0)),
                  pl.BlockSpec((None, q, LANE), lambda bi, t: (bi, t, dtb)),
                  pl.BlockSpec((None, q, LANE), lambda bi, t: (bi, nt - 1 - t, dtb)),
                  pl.BlockSpec((1, LANE), lambda bi, t: (0, 0)),
                  pl.BlockSpec((1, LANE), lambda bi, t: (0, 0)),
                  pl.BlockSpec((None, 2, SSD_STATE, hp), lambda bi, t: (bi, 0, 0, 0))],
        out_specs=[pl.BlockSpec((None, q, hp), lambda bi, t: (bi, t, 0)),
                   pl.BlockSpec((None, q, hp), lambda bi, t: (bi, nt - 1 - t, 0)),
                   pl.BlockSpec((None, 2, SSD_STATE, hp), lambda bi, t: (bi, 0, 0, 0))],
        out_shape=[jax.ShapeDtypeStruct((b, n, hp), F32),
                   jax.ShapeDtypeStruct((b, n, hp), F32),
                   jax.ShapeDtypeStruct((b, 2, SSD_STATE, hp), F32)],
        scratch_shapes=[pltpu.VMEM((2, SSD_STATE, hp), F32)],
        compiler_params=_params(("parallel", "arbitrary")),
        name="ssd_scan",
    )(u3, u3, p3, p3, alog_row, dtbias_row, h0)


def _ssd_finish_kernel(yf_ref, yb_ref, x_ref, z_ref, dskip_ref, g_ref, o_ref):
    y = dskip_ref[...] * x_ref[...].astype(F32) + yf_ref[...] + yb_ref[...]
    y = y * _silu(z_ref[...].astype(F32))
    r = lax.rsqrt(jnp.mean(y * y, axis=-1, keepdims=True) + LN_EPS)
    o_ref[...] = (y * r * g_ref[...]).astype(o_ref.dtype)


def _ssd_finish(yf2, yb2, u2, p2, z_col0, dskip_row, g_row):
    m, w = yf2.shape
    tm = _pick(m, (1024, 512, 256, 128))
    zb = z_col0 // w
    row = lambda i: (i, 0)
    return pl.pallas_call(
        _ssd_finish_kernel,
        grid=(m // tm,),
        in_specs=[pl.BlockSpec((tm, w), row), pl.BlockSpec((tm, w), row),
                  pl.BlockSpec((tm, w), row),
                  pl.BlockSpec((tm, w), lambda i: (i, zb)),
                  pl.BlockSpec((1, w), lambda i: (0, 0)),
                  pl.BlockSpec((1, w), lambda i: (0, 0))],
        out_specs=pl.BlockSpec((tm, w), row),
        out_shape=jax.ShapeDtypeStruct((m, w), BF16),
        compiler_params=_params(("parallel",)),
        name="ssd_finish",
    )(yf2, yb2, u2, p2, dskip_row, g_row)


def _block_diag_rows(kv):
    lane = lax.broadcasted_iota(jnp.int32, kv.shape, 1)
    lo = lane < HEAD_DIM
    zero = jnp.zeros_like(kv)
    return jnp.concatenate([jnp.where(lo, kv, zero), jnp.where(lo, zero, kv)], axis=0)


def _dup_group(kv, g):
    lane = lax.broadcasted_iota(jnp.int32, kv.shape, 1)
    rolled = pltpu.roll(kv, HEAD_DIM, 1)
    return jnp.where(lane // HEAD_DIM == g, kv, rolled)


def _qk(q, kbd):
    return lax.dot_general(q, kbd, (((1,), (1,)), ((), ())), preferred_element_type=F32)


def _pair_softmax_pv(s_w, s_c, bias, vbd, vcbd, sink, out_dtype):
    nk = s_w.shape[1] // 2
    nc = s_c.shape[1] // 2
    tq = s_w.shape[0]
    pws, pcs, invs = [], [], []
    for h in range(2):
        sw = s_w[:, h * nk:(h + 1) * nk] + bias[h]
        sc = s_c[:, h * nc:(h + 1) * nc]
        m = jnp.maximum(jnp.max(sw, axis=-1, keepdims=True), jnp.max(sc, axis=-1, keepdims=True))
        if sink is not None:
            m = jnp.maximum(m, sink[h])
        pw = jnp.exp(sw - m)
        pc = jnp.exp(sc - m)
        l = jnp.sum(pw, axis=-1, keepdims=True) + jnp.sum(pc, axis=-1, keepdims=True)
        if sink is not None:
            l = l + jnp.exp(sink[h] - m)
        pws.append(pw.astype(BF16))
        pcs.append(pc.astype(BF16))
        invs.append(1.0 / l)
    o = jnp.dot(jnp.concatenate(pws, axis=1), vbd, preferred_element_type=F32)
    o = o + jnp.dot(jnp.concatenate(pcs, axis=1), vcbd, preferred_element_type=F32)
    lane = lax.broadcasted_iota(jnp.int32, (tq, LANE), 1)
    inv = jnp.where(lane < HEAD_DIM, jnp.broadcast_to(invs[0], (tq, LANE)),
                    jnp.broadcast_to(invs[1], (tq, LANE)))
    return (o * inv).astype(out_dtype)


def _na_kernel(q_ref, k_ref, v_ref, kc_ref, vc_ref, bias_ref, o_ref, *, rows):
    qb = pl.program_id(2)
    nk = NA_K_ROWS * GRID_W
    ks = jnp.clip(NA_Q_ROWS * qb - (NA_K_ROWS - NA_Q_ROWS) // 2, 0, rows - NA_K_ROWS) * GRID_W
    ks = pl.multiple_of(ks, GRID_W)
    q = q_ref[...] * (HEAD_DIM ** -0.5)
    kbd = _block_diag_rows(k_ref[pl.ds(ks, nk), :])
    vbd = _block_diag_rows(v_ref[pl.ds(ks, nk), :])
    kcbd = _block_diag_rows(kc_ref[...])
    vcbd = _block_diag_rows(vc_ref[...])
    s_w = _qk(q, kbd)
    s_c = _qk(q, kcbd)
    o_ref[...] = _pair_softmax_pv(s_w, s_c, [bias_ref[0], bias_ref[1]], vbd, vcbd, None, o_ref.dtype)


def _na_key_start(qb, rows):
    return min(max(NA_Q_ROWS * qb - (NA_K_ROWS - NA_Q_ROWS) // 2, 0), rows - NA_K_ROWS)


def _na_bias_tables(rpb, rows):
    nqb = rows // NA_Q_ROWS
    variants = [0, 1 if nqb > 2 else 0, nqb - 1]
    idx_r = np.zeros((3, NA_Q_ROWS * GRID_W, NA_K_ROWS * GRID_W), np.int32)
    idx_c = np.zeros_like(idx_r)
    ok = np.zeros(idx_r.shape, bool)
    qr_l, qc = np.divmod(np.arange(NA_Q_ROWS * GRID_W), GRID_W)
    kr_l, kc = np.divmod(np.arange(NA_K_ROWS * GRID_W), GRID_W)
    for vi, qb in enumerate(variants):
        qr = NA_Q_ROWS * qb + qr_l
        kr = _na_key_start(qb, rows) + kr_l
        rs = np.clip(qr - NA_WIN_ROWS // 2, 0, rows - NA_WIN_ROWS)
        ws = np.clip(qc - NA_WIN_COLS // 2, 0, GRID_W - NA_WIN_COLS)
        in_r = (kr[None, :] >= rs[:, None]) & (kr[None, :] < rs[:, None] + NA_WIN_ROWS)
        in_c = (kc[None, :] >= ws[:, None]) & (kc[None, :] < ws[:, None] + NA_WIN_COLS)
        ok[vi] = in_r & in_c
        idx_r[vi] = np.clip(kr[None, :] - qr[:, None] + NA_WIN_ROWS - 1, 0, 2 * NA_WIN_ROWS - 2)
        idx_c[vi] = np.clip(kc[None, :] - qc[:, None] + NA_WIN_COLS - 1, 0, 2 * NA_WIN_COLS - 2)
    tab = rpb.astype(F32)[:, idx_r, idx_c]
    tab = jnp.where(jnp.asarray(ok)[None], tab, NEG)
    return jnp.transpose(tab, (1, 0, 2, 3))


def _na_attention(p3, pc3, q_col0, k_col0, v_col0, bias_tab):
    b, n, _ = p3.shape
    nc = pc3.shape[1]
    rows = n // GRID_W
    tq = NA_Q_ROWS * GRID_W
    nk = NA_K_ROWS * GRID_W
    nqb = n // tq
    qc, kc, vc = q_col0 // LANE, k_col0 // LANE, v_col0 // LANE
    npair = 4

    def variant(qb):
        return jnp.where(qb == 0, 0, jnp.where(qb == nqb - 1, 2, 1))

    return pl.pallas_call(
        functools.partial(_na_kernel, rows=rows),
        grid=(b, npair, nqb),
        in_specs=[pl.BlockSpec((None, tq, LANE), lambda bi, h, i: (bi, i, qc + h)),
                  pl.BlockSpec((None, n, LANE), lambda bi, h, i: (bi, 0, kc + h)),
                  pl.BlockSpec((None, n, LANE), lambda bi, h, i: (bi, 0, vc + h)),
                  pl.BlockSpec((None, nc, LANE), lambda bi, h, i: (bi, 0, kc + h)),
                  pl.BlockSpec((None, nc, LANE), lambda bi, h, i: (bi, 0, vc + h)),
                  pl.BlockSpec((None, 2, tq, nk), lambda bi, h, i: (variant(i), h, 0, 0))],
        out_specs=pl.BlockSpec((None, tq, LANE), lambda bi, h, i: (bi, i, h)),
        out_shape=jax.ShapeDtypeStruct((b, n, npair * LANE), BF16),
        compiler_params=_params(("parallel", "parallel", "arbitrary")),
        name="na_attention",
    )(p3, p3, p3, pc3, pc3, bias_tab)


def _rope(x, cos, sin_signed):
    lane = lax.broadcasted_iota(jnp.int32, x.shape, 1)
    first = (lane % 32) < 16
    partner = jnp.where(first, pltpu.roll(x, LANE - 16, 1), pltpu.roll(x, 16, 1))
    return x * cos + partner * sin_signed


def _sink_pair(sink_ref):
    s = sink_ref[...]
    return [s[:, 0:1], s[:, HEAD_DIM:HEAD_DIM + 1]]


def _swa_kernel(q_ref, k_ref, v_ref, kc_ref, vc_ref, cos_ref, sin_ref, sink_ref, o_ref, *, n):
    hp = pl.program_id(1)
    qb = pl.program_id(2)
    g = hp // 2
    q0 = pl.multiple_of(qb * SWA_Q, SWA_Q)
    ks = pl.multiple_of(jnp.clip(qb * SWA_Q - SWA_WINDOW, 0, n - SWA_K), SWA_WINDOW)
    q = _rope(q_ref[...].astype(F32), cos_ref[pl.ds(q0, SWA_Q), :], sin_ref[pl.ds(q0, SWA_Q), :])
    q = (q * (HEAD_DIM ** -0.5)).astype(BF16)
    kk = _rope(k_ref[pl.ds(ks, SWA_K), :].astype(F32), cos_ref[pl.ds(ks, SWA_K), :],
               sin_ref[pl.ds(ks, SWA_K), :])
    kbd = _block_diag_rows(_dup_group(kk, g).astype(BF16))
    vbd = _block_diag_rows(_dup_group(v_ref[pl.ds(ks, SWA_K), :].astype(F32), g).astype(BF16))
    kcbd = _block_diag_rows(_dup_group(kc_ref[...].astype(F32), g).astype(BF16))
    vcbd = _block_diag_rows(_dup_group(vc_ref[...].astype(F32), g).astype(BF16))
    s_w = _qk(q, kbd)
    s_c = _qk(q, kcbd)
    qpos = q0 + lax.broadcasted_iota(jnp.int32, (SWA_Q, SWA_K), 0)
    kpos = ks + lax.broadcasted_iota(jnp.int32, (SWA_Q, SWA_K), 1)
    bias = jnp.where(jnp.abs(kpos - qpos) <= SWA_WINDOW, 0.0, NEG)
    o_ref[...] = _pair_softmax_pv(s_w, s_c, [bias, bias], vbd, vcbd, _sink_pair(sink_ref), o_ref.dtype)


def _swa_attention(p3, pc3, q_col0, kv_col0, cos_tab, sin_tab, sink_rows):
    b, n, _ = p3.shape
    nc = pc3.shape[1]
    nqb = n // SWA_Q
    qc, kc = q_col0 // LANE, kv_col0 // LANE
    npair = 4
    return pl.pallas_call(
        functools.partial(_swa_kernel, n=n),
        grid=(b, npair, nqb),
        in_specs=[pl.BlockSpec((None, SWA_Q, LANE), lambda bi, h, i: (bi, i, qc + h)),
                  pl.BlockSpec((None, n, LANE), lambda bi, h, i: (bi, 0, kc)),
                  pl.BlockSpec((None, n, LANE), lambda bi, h, i: (bi, 0, kc + 1)),
                  pl.BlockSpec((None, nc, LANE), lambda bi, h, i: (bi, 0, kc)),
                  pl.BlockSpec((None, nc, LANE), lambda bi, h, i: (bi, 0, kc + 1)),
                  pl.BlockSpec((n, LANE), lambda bi, h, i: (0, 0)),
                  pl.BlockSpec((n, LANE), lambda bi, h, i: (0, 0)),
                  pl.BlockSpec((None, 1, LANE), lambda bi, h, i: (h, 0, 0))],
        out_specs=pl.BlockSpec((None, SWA_Q, LANE), lambda bi, h, i: (bi, i, h)),
        out_shape=jax.ShapeDtypeStruct((b, n, npair * LANE), BF16),
        compiler_params=_params(("parallel", "parallel", "arbitrary")),
        name="swa_attention",
    )(p3, p3, p3, pc3, pc3, cos_tab, sin_tab, sink_rows)


def _ctx_attn_kernel(q_ref, k_ref, v_ref, sink_ref, o_ref, *, grouped):
    hp = pl.program_id(1)
    q = q_ref[...] * (HEAD_DIM ** -0.5)
    k, v = k_ref[...], v_ref[...]
    if grouped:
        g = hp // 2
        k = _dup_group(k.astype(F32), g).astype(BF16)
        v = _dup_group(v.astype(F32), g).astype(BF16)
    kbd, vbd = _block_diag_rows(k), _block_diag_rows(v)
    s = _qk(q, kbd)
    t = q.shape[0]
    sink = _sink_pair(sink_ref) if grouped else None
    ps, invs = [], []
    for h in range(2):
        sh = s[:, h * t:(h + 1) * t]
        m = jnp.max(sh, axis=-1, keepdims=True)
        if sink is not None:
            m = jnp.maximum(m, sink[h])
        p = jnp.exp(sh - m)
        l = jnp.sum(p, axis=-1, keepdims=True)
        if sink is not None:
            l = l + jnp.exp(sink[h] - m)
        ps.append(p.astype(BF16))
        invs.append(1.0 / l)
    o = jnp.dot(jnp.concatenate(ps, axis=1), vbd, preferred_element_type=F32)
    lane = lax.broadcasted_iota(jnp.int32, (t, LANE), 1)
    inv = jnp.where(lane < HEAD_DIM, jnp.broadcast_to(invs[0], (t, LANE)),
                    jnp.broadcast_to(invs[1], (t, LANE)))
    o_ref[...] = (o * inv).astype(o_ref.dtype)


def _ctx_attention(pc3, q_col0, k_col0, v_col0, sink_rows, grouped):
    b, nc, _ = pc3.shape
    qc, kc, vc = q_col0 // LANE, k_col0 // LANE, v_col0 // LANE
    npair = 4
    kv_blk = (lambda h: 0) if grouped else (lambda h: h)
    return pl.pallas_call(
        functools.partial(_ctx_attn_kernel, grouped=grouped),
        grid=(b, npair),
        in_specs=[pl.BlockSpec((None, nc, LANE), lambda bi, h: (bi, 0, qc + h)),
                  pl.BlockSpec((None, nc, LANE), lambda bi, h: (bi, 0, kc + kv_blk(h))),
                  pl.BlockSpec((None, nc, LANE), lambda bi, h: (bi, 0, vc + kv_blk(h))),
                  pl.BlockSpec((None, 1, LANE), lambda bi, h: (h, 0, 0))],
        out_specs=pl.BlockSpec((None, nc, LANE), lambda bi, h: (bi, 0, h)),
        out_shape=jax.ShapeDtypeStruct((b, nc, npair * LANE), BF16),
        compiler_params=_params(("parallel", "parallel")),
        name="ctx_attention",
    )(pc3, pc3, pc3, sink_rows)


def _fn_channel_kernel(x_ref, cs_ref, o_ref):
    w = x_ref.shape[1]
    ngroups = w // FN_GROUP_DIM
    for g in range(ngroups):
        xg = x_ref[:, g * FN_GROUP_DIM:(g + 1) * FN_GROUP_DIM]
        ab = jnp.dot(xg, cs_ref[...], preferred_element_type=F32)
        o_ref[:, g * FN_GROUP_DIM:(g + 1) * FN_GROUP_DIM] = ab[:, :FN_GROUP_DIM].astype(o_ref.dtype)
        o_ref[:, w + g * FN_GROUP_DIM:w + (g + 1) * FN_GROUP_DIM] = ab[:, FN_GROUP_DIM:].astype(o_ref.dtype)


def _fn_channel(p2, col0, width):
    m = p2.shape[0]
    tm = _pick(m, (1024, 512, 256, 128))
    k = np.arange(FN_GROUP_DIM)
    ang = 2.0 * np.pi * ((k[:, None] * k[None, :]) % FN_GROUP_DIM) / FN_GROUP_DIM
    cs = jnp.asarray(np.concatenate([np.cos(ang), np.sin(ang)], axis=1), BF16)
    cb = col0 // width
    return pl.pallas_call(
        _fn_channel_kernel,
        grid=(m // tm,),
        in_specs=[pl.BlockSpec((tm, width), lambda i: (i, cb)),
                  pl.BlockSpec((FN_GROUP_DIM, 2 * FN_GROUP_DIM), lambda i: (0, 0))],
        out_specs=pl.BlockSpec((tm, 2 * width), lambda i: (i, 0)),
        out_shape=jax.ShapeDtypeStruct((m, 2 * width), BF16),
        compiler_params=_params(("parallel",)),
        name="fn_channel",
    )(p2, cs)


def _fn_position_kernel(ac_ref, as_ref, bc_ref, bs_ref, ab_ref, o_ref, c_ref, s_ref, *, scale):
    @pl.when(pl.program_id(1) == 0)
    def _():
        for j in range(ac_ref.shape[0]):
            ca, sa = ac_ref[j:j + 1, :], as_ref[j:j + 1, :]
            cb, sb = bc_ref[...], bs_ref[...]
            c_ref[j * DFT_ROWS:(j + 1) * DFT_ROWS, :] = (ca * cb - sa * sb).astype(BF16)
            s_ref[j * DFT_ROWS:(j + 1) * DFT_ROWS, :] = (sa * cb + ca * sb).astype(BF16)

    w = o_ref.shape[1]
    y = jnp.dot(c_ref[...], ab_ref[:, :w], preferred_element_type=F32)
    y = y - jnp.dot(s_ref[...], ab_ref[:, w:], preferred_element_type=F32)
    o_ref[...] = (y * scale).astype(o_ref.dtype)


def _fn_position(ab3):
    b, n, w2 = ab3.shape
    w = w2 // 2
    tm = _pick(n, (512, 256, 128, 64))
    jc = tm // DFT_ROWS
    n1 = n // DFT_ROWS
    k = np.arange(n, dtype=np.int64)
    j1 = np.arange(n1, dtype=np.int64)
    j2 = np.arange(DFT_ROWS, dtype=np.int64)
    ang_a = 2.0 * np.pi * ((j1[:, None] * DFT_ROWS * k[None, :]) % n) / n
    ang_b = 2.0 * np.pi * ((j2[:, None] * k[None, :]) % n) / n
    ac, as_ = jnp.asarray(np.cos(ang_a), F32), jnp.asarray(np.sin(ang_a), F32)
    bc, bs = jnp.asarray(np.cos(ang_b), F32), jnp.asarray(np.sin(ang_b), F32)
    scale = 1.0 / math.sqrt(n * FN_GROUP_DIM)
    if n1 % 8 != 0 and jc != n1:
        raise ValueError("unsupported sequence length for the position DFT tiling")
    return pl.pallas_call(
        functools.partial(_fn_position_kernel, scale=scale),
        grid=(n // tm, b),
        in_specs=[pl.BlockSpec((jc, n), lambda i, bi: (i, 0)),
                  pl.BlockSpec((jc, n), lambda i, bi: (i, 0)),
                  pl.BlockSpec((DFT_ROWS, n), lambda i, bi: (0, 0)),
                  pl.BlockSpec((DFT_ROWS, n), lambda i, bi: (0, 0)),
                  pl.BlockSpec((None, n, w2), lambda i, bi: (bi, 0, 0))],
        out_specs=pl.BlockSpec((None, tm, w), lambda i, bi: (bi, i, 0)),
        out_shape=jax.ShapeDtypeStruct((b, n, w), BF16),
        scratch_shapes=[pltpu.VMEM((tm, n), BF16), pltpu.VMEM((tm, n), BF16)],
        compiler_params=_params(("parallel", "arbitrary")),
        name="fn_position",
    )(ac, as_, bc, bs, ab3)


def _merge_kernel(y0, y1, y2, y3, g0, g1, g2, g3, wb_ref, bg_ref, o_ref):
    acc = None
    for i, (y, g) in enumerate(((y0, g0), (y1, g1), (y2, g2), (y3, g3))):
        gate = jax.nn.sigmoid(g[...].astype(F32) + bg_ref[i])
        term = gate * jnp.dot(y[...], wb_ref[i], preferred_element_type=F32)
        acc = term if acc is None else acc + term
    o_ref[...] = acc.astype(o_ref.dtype)


def _merge(ys, p2, gate_col0, wb, bg3):
    m = p2.shape[0]
    d = wb.shape[2]
    tm = _pick(m, (1024, 512, 256))
    tn = _pick(d, (1024, 512, 256))
    gb = gate_col0 // tn
    per = d // tn
    bw = ys[0].shape[1]
    y_specs = [pl.BlockSpec((tm, bw), lambda i, j: (i, 0)) for _ in range(N_BRANCH)]
    g_specs = [pl.BlockSpec((tm, tn), functools.partial(lambda i, j, br: (i, gb + br * per + j), br=br))
               for br in range(N_BRANCH)]
    return pl.pallas_call(
        _merge_kernel,
        grid=(m // tm, per),
        in_specs=y_specs + g_specs + [
            pl.BlockSpec((N_BRANCH, bw, tn), lambda i, j: (0, 0, j)),
            pl.BlockSpec((N_BRANCH, 1, tn), lambda i, j: (0, 0, j))],
        out_specs=pl.BlockSpec((tm, tn), lambda i, j: (i, j)),
        out_shape=jax.ShapeDtypeStruct((m, d), BF16),
        compiler_params=_params(("parallel", "parallel")),
        name="branch_merge",
    )(*ys, p2, p2, p2, p2, wb, bg3)


def _mm_resln_kernel(a_ref, w_ref, x_ref, gate_ref, lng_ref, lnb_ref, o_ref, acc_ref, *, alpha):
    k = pl.program_id(1)

    @pl.when(k == 0)
    def _():
        acc_ref[...] = jnp.zeros_like(acc_ref)

    acc_ref[...] += jnp.dot(a_ref[...], w_ref[...], preferred_element_type=F32)

    @pl.when(k == pl.num_programs(1) - 1)
    def _():
        v = alpha * x_ref[...] + gate_ref[...] * acc_ref[...]
        o_ref[...] = _layer_norm(v) * lng_ref[...] + lnb_ref[...]


def _mm_resln(a2, w, x2, mod3, mod_row, gate_chunk, ln_g, ln_b, alpha, tk):
    m, kdim = a2.shape
    d = w.shape[1]
    tm = mod_row.tm
    return pl.pallas_call(
        functools.partial(_mm_resln_kernel, alpha=alpha),
        grid=(m // tm, kdim // tk),
        in_specs=[pl.BlockSpec((tm, tk), lambda i, k: (i, k)),
                  pl.BlockSpec((tk, d), lambda i, k: (k, 0)),
                  pl.BlockSpec((tm, d), lambda i, k: (i, 0)),
                  pl.BlockSpec((None, 1, d), lambda i, k: (mod_row(i), 0, gate_chunk)),
                  pl.BlockSpec((1, d), lambda i, k: (0, 0)),
                  pl.BlockSpec((1, d), lambda i, k: (0, 0))],
        out_specs=pl.BlockSpec((tm, d), lambda i, k: (i, 0)),
        out_shape=jax.ShapeDtypeStruct((m, d), F32),
        scratch_shapes=[pltpu.VMEM((tm, d), F32)],
        compiler_params=_params(("parallel", "arbitrary")),
        name="matmul_residual_ln",
    )(a2, w, x2, mod3, ln_g.reshape(1, d), ln_b.reshape(1, d))


def _rope_tables(n):
    t = jnp.arange(n)
    rows = (t // GRID_W).astype(F32)
    cols = (t % GRID_W).astype(F32)
    n_freq = HEAD_DIM // 4
    inv = ROPE_BASE ** (-jnp.arange(n_freq, dtype=F32) / n_freq)
    ar, ac = rows[:, None] * inv, cols[:, None] * inv
    cos = jnp.concatenate([jnp.cos(ar), jnp.cos(ar), jnp.cos(ac), jnp.cos(ac)], axis=1)
    sin = jnp.concatenate([-jnp.sin(ar), jnp.sin(ar), -jnp.sin(ac), jnp.sin(ac)], axis=1)
    return jnp.tile(cos, (1, 2)), jnp.tile(sin, (1, 2))


def _branch_params(conv_w, conv_b, a_log, dt_bias, d_skip, norm_g, rpb, sink, rows, cos_tab, sin_tab):
    pad = jnp.zeros((1, LANE - 2 * SSD_HEADS), F32)
    return dict(
        conv_w=conv_w, conv_b=conv_b,
        alog_row=jnp.concatenate([a_log.reshape(1, -1), pad], axis=1),
        dtb_row=jnp.concatenate([dt_bias.reshape(1, -1), pad], axis=1),
        dskip_row=jnp.repeat(d_skip, HEAD_DIM).reshape(1, SSD_D_INNER),
        g_row=norm_g.reshape(1, SSD_D_INNER),
        sink_rows=jnp.repeat(sink, HEAD_DIM).reshape(4, 1, LANE),
        bias_tab=_na_bias_tables(rpb, rows),
        cos_tab=cos_tab, sin_tab=sin_tab)


def _mixer_branches(p3, pc3, cols, prm, ctx_out):
    bsz, n, npad = p3.shape
    nctx = pc3.shape[1]
    p2 = p3.reshape(bsz * n, npad)
    pc2 = pc3.reshape(bsz * nctx, npad)
    flat = lambda t: t.reshape(-1, t.shape[-1])

    u3 = _conv_silu(p3, cols["xbc"], SSD_XBC, prm["conv_w"], prm["conv_b"])
    uc3 = _conv_silu(pc3, cols["xbc"], SSD_XBC, prm["conv_w"], prm["conv_b"])
    h_zero = jnp.zeros((bsz, 2, SSD_STATE, SSD_D_INNER), F32)
    yfc, ybc, h_ctx = _ssd_scan(uc3, pc3, cols["dt"], prm["alog_row"], prm["dtb_row"], h_zero)
    yf, yb, _ = _ssd_scan(u3, p3, cols["dt"], prm["alog_row"], prm["dtb_row"], h_ctx)
    y_ssd = _ssd_finish(flat(yf), flat(yb), flat(u3), p2, cols["z"], prm["dskip_row"], prm["g_row"])
    y_na = flat(_na_attention(p3, pc3, cols["nq"], cols["nk"], cols["nv"], prm["bias_tab"]))
    y_swa = flat(_swa_attention(p3, pc3, cols["sq"], cols["skv"], prm["cos_tab"], prm["sin_tab"],
                                prm["sink_rows"]))
    y_fn = flat(_fn_position(_fn_channel(p2, cols["fn"], 512).reshape(bsz, n, -1)))
    ys = [y_ssd, y_na, y_swa, y_fn]
    if not ctx_out:
        return ys, None
    yc_ssd = _ssd_finish(flat(yfc), flat(ybc), flat(uc3), pc2, cols["z"], prm["dskip_row"], prm["g_row"])
    yc_na = flat(_ctx_attention(pc3, cols["nq"], cols["nk"], cols["nv"], prm["sink_rows"], False))
    yc_swa = flat(_ctx_attention(pc3, cols["sq"], cols["skv"], cols["skv"] + LANE, prm["sink_rows"], True))
    yc_fn = flat(_fn_position(_fn_channel(pc2, cols["fn"], 512).reshape(bsz, nctx, -1)))
    return ys, [yc_ssd, yc_na, yc_swa, yc_fn]


def kernel(x, c, ctx, c_ctx, w_ada, b_ada, w_in, b_gate, ssd_conv_w, ssd_conv_b, ssd_a_log,
           ssd_dt_bias, ssd_d, ssd_norm_g, na_rpb, swa_sink, w_branch, w_out, ln1_g, ln1_b,
           ln2_g, ln2_b, ffn_w_up, ffn_conv_w, ffn_conv_b, ffn_w_down):
    bsz, n, d = x.shape
    nctx = ctx.shape[1]
    depth = w_ada.shape[0]
    dff = ffn_w_down.shape[1]
    alpha = (2.0 * depth) ** 0.25
    rows = n // GRID_W

    g_col = 0
    z_col = N_BRANCH * d
    fn_col = z_col + 512
    sq_col = fn_col + 512
    nq_col = sq_col + 512
    nk_col = nq_col + 512
    nv_col = nk_col + 512
    xbc_col = nv_col + 512
    skv_col = xbc_col + SSD_XBC
    dt_col = skv_col + 256
    n_used = dt_col + LANE
    n_in = -(-n_used // 256) * 256
    tn_in = _pick(n_in, (1280, 1024, 768, 512, 256))

    o_z, o_xbc, o_dt = 0, 512, 512 + SSD_XBC
    o_na = o_dt + 2 * SSD_HEADS
    o_sq = o_na + 3 * 512
    o_skv = o_sq + 512
    o_fn = o_skv + 256
    o_g = o_fn + 512

    def permute_w_in(w):
        parts = [w[:, o_g:o_g + N_BRANCH * d], w[:, o_z:o_z + 512], w[:, o_fn:o_fn + 512],
                 w[:, o_sq:o_sq + 512], w[:, o_na:o_na + 3 * 512], w[:, o_xbc:o_xbc + SSD_XBC],
                 w[:, o_skv:o_skv + 256], w[:, o_dt:o_dt + 2 * SSD_HEADS],
                 jnp.zeros((d, n_in - n_used + LANE - 2 * SSD_HEADS), w.dtype)]
        return jnp.concatenate(parts, axis=1).astype(BF16)

    n_rows = -(-(bsz + 1) // 8) * 8
    cvec = jnp.concatenate([c, c_ctx[None], jnp.zeros((n_rows - bsz - 1, d), F32)], axis=0)
    mods = _ada_mods(cvec, w_ada, b_ada)

    tm_lat = _pick(n, (1024, 512, 256))
    tm_ctx = _pick(bsz * nctx, (1024, 512, 256))
    tm_lat_k = _pick(n, (512, 256))
    tm_ctx_k = _pick(bsz * nctx, (512, 256))
    lat_row = _ModRow(tm_lat, seq=n)
    ctx_row = _ModRow(tm_ctx, fixed=bsz)
    lat_row_k = _ModRow(tm_lat_k, seq=n)
    ctx_row_k = _ModRow(tm_ctx_k, fixed=bsz)
    tn_up = _pick(dff, (1408, 1024, 512, 256, 128))
    tk_down = _pick(dff, (1408, 1024, 512, 256, 128))
    tk_out = _pick(d, (2048, 1024, 512, 256))

    cos_tab, sin_tab = _rope_tables(n)

    x2 = x.reshape(bsz * n, d)
    xc2 = ctx.reshape(bsz * nctx, d)
    for l in range(depth):
        ctx_out = l < depth - 1
        mod3 = mods[l].reshape(n_rows, 1, 6 * d)
        w_in_p = permute_w_in(w_in[l])
        wb = w_branch[l].astype(BF16)
        bg3 = b_gate[l].reshape(N_BRANCH, 1, d)
        w_out_b = w_out[l].astype(BF16)
        w_up_b = ffn_w_up[l].astype(BF16)
        w_down_b = ffn_w_down[l].astype(BF16)
        prm = _branch_params(ssd_conv_w[l], ssd_conv_b[l], ssd_a_log[l], ssd_dt_bias[l], ssd_d[l],
                             ssd_norm_g[l], na_rpb[l], swa_sink[l], rows, cos_tab, sin_tab)

        p2 = _lnmod_matmul(x2, mod3, lat_row, 0, 1, w_in_p, tn_in)
        pc2 = _lnmod_matmul(xc2, mod3, ctx_row, 0, 1, w_in_p, tn_in)
        p3 = p2.reshape(bsz, n, n_in)
        pc3 = pc2.reshape(bsz, nctx, n_in)

        cols = dict(z=z_col, fn=fn_col, sq=sq_col, nq=nq_col, nk=nk_col, nv=nv_col, xbc=xbc_col,
                    skv=skv_col, dt=dt_col)
        ys, ycs = _mixer_branches(p3, pc3, cols, prm, ctx_out)

        acc = _merge(ys, p2, g_col, wb, bg3)
        x_mid = _mm_resln(acc, w_out_b, x2, mod3, lat_row_k, 2, ln1_g[l], ln1_b[l], alpha, tk_out)
        gu = _lnmod_matmul(x_mid, mod3, lat_row, 3, 4, w_up_b, tn_up)
        act = _conv_silu(gu.reshape(bsz, n, 2 * dff), 0, dff, ffn_conv_w[l], ffn_conv_b[l], mul_col0=dff)
        x2 = _mm_resln(act.reshape(bsz * n, dff), w_down_b, x_mid, mod3, lat_row_k, 5,
                       ln2_g[l], ln2_b[l], alpha, tk_down)

        if ctx_out:
            acc_c = _merge(ycs, pc2, g_col, wb, bg3)
            xc_mid = _mm_resln(acc_c, w_out_b, xc2, mod3, ctx_row_k, 2, ln1_g[l], ln1_b[l], alpha, tk_out)
            gu_c = _lnmod_matmul(xc_mid, mod3, ctx_row, 3, 4, w_up_b, tn_up)
            act_c = _conv_silu(gu_c.reshape(bsz, nctx, 2 * dff), 0, dff, ffn_conv_w[l], ffn_conv_b[l],
                               mul_col0=dff)
            xc2 = _mm_resln(act_c.reshape(bsz * nctx, dff), w_down_b, xc_mid, mod3, ctx_row_k, 5,
                            ln2_g[l], ln2_b[l], alpha, tk_down)

    return x2.reshape(bsz, n, d)
```

```python
import functools
import math

import numpy as np
import jax
import jax.numpy as jnp
from jax import lax
from jax.experimental import pallas as pl
from jax.experimental.pallas import tpu as pltpu

F32 = jnp.float32
BF16 = jnp.bfloat16

GRID_W = 64
HEAD_DIM = 64
SSD_HEADS = 8
SSD_D_INNER = 512
SSD_STATE = 128
SSD_GROUPS = 2
SSD_CHUNK = 128
SSD_XBC = 1024
NA_WIN_ROWS = 8
NA_WIN_COLS = 16
NA_Q_ROWS = 4
NA_K_ROWS = 12
SWA_WINDOW = 128
SWA_Q = 256
SWA_K = 512
FN_GROUP_DIM = 128
N_BRANCH = 4
BRANCH_WIDTH = 512
ROPE_BASE = 10000.0
LN_EPS = 1e-6
NEG = -1e30
DFT_ROWS = 64

LANE = 128
VMEM_LIMIT = 56 * 1024 * 1024


def _pick(dim, prefs):
    for p in prefs:
        if p <= dim and dim % p == 0:
            return p
    return dim


def _params(sem):
    return pltpu.CompilerParams(dimension_semantics=sem, vmem_limit_bytes=VMEM_LIMIT)


def _layer_norm(v):
    mu = jnp.mean(v, axis=-1, keepdims=True)
    vc = v - mu
    var = jnp.mean(vc * vc, axis=-1, keepdims=True)
    return vc * lax.rsqrt(var + LN_EPS)


def _silu(v):
    return v * jax.nn.sigmoid(v)


def _ada_kernel(c_ref, w_ref, b_ref, o_ref):
    s = _silu(c_ref[...])
    o_ref[...] = jnp.dot(s.astype(BF16), w_ref[...].astype(BF16),
                         preferred_element_type=F32) + b_ref[...]


def _ada_mods(cvec, w_ada, b_ada):
    depth, d, n6 = w_ada.shape
    r = cvec.shape[0]
    tn = _pick(n6, (1024, 512, 256, 128))
    return pl.pallas_call(
        _ada_kernel,
        grid=(depth, n6 // tn),
        in_specs=[pl.BlockSpec((r, d), lambda l, j: (0, 0)),
                  pl.BlockSpec((None, d, tn), lambda l, j: (l, 0, j)),
                  pl.BlockSpec((None, 1, tn), lambda l, j: (l, 0, j))],
        out_specs=pl.BlockSpec((None, r, tn), lambda l, j: (l, 0, j)),
        out_shape=jax.ShapeDtypeStruct((depth, r, n6), F32),
        compiler_params=_params(("parallel", "parallel")),
        name="ada_mods",
    )(cvec, w_ada, b_ada.reshape(depth, 1, n6))


def _lnmod_mm_kernel(x_ref, sh_ref, sc_ref, w_ref, o_ref, h_ref):
    @pl.when(pl.program_id(1) == 0)
    def _():
        y = _layer_norm(x_ref[...])
        h_ref[...] = (y * (1.0 + sc_ref[...]) + sh_ref[...]).astype(BF16)

    o_ref[...] = jnp.dot(h_ref[...], w_ref[...], preferred_element_type=F32).astype(o_ref.dtype)


def _lnmod_matmul(x2, mod3, mod_row, sh_chunk, sc_chunk, w, tn):
    m, d = x2.shape
    n = w.shape[1]
    tm = mod_row.tm
    return pl.pallas_call(
        _lnmod_mm_kernel,
        grid=(m // tm, n // tn),
        in_specs=[pl.BlockSpec((tm, d), lambda i, j: (i, 0)),
                  pl.BlockSpec((None, 1, d), lambda i, j: (mod_row(i), 0, sh_chunk)),
                  pl.BlockSpec((None, 1, d), lambda i, j: (mod_row(i), 0, sc_chunk)),
                  pl.BlockSpec((d, tn), lambda i, j: (0, j))],
        out_specs=pl.BlockSpec((tm, tn), lambda i, j: (i, j)),
        out_shape=jax.ShapeDtypeStruct((m, n), BF16),
        scratch_shapes=[pltpu.VMEM((tm, d), BF16)],
        compiler_params=_params(("parallel", "arbitrary")),
        name="lnmod_matmul",
    )(x2, mod3, mod3, w)


class _ModRow:
    def __init__(self, tm, seq=None, fixed=None):
        self.tm, self.seq, self.fixed = tm, seq, fixed

    def __call__(self, i):
        if self.fixed is not None:
            return self.fixed
        return (i * self.tm) // self.seq


def _dwconv_rows(x, w_ref):
    n = x.shape[0]
    k = w_ref.shape[0]
    row = lax.broadcasted_iota(jnp.int32, x.shape, 0)
    acc = x * w_ref[k // 2:k // 2 + 1, :]
    for t in range(k):
        off = t - k // 2
        if off == 0:
            continue
        shifted = pltpu.roll(x, (-off) % n, 0)
        valid = (row + off >= 0) & (row + off < n)
        acc = acc + jnp.where(valid, shifted, 0.0) * w_ref[t:t + 1, :]
    return acc


def _conv_silu_kernel(x_ref, w_ref, b_ref, o_ref):
    y = _dwconv_rows(x_ref[...].astype(F32), w_ref) + b_ref[...]
    o_ref[...] = _silu(y).astype(o_ref.dtype)


def _conv_silu(src3, col0, width, conv_w, conv_b):
    b, n, _ = src3.shape
    tc = LANE
    c0 = col0 // tc
    kk = conv_w.shape[0]
    return pl.pallas_call(
        _conv_silu_kernel,
        grid=(b, width // tc),
        in_specs=[pl.BlockSpec((None, n, tc), lambda bi, j: (bi, 0, c0 + j)),
                  pl.BlockSpec((kk, tc), lambda bi, j: (0, j)),
                  pl.BlockSpec((1, tc), lambda bi, j: (0, j))],
        out_specs=pl.BlockSpec((None, n, tc), lambda bi, j: (bi, 0, j)),
        out_shape=jax.ShapeDtypeStruct((b, n, width), BF16),
        compiler_params=_params(("parallel", "parallel")),
        name="conv_silu",
    )(src3, conv_w, conv_b.reshape(1, width))


def _softplus(v):
    return jnp.maximum(v, 0.0) + jnp.log(1.0 + jnp.exp(-jnp.abs(v)))


def _ssd_direction(u_ref, dt_ref, a_row, dtb_row, s_ref, y_ref, d, reverse):
    q = SSD_CHUNK
    dt_all = _softplus(dt_ref[...].astype(F32) + dtb_row)
    cum = dt_all * a_row
    row = lax.broadcasted_iota(jnp.int32, (q, LANE), 0)
    s = 1
    while s < q:
        if reverse:
            cum = cum + jnp.where(row < q - s, pltpu.roll(cum, q - s, 0), 0.0)
        else:
            cum = cum + jnp.where(row >= s, pltpu.roll(cum, s, 0), 0.0)
        s *= 2
    tot = cum[0:1, :] if reverse else cum[q - 1:q, :]
    cum_t = cum.T
    dt_t = dt_all.T
    e_cum = jnp.exp(cum)
    w_end = jnp.exp(tot - cum) * dt_all
    e_tot = jnp.exp(tot)

    ri = lax.broadcasted_iota(jnp.int32, (q, q), 0)
    ci = lax.broadcasted_iota(jnp.int32, (q, q), 1)
    tri = (ri <= ci) if reverse else (ri >= ci)
    lane = lax.broadcasted_iota(jnp.int32, (q, LANE), 1)
    lo = lane < HEAD_DIM
    lane_row = lax.broadcasted_iota(jnp.int32, (1, LANE), 1)

    def pair_cols(v, c0, c1):
        return jnp.where(lo, jnp.broadcast_to(v[:, c0:c0 + 1], (q, LANE)),
                         jnp.broadcast_to(v[:, c1:c1 + 1], (q, LANE)))

    heads_per_group = SSD_HEADS // SSD_GROUPS
    for g in range(SSD_GROUPS):
        b0 = SSD_D_INNER + g * SSD_STATE
        c0 = SSD_D_INNER + SSD_GROUPS * SSD_STATE + g * SSD_STATE
        bg = u_ref[:, b0:b0 + SSD_STATE]
        cg = u_ref[:, c0:c0 + SSD_STATE]
        cb = lax.dot_general(cg, bg, (((1,), (1,)), ((), ())), preferred_element_type=F32)
        bg_t = bg.astype(F32).T.astype(BF16)
        for hp in range(heads_per_group // 2):
            h0 = g * heads_per_group + 2 * hp
            col0, col1 = d * SSD_HEADS + h0, d * SSD_HEADS + h0 + 1
            l0 = h0 * HEAD_DIM
            xp = u_ref[:, l0:l0 + LANE]
            ws = []
            for col in (col0, col1):
                seg = cum[:, col:col + 1] - cum_t[col:col + 1, :]
                dec = jnp.exp(jnp.where(tri, seg, NEG))
                ws.append((dec * cb * dt_t[col:col + 1, :]).astype(BF16))
            w_pair = jnp.concatenate(ws, axis=1)
            zero = jnp.zeros_like(xp)
            x_bd = jnp.concatenate([jnp.where(lo, xp, zero), jnp.where(lo, zero, xp)], axis=0)
            y_diag = jnp.dot(w_pair, x_bd, preferred_element_type=F32)
            st = s_ref[d, :, l0:l0 + LANE]
            y_off = jnp.dot(cg, st.astype(BF16), preferred_element_type=F32) * pair_cols(e_cum, col0, col1)
            y_ref[:, l0:l0 + LANE] = y_diag + y_off
            xw = (xp.astype(F32) * pair_cols(w_end, col0, col1)).astype(BF16)
            upd = jnp.dot(bg_t, xw, preferred_element_type=F32)
            tot_pair = jnp.where(lane_row < HEAD_DIM,
                                 jnp.broadcast_to(e_tot[:, col0:col0 + 1], (1, LANE)),
                                 jnp.broadcast_to(e_tot[:, col1:col1 + 1], (1, LANE)))
            s_ref[d, :, l0:l0 + LANE] = st * tot_pair + upd


def _ssd_kernel(uf_ref, ub_ref, dtf_ref, dtb_ref, alog_ref, dtbias_ref, h0_ref,
                yf_ref, yb_ref, hT_ref, s_ref):
    t = pl.program_id(1)

    @pl.when(t == 0)
    def _():
        s_ref[...] = h0_ref[...]

    a_row = -jnp.exp(alog_ref[...])
    dtb_row = dtbias_ref[...]
    _ssd_direction(uf_ref, dtf_ref, a_row, dtb_row, s_ref, yf_ref, 0, False)
    _ssd_direction(ub_ref, dtb_ref, a_row, dtb_row, s_ref, yb_ref, 1, True)

    @pl.when(t == pl.num_programs(1) - 1)
    def _():
        hT_ref[...] = s_ref[...]


def _ssd_scan(u3, p3, dt_col0, alog_row, dtbias_row, h0):
    b, n, _ = u3.shape
    q = SSD_CHUNK
    nt = n // q
    dtb = dt_col0 // LANE
    hp = SSD_D_INNER
    return pl.pallas_call(
        _ssd_kernel,
        grid=(b, nt),
        in_specs=[pl.BlockSpec((None, q, SSD_XBC), lambda bi, t: (bi, t, 0)),
                  pl.BlockSpec((None, q, SSD_XBC), lambda bi, t: (bi, nt - 1 - t, 0)),
                  pl.BlockSpec((None, q, LANE), lambda bi, t: (bi, t, dtb)),
                  pl.BlockSpec((None, q, LANE), lambda bi, t: (bi, nt - 1 - t, dtb)),
                  pl.BlockSpec((1, LANE), lambda bi, t: (0, 0)),
                  pl.BlockSpec((1, LANE), lambda bi, t: (0, 0)),
                  pl.BlockSpec((None, 2, SSD_STATE, hp), lambda bi, t: (bi, 0, 0, 0))],
        out_specs=[pl.BlockSpec((None, q, hp), lambda bi, t: (bi, t, 0)),
                   pl.BlockSpec((None, q, hp), lambda bi, t: (bi, nt - 1 - t, 0)),
                   pl.BlockSpec((None, 2, SSD_STATE, hp), lambda bi, t: (bi, 0, 0, 0))],
        out_shape=[jax.ShapeDtypeStruct((b, n, hp), F32),
                   jax.ShapeDtypeStruct((b, n, hp), F32),
                   jax.ShapeDtypeStruct((b, 2, SSD_STATE, hp), F32)],
        scratch_shapes=[pltpu.VMEM((2, SSD_STATE, hp), F32)],
        compiler_params=_params(("parallel", "arbitrary")),
        name="ssd_scan",
    )(u3, u3, p3, p3, alog_row, dtbias_row, h0)


def _ssd_finish_kernel(yf_ref, yb_ref, x_ref, z_ref, dskip_ref, g_ref, o_ref):
    y = dskip_ref[...] * x_ref[...].astype(F32) + yf_ref[...] + yb_ref[...]
    y = y * _silu(z_ref[...].astype(F32))
    r = lax.rsqrt(jnp.mean(y * y, axis=-1, keepdims=True) + LN_EPS)
    o_ref[...] = (y * r * g_ref[...]).astype(o_ref.dtype)


def _ssd_finish(yf2, yb2, u2, p2, z_col0, dskip_row, g_row):
    m, w = yf2.shape
    tm = _pick(m, (1024, 512, 256, 128))
    zb = z_col0 // w
    row = lambda i: (i, 0)
    return pl.pallas_call(
        _ssd_finish_kernel,
        grid=(m // tm,),
        in_specs=[pl.BlockSpec((tm, w), row), pl.BlockSpec((tm, w), row),
                  pl.BlockSpec((tm, w), row),
                  pl.BlockSpec((tm, w), lambda i: (i, zb)),
                  pl.BlockSpec((1, w), lambda i: (0, 0)),
                  pl.BlockSpec((1, w), lambda i: (0, 0))],
        out_specs=pl.BlockSpec((tm, w), row),
        out_shape=jax.ShapeDtypeStruct((m, w), BF16),
        compiler_params=_params(("parallel",)),
        name="ssd_finish",
    )(yf2, yb2, u2, p2, dskip_row, g_row)


def _block_diag_rows(kv):
    lane = lax.broadcasted_iota(jnp.int32, kv.shape, 1)
    lo = lane < HEAD_DIM
    zero = jnp.zeros_like(kv)
    return jnp.concatenate([jnp.where(lo, kv, zero), jnp.where(lo, zero, kv)], axis=0)


def _dup_group(kv, g):
    lane = lax.broadcasted_iota(jnp.int32, kv.shape, 1)
    rolled = pltpu.roll(kv, HEAD_DIM, 1)
    return jnp.where(lane // HEAD_DIM == g, kv, rolled)


def _qk(q, kbd):
    return lax.dot_general(q, kbd, (((1,), (1,)), ((), ())), preferred_element_type=F32)


def _pair_softmax_pv(s_w, s_c, bias, vbd, vcbd, sink, out_dtype):
    nk = s_w.shape[1] // 2
    nc = s_c.shape[1] // 2
    tq = s_w.shape[0]
    pws, pcs, invs = [], [], []
    for h in range(2):
        sw = s_w[:, h * nk:(h + 1) * nk] + bias[h]
        sc = s_c[:, h * nc:(h + 1) * nc]
        m = jnp.maximum(jnp.max(sw, axis=-1, keepdims=True), jnp.max(sc, axis=-1, keepdims=True))
        if sink is not None:
            m = jnp.maximum(m, sink[h])
        pw = jnp.exp(sw - m)
        pc = jnp.exp(sc - m)
        l = jnp.sum(pw, axis=-1, keepdims=True) + jnp.sum(pc, axis=-1, keepdims=True)
        if sink is not None:
            l = l + jnp.exp(sink[h] - m)
        pws.append(pw.astype(BF16))
        pcs.append(pc.astype(BF16))
        invs.append(1.0 / l)
    o = jnp.dot(jnp.concatenate(pws, axis=1), vbd, preferred_element_type=F32)
    o = o + jnp.dot(jnp.concatenate(pcs, axis=1), vcbd, preferred_element_type=F32)
    lane = lax.broadcasted_iota(jnp.int32, (tq, LANE), 1)
    inv = jnp.where(lane < HEAD_DIM, jnp.broadcast_to(invs[0], (tq, LANE)),
                    jnp.broadcast_to(invs[1], (tq, LANE)))
    return (o * inv).astype(out_dtype)


def _na_kernel(q_ref, k_ref, v_ref, kc_ref, vc_ref, bias_ref, o_ref, *, rows):
    qb = pl.program_id(2)
    nk = NA_K_ROWS * GRID_W
    ks = jnp.clip(NA_Q_ROWS * qb - (NA_K_ROWS - NA_Q_ROWS) // 2, 0, rows - NA_K_ROWS) * GRID_W
    ks = pl.multiple_of(ks, GRID_W)
    for p in range(q_ref.shape[1] // LANE):
        cs = slice(p * LANE, (p + 1) * LANE)
        q = q_ref[:, cs] * (HEAD_DIM ** -0.5)
        kbd = _block_diag_rows(k_ref[pl.ds(ks, nk), cs])
        vbd = _block_diag_rows(v_ref[pl.ds(ks, nk), cs])
        kcbd = _block_diag_rows(kc_ref[:, cs])
        vcbd = _block_diag_rows(vc_ref[:, cs])
        s_w = _qk(q, kbd)
        s_c = _qk(q, kcbd)
        o_ref[:, cs] = _pair_softmax_pv(s_w, s_c, [bias_ref[2 * p], bias_ref[2 * p + 1]], vbd, vcbd,
                                        None, o_ref.dtype)


def _na_key_start(qb, rows):
    return min(max(NA_Q_ROWS * qb - (NA_K_ROWS - NA_Q_ROWS) // 2, 0), rows - NA_K_ROWS)


def _na_bias_tables(rpb, rows):
    nqb = rows // NA_Q_ROWS
    variants = [0, 1 if nqb > 2 else 0, nqb - 1]
    nh = rpb.shape[0]
    w = GRID_W
    pad = w - NA_WIN_COLS
    rp = jnp.pad(rpb.astype(F32), ((0, 0), (0, 0), (pad, pad)))
    toep = jnp.stack([rp[:, :, w - 1 - qc:2 * w - 1 - qc] for qc in range(w)], axis=2)
    qc, kc = np.arange(w)[:, None], np.arange(w)[None, :]
    ws = np.clip(qc - NA_WIN_COLS // 2, 0, w - NA_WIN_COLS)
    toep = jnp.where(jnp.asarray((kc >= ws) & (kc < ws + NA_WIN_COLS)), toep, NEG)
    masked = jnp.full((nh, w, w), NEG, F32)
    tabs = []
    for qb in variants:
        block_rows = []
        for qr_l in range(NA_Q_ROWS):
            qr = NA_Q_ROWS * qb + qr_l
            rs = min(max(qr - NA_WIN_ROWS // 2, 0), rows - NA_WIN_ROWS)
            blocks = []
            for kr_l in range(NA_K_ROWS):
                kr = _na_key_start(qb, rows) + kr_l
                inside = rs <= kr < rs + NA_WIN_ROWS
                blocks.append(toep[:, kr - qr + NA_WIN_ROWS - 1] if inside else masked)
            block_rows.append(jnp.concatenate(blocks, axis=-1))
        tabs.append(jnp.concatenate(block_rows, axis=-2))
    return jnp.stack(tabs, axis=0)


def _na_attention(p3, pc3, q_col0, k_col0, v_col0, bias_tab):
    b, n, _ = p3.shape
    nc = pc3.shape[1]
    rows = n // GRID_W
    tq = NA_Q_ROWS * GRID_W
    nk = NA_K_ROWS * GRID_W
    nqb = n // tq
    wb = 2 * LANE
    qc, kc, vc = q_col0 // wb, k_col0 // wb, v_col0 // wb
    ngrp = 2

    def variant(qb):
        return jnp.where(qb == 0, 0, jnp.where(qb == nqb - 1, 2, 1))

    return pl.pallas_call(
        functools.partial(_na_kernel, rows=rows),
        grid=(b, ngrp, nqb),
        in_specs=[pl.BlockSpec((None, tq, wb), lambda bi, h, i: (bi, i, qc + h)),
                  pl.BlockSpec((None, n, wb), lambda bi, h, i: (bi, 0, kc + h)),
                  pl.BlockSpec((None, n, wb), lambda bi, h, i: (bi, 0, vc + h)),
                  pl.BlockSpec((None, nc, wb), lambda bi, h, i: (bi, 0, kc + h)),
                  pl.BlockSpec((None, nc, wb), lambda bi, h, i: (bi, 0, vc + h)),
                  pl.BlockSpec((None, 4, tq, nk), lambda bi, h, i: (variant(i), h, 0, 0))],
        out_specs=pl.BlockSpec((None, tq, wb), lambda bi, h, i: (bi, i, h)),
        out_shape=jax.ShapeDtypeStruct((b, n, ngrp * wb), BF16),
        compiler_params=_params(("parallel", "parallel", "arbitrary")),
        name="na_attention",
    )(p3, p3, p3, pc3, pc3, bias_tab)


def _rope(x, cos, sin_signed):
    lane = lax.broadcasted_iota(jnp.int32, x.shape, 1)
    first = (lane % 32) < 16
    partner = jnp.where(first, pltpu.roll(x, LANE - 16, 1), pltpu.roll(x, 16, 1))
    return x * cos + partner * sin_signed


def _sink_pair(sink_ref):
    s = sink_ref[...]
    return [s[:, 0:1], s[:, HEAD_DIM:HEAD_DIM + 1]]


def _swa_kernel(q_ref, k_ref, v_ref, kc_ref, vc_ref, cos_ref, sin_ref, sink_ref, o_ref, *, n):
    g = pl.program_id(1)
    qb = pl.program_id(2)
    q0 = pl.multiple_of(qb * SWA_Q, SWA_Q)
    ks = pl.multiple_of(jnp.clip(qb * SWA_Q - SWA_WINDOW, 0, n - SWA_K), SWA_WINDOW)
    cos_q, sin_q = cos_ref[pl.ds(q0, SWA_Q), :], sin_ref[pl.ds(q0, SWA_Q), :]
    kk = _rope(k_ref[pl.ds(ks, SWA_K), :].astype(F32), cos_ref[pl.ds(ks, SWA_K), :],
               sin_ref[pl.ds(ks, SWA_K), :])
    kbd = _block_diag_rows(_dup_group(kk, g).astype(BF16))
    vbd = _block_diag_rows(_dup_group(v_ref[pl.ds(ks, SWA_K), :].astype(F32), g).astype(BF16))
    kcbd = _block_diag_rows(_dup_group(kc_ref[...].astype(F32), g).astype(BF16))
    vcbd = _block_diag_rows(_dup_group(vc_ref[...].astype(F32), g).astype(BF16))
    qpos = q0 + lax.broadcasted_iota(jnp.int32, (SWA_Q, SWA_K), 0)
    kpos = ks + lax.broadcasted_iota(jnp.int32, (SWA_Q, SWA_K), 1)
    bias = jnp.where(jnp.abs(kpos - qpos) <= SWA_WINDOW, 0.0, NEG)
    for p in range(q_ref.shape[1] // LANE):
        cs = slice(p * LANE, (p + 1) * LANE)
        q = _rope(q_ref[:, cs].astype(F32), cos_q, sin_q)
        q = (q * (HEAD_DIM ** -0.5)).astype(BF16)
        s_w = _qk(q, kbd)
        s_c = _qk(q, kcbd)
        o_ref[:, cs] = _pair_softmax_pv(s_w, s_c, [bias, bias], vbd, vcbd, _sink_pair(sink_ref.at[p]),
                                        o_ref.dtype)


def _swa_attention(p3, pc3, q_col0, kv_col0, cos_tab, sin_tab, sink_rows):
    b, n, _ = p3.shape
    nc = pc3.shape[1]
    nqb = n // SWA_Q
    wb = 2 * LANE
    qc, kc = q_col0 // wb, kv_col0 // LANE
    ngrp = 2
    return pl.pallas_call(
        functools.partial(_swa_kernel, n=n),
        grid=(b, ngrp, nqb),
        in_specs=[pl.BlockSpec((None, SWA_Q, wb), lambda bi, h, i: (bi, i, qc + h)),
                  pl.BlockSpec((None, n, LANE), lambda bi, h, i: (bi, 0, kc)),
                  pl.BlockSpec((None, n, LANE), lambda bi, h, i: (bi, 0, kc + 1)),
                  pl.BlockSpec((None, nc, LANE), lambda bi, h, i: (bi, 0, kc)),
                  pl.BlockSpec((None, nc, LANE), lambda bi, h, i: (bi, 0, kc + 1)),
                  pl.BlockSpec((n, LANE), lambda bi, h, i: (0, 0)),
                  pl.BlockSpec((n, LANE), lambda bi, h, i: (0, 0)),
                  pl.BlockSpec((2, 1, LANE), lambda bi, h, i: (h, 0, 0))],
        out_specs=pl.BlockSpec((None, SWA_Q, wb), lambda bi, h, i: (bi, i, h)),
        out_shape=jax.ShapeDtypeStruct((b, n, ngrp * wb), BF16),
        compiler_params=_params(("parallel", "parallel", "arbitrary")),
        name="swa_attention",
    )(p3, p3, p3, pc3, pc3, cos_tab, sin_tab, sink_rows)


def _ctx_attn_kernel(q_ref, k_ref, v_ref, sink_ref, o_ref, *, grouped):
    hp = pl.program_id(1)
    q = q_ref[...] * (HEAD_DIM ** -0.5)
    k, v = k_ref[...], v_ref[...]
    if grouped:
        g = hp // 2
        k = _dup_group(k.astype(F32), g).astype(BF16)
        v = _dup_group(v.astype(F32), g).astype(BF16)
    kbd, vbd = _block_diag_rows(k), _block_diag_rows(v)
    s = _qk(q, kbd)
    t = q.shape[0]
    sink = _sink_pair(sink_ref) if grouped else None
    ps, invs = [], []
    for h in range(2):
        sh = s[:, h * t:(h + 1) * t]
        m = jnp.max(sh, axis=-1, keepdims=True)
        if sink is not None:
            m = jnp.maximum(m, sink[h])
        p = jnp.exp(sh - m)
        l = jnp.sum(p, axis=-1, keepdims=True)
        if sink is not None:
            l = l + jnp.exp(sink[h] - m)
        ps.append(p.astype(BF16))
        invs.append(1.0 / l)
    o = jnp.dot(jnp.concatenate(ps, axis=1), vbd, preferred_element_type=F32)
    lane = lax.broadcasted_iota(jnp.int32, (t, LANE), 1)
    inv = jnp.where(lane < HEAD_DIM, jnp.broadcast_to(invs[0], (t, LANE)),
                    jnp.broadcast_to(invs[1], (t, LANE)))
    o_ref[...] = (o * inv).astype(o_ref.dtype)


def _ctx_attention(pc3, q_col0, k_col0, v_col0, sink_rows, grouped):
    b, nc, _ = pc3.shape
    qc, kc, vc = q_col0 // LANE, k_col0 // LANE, v_col0 // LANE
    npair = 4
    kv_blk = (lambda h: 0) if grouped else (lambda h: h)
    return pl.pallas_call(
        functools.partial(_ctx_attn_kernel, grouped=grouped),
        grid=(b, npair),
        in_specs=[pl.BlockSpec((None, nc, LANE), lambda bi, h: (bi, 0, qc + h)),
                  pl.BlockSpec((None, nc, LANE), lambda bi, h: (bi, 0, kc + kv_blk(h))),
                  pl.BlockSpec((None, nc, LANE), lambda bi, h: (bi, 0, vc + kv_blk(h))),
                  pl.BlockSpec((None, 1, LANE), lambda bi, h: (h, 0, 0))],
        out_specs=pl.BlockSpec((None, nc, LANE), lambda bi, h: (bi, 0, h)),
        out_shape=jax.ShapeDtypeStruct((b, nc, npair * LANE), BF16),
        compiler_params=_params(("parallel", "parallel")),
        name="ctx_attention",
    )(pc3, pc3, pc3, sink_rows)


def _fn_channel_kernel(x_ref, cs_ref, o_ref):
    w = x_ref.shape[1]
    ngroups = w // FN_GROUP_DIM
    for g in range(ngroups):
        xg = x_ref[:, g * FN_GROUP_DIM:(g + 1) * FN_GROUP_DIM]
        ab = jnp.dot(xg, cs_ref[...], preferred_element_type=F32)
        o_ref[:, g * FN_GROUP_DIM:(g + 1) * FN_GROUP_DIM] = ab[:, :FN_GROUP_DIM].astype(o_ref.dtype)
        o_ref[:, w + g * FN_GROUP_DIM:w + (g + 1) * FN_GROUP_DIM] = ab[:, FN_GROUP_DIM:].astype(o_ref.dtype)


def _fn_channel(p2, col0, width):
    m = p2.shape[0]
    tm = _pick(m, (1024, 512, 256, 128))
    k = np.arange(FN_GROUP_DIM)
    ang = 2.0 * np.pi * ((k[:, None] * k[None, :]) % FN_GROUP_DIM) / FN_GROUP_DIM
    cs = jnp.asarray(np.concatenate([np.cos(ang), np.sin(ang)], axis=1), BF16)
    cb = col0 // width
    return pl.pallas_call(
        _fn_channel_kernel,
        grid=(m // tm,),
        in_specs=[pl.BlockSpec((tm, width), lambda i: (i, cb)),
                  pl.BlockSpec((FN_GROUP_DIM, 2 * FN_GROUP_DIM), lambda i: (0, 0))],
        out_specs=pl.BlockSpec((tm, 2 * width), lambda i: (i, 0)),
        out_shape=jax.ShapeDtypeStruct((m, 2 * width), BF16),
        compiler_params=_params(("parallel",)),
        name="fn_channel",
    )(p2, cs)


def _fn_position_kernel(ac_ref, as_ref, bc_ref, bs_ref, ab_ref, o_ref, c_ref, s_ref, *, scale):
    @pl.when(pl.program_id(1) == 0)
    def _():
        for j in range(ac_ref.shape[0]):
            ca, sa = ac_ref[j:j + 1, :], as_ref[j:j + 1, :]
            cb, sb = bc_ref[...], bs_ref[...]
            c_ref[j * DFT_ROWS:(j + 1) * DFT_ROWS, :] = (ca * cb - sa * sb).astype(BF16)
            s_ref[j * DFT_ROWS:(j + 1) * DFT_ROWS, :] = (sa * cb + ca * sb).astype(BF16)

    w = o_ref.shape[1]
    y = jnp.dot(c_ref[...], ab_ref[:, :w], preferred_element_type=F32)
    y = y - jnp.dot(s_ref[...], ab_ref[:, w:], preferred_element_type=F32)
    o_ref[...] = (y * scale).astype(o_ref.dtype)


def _fn_position(ab3):
    b, n, w2 = ab3.shape
    w = w2 // 2
    tm = _pick(n, (512, 256, 128, 64))
    jc = tm // DFT_ROWS
    n1 = n // DFT_ROWS
    k = np.arange(n, dtype=np.int64)
    j1 = np.arange(n1, dtype=np.int64)
    j2 = np.arange(DFT_ROWS, dtype=np.int64)
    ang_a = 2.0 * np.pi * ((j1[:, None] * DFT_ROWS * k[None, :]) % n) / n
    ang_b = 2.0 * np.pi * ((j2[:, None] * k[None, :]) % n) / n
    ac, as_ = jnp.asarray(np.cos(ang_a), F32), jnp.asarray(np.sin(ang_a), F32)
    bc, bs = jnp.asarray(np.cos(ang_b), F32), jnp.asarray(np.sin(ang_b), F32)
    scale = 1.0 / math.sqrt(n * FN_GROUP_DIM)
    if n1 % 8 != 0 and jc != n1:
        raise ValueError("unsupported sequence length for the position DFT tiling")
    return pl.pallas_call(
        functools.partial(_fn_position_kernel, scale=scale),
        grid=(n // tm, b),
        in_specs=[pl.BlockSpec((jc, n), lambda i, bi: (i, 0)),
                  pl.BlockSpec((jc, n), lambda i, bi: (i, 0)),
                  pl.BlockSpec((DFT_ROWS, n), lambda i, bi: (0, 0)),
                  pl.BlockSpec((DFT_ROWS, n), lambda i, bi: (0, 0)),
                  pl.BlockSpec((None, n, w2), lambda i, bi: (bi, 0, 0))],
        out_specs=pl.BlockSpec((None, tm, w), lambda i, bi: (bi, i, 0)),
        out_shape=jax.ShapeDtypeStruct((b, n, w), BF16),
        scratch_shapes=[pltpu.VMEM((tm, n), BF16), pltpu.VMEM((tm, n), BF16)],
        compiler_params=_params(("parallel", "arbitrary")),
        name="fn_position",
    )(ac, as_, bc, bs, ab3)


def _merge_kernel(y0, y1, y2, y3, g0, g1, g2, g3, wb_ref, bg_ref, o_ref):
    acc = None
    for i, (y, g) in enumerate(((y0, g0), (y1, g1), (y2, g2), (y3, g3))):
        gate = jax.nn.sigmoid(g[...].astype(F32) + bg_ref[i])
        term = gate * jnp.dot(y[...], wb_ref[i], preferred_element_type=F32)
        acc = term if acc is None else acc + term
    o_ref[...] = acc.astype(o_ref.dtype)


def _merge(ys, p2, gate_col0, wb, bg3):
    m = p2.shape[0]
    d = wb.shape[2]
    tm = _pick(m, (1024, 512, 256))
    tn = _pick(d, (1024, 512, 256))
    gb = gate_col0 // tn
    per = d // tn
    bw = ys[0].shape[1]
    y_specs = [pl.BlockSpec((tm, bw), lambda i, j: (i, 0)) for _ in range(N_BRANCH)]
    g_specs = [pl.BlockSpec((tm, tn), functools.partial(lambda i, j, br: (i, gb + br * per + j), br=br))
               for br in range(N_BRANCH)]
    return pl.pallas_call(
        _merge_kernel,
        grid=(m // tm, per),
        in_specs=y_specs + g_specs + [
            pl.BlockSpec((N_BRANCH, bw, tn), lambda i, j: (0, 0, j)),
            pl.BlockSpec((N_BRANCH, 1, tn), lambda i, j: (0, 0, j))],
        out_specs=pl.BlockSpec((tm, tn), lambda i, j: (i, j)),
        out_shape=jax.ShapeDtypeStruct((m, d), BF16),
        compiler_params=_params(("parallel", "parallel")),
        name="branch_merge",
    )(*ys, p2, p2, p2, p2, wb, bg3)


def _mm_resln_kernel(a_ref, w_ref, x_ref, gate_ref, lng_ref, lnb_ref, o_ref, acc_ref, *, alpha):
    k = pl.program_id(1)

    @pl.when(k == 0)
    def _():
        acc_ref[...] = jnp.zeros_like(acc_ref)

    acc_ref[...] += jnp.dot(a_ref[...], w_ref[...], preferred_element_type=F32)

    @pl.when(k == pl.num_programs(1) - 1)
    def _():
        v = alpha * x_ref[...] + gate_ref[...] * acc_ref[...]
        o_ref[...] = _layer_norm(v) * lng_ref[...] + lnb_ref[...]


def _mm_resln(a2, w, x2, mod3, mod_row, gate_chunk, ln_g, ln_b, alpha, tk):
    m, kdim = a2.shape
    d = w.shape[1]
    tm = mod_row.tm
    return pl.pallas_call(
        functools.partial(_mm_resln_kernel, alpha=alpha),
        grid=(m // tm, kdim // tk),
        in_specs=[pl.BlockSpec((tm, tk), lambda i, k: (i, k)),
                  pl.BlockSpec((tk, d), lambda i, k: (k, 0)),
                  pl.BlockSpec((tm, d), lambda i, k: (i, 0)),
                  pl.BlockSpec((None, 1, d), lambda i, k: (mod_row(i), 0, gate_chunk)),
                  pl.BlockSpec((1, d), lambda i, k: (0, 0)),
                  pl.BlockSpec((1, d), lambda i, k: (0, 0))],
        out_specs=pl.BlockSpec((tm, d), lambda i, k: (i, 0)),
        out_shape=jax.ShapeDtypeStruct((m, d), F32),
        scratch_shapes=[pltpu.VMEM((tm, d), F32)],
        compiler_params=_params(("parallel", "arbitrary")),
        name="matmul_residual_ln",
    )(a2, w, x2, mod3, ln_g.reshape(1, d), ln_b.reshape(1, d))


HALO_ROWS = 16


def _ffn_down_kernel(g_ref, u_ref, gp_ref, gn_ref, cw_ref, cb_ref, w_ref, x_ref, gate_ref, lng_ref,
                     lnb_ref, o_ref, acc_ref, *, alpha, seq):
    i = pl.program_id(0)
    k = pl.program_id(1)
    tm = g_ref.shape[0]

    @pl.when(k == 0)
    def _():
        acc_ref[...] = jnp.zeros_like(acc_ref)

    has_prev = jnp.where((i * tm) % seq == 0, 0.0, 1.0)
    has_next = jnp.where(((i + 1) * tm) % seq == 0, 0.0, 1.0)
    g = g_ref[...].astype(F32)
    g_prev = gp_ref[HALO_ROWS - 1:HALO_ROWS, :].astype(F32) * has_prev
    g_next = gn_ref[0:1, :].astype(F32) * has_next
    row = lax.broadcasted_iota(jnp.int32, g.shape, 0)
    above = jnp.where(row == 0, g_prev, pltpu.roll(g, 1, 0))
    below = jnp.where(row == tm - 1, g_next, pltpu.roll(g, tm - 1, 0))
    y = above * cw_ref[0:1, :] + g * cw_ref[1:2, :] + below * cw_ref[2:3, :] + cb_ref[...]
    act = (_silu(y) * u_ref[...].astype(F32)).astype(BF16)
    acc_ref[...] += jnp.dot(act, w_ref[...], preferred_element_type=F32)

    @pl.when(k == pl.num_programs(1) - 1)
    def _():
        v = alpha * x_ref[...] + gate_ref[...] * acc_ref[...]
        o_ref[...] = _layer_norm(v) * lng_ref[...] + lnb_ref[...]


def _ffn_down(gu2, conv_w, conv_b, w, x2, mod3, mod_row, gate_chunk, ln_g, ln_b, alpha, tk, seq):
    m = gu2.shape[0]
    dff, d = w.shape
    tm = mod_row.tm
    if conv_w.shape[0] != 3 or seq % tm != 0 or tm % HALO_ROWS != 0:
        raise ValueError("unsupported ConvFFN tiling")
    ub = dff // tk
    hb = tm // HALO_ROWS
    last_hb = m // HALO_ROWS - 1
    return pl.pallas_call(
        functools.partial(_ffn_down_kernel, alpha=alpha, seq=seq),
        grid=(m // tm, dff // tk),
        in_specs=[pl.BlockSpec((tm, tk), lambda i, k: (i, k)),
                  pl.BlockSpec((tm, tk), lambda i, k: (i, ub + k)),
                  pl.BlockSpec((HALO_ROWS, tk), lambda i, k: (jnp.maximum(i * hb - 1, 0), k)),
                  pl.BlockSpec((HALO_ROWS, tk), lambda i, k: (jnp.minimum((i + 1) * hb, last_hb), k)),
                  pl.BlockSpec((3, tk), lambda i, k: (0, k)),
                  pl.BlockSpec((1, tk), lambda i, k: (0, k)),
                  pl.BlockSpec((tk, d), lambda i, k: (k, 0)),
                  pl.BlockSpec((tm, d), lambda i, k: (i, 0)),
                  pl.BlockSpec((None, 1, d), lambda i, k: (mod_row(i), 0, gate_chunk)),
                  pl.BlockSpec((1, d), lambda i, k: (0, 0)),
                  pl.BlockSpec((1, d), lambda i, k: (0, 0))],
        out_specs=pl.BlockSpec((tm, d), lambda i, k: (i, 0)),
        out_shape=jax.ShapeDtypeStruct((m, d), F32),
        scratch_shapes=[pltpu.VMEM((tm, d), F32)],
        compiler_params=_params(("parallel", "arbitrary")),
        name="ffn_down",
    )(gu2, gu2, gu2, gu2, conv_w, conv_b.reshape(1, dff), w, x2, mod3, ln_g.reshape(1, d), ln_b.reshape(1, d))


def _rope_tables(n):
    t = jnp.arange(n)
    rows = (t // GRID_W).astype(F32)
    cols = (t % GRID_W).astype(F32)
    n_freq = HEAD_DIM // 4
    inv = ROPE_BASE ** (-jnp.arange(n_freq, dtype=F32) / n_freq)
    ar, ac = rows[:, None] * inv, cols[:, None] * inv
    cos = jnp.concatenate([jnp.cos(ar), jnp.cos(ar), jnp.cos(ac), jnp.cos(ac)], axis=1)
    sin = jnp.concatenate([-jnp.sin(ar), jnp.sin(ar), -jnp.sin(ac), jnp.sin(ac)], axis=1)
    return jnp.tile(cos, (1, 2)), jnp.tile(sin, (1, 2))


def _branch_params(conv_w, conv_b, a_log, dt_bias, d_skip, norm_g, rpb, sink, rows, cos_tab, sin_tab):
    pad = jnp.zeros((1, LANE - 2 * SSD_HEADS), F32)
    return dict(
        conv_w=conv_w, conv_b=conv_b,
        alog_row=jnp.concatenate([a_log.reshape(1, -1), pad], axis=1),
        dtb_row=jnp.concatenate([dt_bias.reshape(1, -1), pad], axis=1),
        dskip_row=jnp.repeat(d_skip, HEAD_DIM).reshape(1, SSD_D_INNER),
        g_row=norm_g.reshape(1, SSD_D_INNER),
        sink_rows=jnp.repeat(sink, HEAD_DIM).reshape(4, 1, LANE),
        bias_tab=_na_bias_tables(rpb, rows),
        cos_tab=cos_tab, sin_tab=sin_tab)


def _mixer_branches(p3, pc3, cols, prm, ctx_out):
    bsz, n, npad = p3.shape
    nctx = pc3.shape[1]
    p2 = p3.reshape(bsz * n, npad)
    pc2 = pc3.reshape(bsz * nctx, npad)
    flat = lambda t: t.reshape(-1, t.shape[-1])

    u3 = _conv_silu(p3, cols["xbc"], SSD_XBC, prm["conv_w"], prm["conv_b"])
    uc3 = _conv_silu(pc3, cols["xbc"], SSD_XBC, prm["conv_w"], prm["conv_b"])
    h_zero = jnp.zeros((bsz, 2, SSD_STATE, SSD_D_INNER), F32)
    yfc, ybc, h_ctx = _ssd_scan(uc3, pc3, cols["dt"], prm["alog_row"], prm["dtb_row"], h_zero)
    yf, yb, _ = _ssd_scan(u3, p3, cols["dt"], prm["alog_row"], prm["dtb_row"], h_ctx)
    y_ssd = _ssd_finish(flat(yf), flat(yb), flat(u3), p2, cols["z"], prm["dskip_row"], prm["g_row"])
    y_na = flat(_na_attention(p3, pc3, cols["nq"], cols["nk"], cols["nv"], prm["bias_tab"]))
    y_swa = flat(_swa_attention(p3, pc3, cols["sq"], cols["skv"], prm["cos_tab"], prm["sin_tab"],
                                prm["sink_rows"]))
    y_fn = flat(_fn_position(_fn_channel(p2, cols["fn"], 512).reshape(bsz, n, -1)))
    ys = [y_ssd, y_na, y_swa, y_fn]
    if not ctx_out:
        return ys, None
    yc_ssd = _ssd_finish(flat(yfc), flat(ybc), flat(uc3), pc2, cols["z"], prm["dskip_row"], prm["g_row"])
    yc_na = flat(_ctx_attention(pc3, cols["nq"], cols["nk"], cols["nv"], prm["sink_rows"], False))
    yc_swa = flat(_ctx_attention(pc3, cols["sq"], cols["skv"], cols["skv"] + LANE, prm["sink_rows"], True))
    yc_fn = flat(_fn_position(_fn_channel(pc2, cols["fn"], 512).reshape(bsz, nctx, -1)))
    return ys, [yc_ssd, yc_na, yc_swa, yc_fn]


def kernel(x, c, ctx, c_ctx, w_ada, b_ada, w_in, b_gate, ssd_conv_w, ssd_conv_b, ssd_a_log,
           ssd_dt_bias, ssd_d, ssd_norm_g, na_rpb, swa_sink, w_branch, w_out, ln1_g, ln1_b,
           ln2_g, ln2_b, ffn_w_up, ffn_conv_w, ffn_conv_b, ffn_w_down):
    bsz, n, d = x.shape
    nctx = ctx.shape[1]
    depth = w_ada.shape[0]
    dff = ffn_w_down.shape[1]
    alpha = (2.0 * depth) ** 0.25
    rows = n // GRID_W

    g_col = 0
    z_col = N_BRANCH * d
    fn_col = z_col + 512
    sq_col = fn_col + 512
    nq_col = sq_col + 512
    nk_col = nq_col + 512
    nv_col = nk_col + 512
    xbc_col = nv_col + 512
    skv_col = xbc_col + SSD_XBC
    dt_col = skv_col + 256
    n_used = dt_col + LANE
    n_in = -(-n_used // 256) * 256
    tn_in = _pick(n_in, (1280, 1024, 768, 512, 256))

    o_z, o_xbc, o_dt = 0, 512, 512 + SSD_XBC
    o_na = o_dt + 2 * SSD_HEADS
    o_sq = o_na + 3 * 512
    o_skv = o_sq + 512
    o_fn = o_skv + 256
    o_g = o_fn + 512

    def permute_w_in(w):
        parts = [w[:, o_g:o_g + N_BRANCH * d], w[:, o_z:o_z + 512], w[:, o_fn:o_fn + 512],
                 w[:, o_sq:o_sq + 512], w[:, o_na:o_na + 3 * 512], w[:, o_xbc:o_xbc + SSD_XBC],
                 w[:, o_skv:o_skv + 256], w[:, o_dt:o_dt + 2 * SSD_HEADS],
                 jnp.zeros((d, n_in - n_used + LANE - 2 * SSD_HEADS), w.dtype)]
        return jnp.concatenate(parts, axis=1).astype(BF16)

    n_rows = -(-(bsz + 1) // 8) * 8
    cvec = jnp.concatenate([c, c_ctx[None], jnp.zeros((n_rows - bsz - 1, d), F32)], axis=0)
    mods = _ada_mods(cvec, w_ada, b_ada)

    tm_lat = _pick(n, (1024, 512, 256))
    tm_ctx = _pick(bsz * nctx, (1024, 512, 256))
    tm_lat_k = _pick(n, (512, 256))
    tm_ctx_k = _pick(nctx, (512, 256))
    lat_row = _ModRow(tm_lat, seq=n)
    ctx_row = _ModRow(tm_ctx, fixed=bsz)
    lat_row_k = _ModRow(tm_lat_k, seq=n)
    ctx_row_k = _ModRow(tm_ctx_k, fixed=bsz)
    tn_up = _pick(dff, (1408, 1024, 512, 256, 128))
    tk_down = _pick(dff, (1408, 1024, 512, 256, 128))
    tk_out = _pick(d, (2048, 1024, 512, 256))

    cos_tab, sin_tab = _rope_tables(n)

    x2 = x.reshape(bsz * n, d)
    xc2 = ctx.reshape(bsz * nctx, d)
    for l in range(depth):
        ctx_out = l < depth - 1
        mod3 = mods[l].reshape(n_rows, 1, 6 * d)
        w_in_p = permute_w_in(w_in[l])
        wb = w_branch[l].astype(BF16)
        bg3 = b_gate[l].reshape(N_BRANCH, 1, d)
        w_out_b = w_out[l].astype(BF16)
        w_up_b = ffn_w_up[l].astype(BF16)
        w_down_b = ffn_w_down[l].astype(BF16)
        prm = _branch_params(ssd_conv_w[l], ssd_conv_b[l], ssd_a_log[l], ssd_dt_bias[l], ssd_d[l],
                             ssd_norm_g[l], na_rpb[l], swa_sink[l], rows, cos_tab, sin_tab)

        p2 = _lnmod_matmul(x2, mod3, lat_row, 0, 1, w_in_p, tn_in)
        pc2 = _lnmod_matmul(xc2, mod3, ctx_row, 0, 1, w_in_p, tn_in)
        p3 = p2.reshape(bsz, n, n_in)
        pc3 = pc2.reshape(bsz, nctx, n_in)

        cols = dict(z=z_col, fn=fn_col, sq=sq_col, nq=nq_col, nk=nk_col, nv=nv_col, xbc=xbc_col,
                    skv=skv_col, dt=dt_col)
        ys, ycs = _mixer_branches(p3, pc3, cols, prm, ctx_out)

        acc = _merge(ys, p2, g_col, wb, bg3)
        x_mid = _mm_resln(acc, w_out_b, x2, mod3, lat_row_k, 2, ln1_g[l], ln1_b[l], alpha, tk_out)
        gu = _lnmod_matmul(x_mid, mod3, lat_row, 3, 4, w_up_b, tn_up)
        x2 = _ffn_down(gu, ffn_conv_w[l], ffn_conv_b[l], w_down_b, x_mid, mod3, lat_row_k, 5,
                       ln2_g[l], ln2_b[l], alpha, tk_down, n)

        if ctx_out:
            acc_c = _merge(ycs, pc2, g_col, wb, bg3)
            xc_mid = _mm_resln(acc_c, w_out_b, xc2, mod3, ctx_row_k, 2, ln1_g[l], ln1_b[l], alpha, tk_out)
            gu_c = _lnmod_matmul(xc_mid, mod3, ctx_row, 3, 4, w_up_b, tn_up)
            xc2 = _ffn_down(gu_c, ffn_conv_w[l], ffn_conv_b[l], w_down_b, xc_mid, mod3, ctx_row_k, 5,
                            ln2_g[l], ln2_b[l], alpha, tk_down, nctx)

    return x2.reshape(bsz, n, d)
```

```python
import functools
import math

import numpy as np
import jax
import jax.numpy as jnp
from jax import lax
from jax.experimental import pallas as pl
from jax.experimental.pallas import tpu as pltpu

F32 = jnp.float32
BF16 = jnp.bfloat16

GRID_W = 64
HEAD_DIM = 64
SSD_HEADS = 8
SSD_D_INNER = 512
SSD_STATE = 128
SSD_GROUPS = 2
SSD_CHUNK = 128
SSD_XBC = 1024
NA_WIN_ROWS = 8
NA_WIN_COLS = 16
NA_Q_ROWS = 4
NA_K_ROWS = 12
SWA_WINDOW = 128
SWA_Q = 256
SWA_K = 512
FN_GROUP_DIM = 128
N_BRANCH = 4
BRANCH_WIDTH = 512
ROPE_BASE = 10000.0
LN_EPS = 1e-6
NEG = -1e30
DFT_ROWS = 64

LANE = 128
VMEM_LIMIT = 56 * 1024 * 1024


def _pick(dim, prefs):
    for p in prefs:
        if p <= dim and dim % p == 0:
            return p
    return dim


def _params(sem):
    return pltpu.CompilerParams(dimension_semantics=sem, vmem_limit_bytes=VMEM_LIMIT)


def _layer_norm(v):
    mu = jnp.mean(v, axis=-1, keepdims=True)
    vc = v - mu
    var = jnp.mean(vc * vc, axis=-1, keepdims=True)
    return vc * lax.rsqrt(var + LN_EPS)


def _sigmoid(v):
    return 0.5 * jnp.tanh(0.5 * v) + 0.5


def _silu(v):
    return v * _sigmoid(v)


def _cast_kernel(w_ref, o_ref):
    o_ref[...] = w_ref[...].astype(o_ref.dtype)


def _stage_weight(w_stack, l):
    _, r, c = w_stack.shape
    tr = _pick(r, (512, 256, 128))
    tc = _pick(c, (2048, 1408, 1024, 512, 256, 128))
    return pl.pallas_call(
        _cast_kernel,
        grid=(r // tr, c // tc),
        in_specs=[pl.BlockSpec((None, tr, tc), lambda i, j: (l, i, j))],
        out_specs=pl.BlockSpec((tr, tc), lambda i, j: (i, j)),
        out_shape=jax.ShapeDtypeStruct((r, c), BF16),
        compiler_params=_params(("parallel", "parallel")),
        name="stage_weight",
    )(w_stack)


def _permute_kernel(w_ref, o_ref, *, segments, pad):
    w = w_ref[...]
    parts = [w[:, a:a + n] for a, n in segments]
    if pad:
        parts.append(jnp.zeros((w.shape[0], pad), w.dtype))
    o_ref[...] = jnp.concatenate(parts, axis=1).astype(o_ref.dtype)


def _stage_permuted(w_stack, l, segments, n_out):
    _, r, c = w_stack.shape
    tr = _pick(r, (128, 64, 32, 16))
    pad = n_out - sum(n for _, n in segments)
    return pl.pallas_call(
        functools.partial(_permute_kernel, segments=tuple(segments), pad=pad),
        grid=(r // tr,),
        in_specs=[pl.BlockSpec((None, tr, c), lambda i: (l, i, 0))],
        out_specs=pl.BlockSpec((tr, n_out), lambda i: (i, 0)),
        out_shape=jax.ShapeDtypeStruct((r, n_out), BF16),
        compiler_params=_params(("parallel",)),
        name="stage_permuted",
    )(w_stack)


def _ada_kernel(c_ref, w_ref, b_ref, o_ref):
    s = _silu(c_ref[...])
    o_ref[...] = jnp.dot(s.astype(BF16), w_ref[...].astype(BF16),
                         preferred_element_type=F32) + b_ref[...]


def _ada_mods(cvec, w_ada, b_ada):
    depth, d, n6 = w_ada.shape
    r = cvec.shape[0]
    tn = _pick(n6, (1024, 512, 256, 128))
    return pl.pallas_call(
        _ada_kernel,
        grid=(depth, n6 // tn),
        in_specs=[pl.BlockSpec((r, d), lambda l, j: (0, 0)),
                  pl.BlockSpec((None, d, tn), lambda l, j: (l, 0, j)),
                  pl.BlockSpec((None, 1, tn), lambda l, j: (l, 0, j))],
        out_specs=pl.BlockSpec((None, r, tn), lambda l, j: (l, 0, j)),
        out_shape=jax.ShapeDtypeStruct((depth, r, n6), F32),
        compiler_params=_params(("parallel", "parallel")),
        name="ada_mods",
    )(cvec, w_ada, b_ada.reshape(depth, 1, n6))


def _lnmod_mm_kernel(x_ref, sh_ref, sc_ref, w_ref, o_ref, h_ref):
    @pl.when(pl.program_id(1) == 0)
    def _():
        y = _layer_norm(x_ref[...])
        h_ref[...] = (y * (1.0 + sc_ref[...]) + sh_ref[...]).astype(BF16)

    o_ref[...] = jnp.dot(h_ref[...], w_ref[...], preferred_element_type=F32).astype(o_ref.dtype)


def _lnmod_matmul(x2, mod3, mod_row, sh_chunk, sc_chunk, w, tn):
    m, d = x2.shape
    n = w.shape[1]
    tm = mod_row.tm
    return pl.pallas_call(
        _lnmod_mm_kernel,
        grid=(m // tm, n // tn),
        in_specs=[pl.BlockSpec((tm, d), lambda i, j: (i, 0)),
                  pl.BlockSpec((None, 1, d), lambda i, j: (mod_row(i), 0, sh_chunk)),
                  pl.BlockSpec((None, 1, d), lambda i, j: (mod_row(i), 0, sc_chunk)),
                  pl.BlockSpec((d, tn), lambda i, j: (0, j))],
        out_specs=pl.BlockSpec((tm, tn), lambda i, j: (i, j)),
        out_shape=jax.ShapeDtypeStruct((m, n), BF16),
        scratch_shapes=[pltpu.VMEM((tm, d), BF16)],
        compiler_params=_params(("parallel", "arbitrary")),
        name="lnmod_matmul",
    )(x2, mod3, mod3, w)


class _ModRow:
    def __init__(self, tm, seq=None, fixed=None):
        self.tm, self.seq, self.fixed = tm, seq, fixed

    def __call__(self, i):
        if self.fixed is not None:
            return self.fixed
        return (i * self.tm) // self.seq


def _dwconv_rows(x, w_ref):
    n = x.shape[0]
    k = w_ref.shape[0]
    row = lax.broadcasted_iota(jnp.int32, x.shape, 0)
    acc = x * w_ref[k // 2:k // 2 + 1, :]
    for t in range(k):
        off = t - k // 2
        if off == 0:
            continue
        shifted = pltpu.roll(x, (-off) % n, 0)
        valid = (row + off >= 0) & (row + off < n)
        acc = acc + jnp.where(valid, shifted, 0.0) * w_ref[t:t + 1, :]
    return acc


def _conv_silu_kernel(x_ref, w_ref, b_ref, o_ref):
    y = _dwconv_rows(x_ref[...].astype(F32), w_ref) + b_ref[...]
    o_ref[...] = _silu(y).astype(o_ref.dtype)


def _conv_silu(src3, col0, width, conv_w, conv_b):
    b, n, _ = src3.shape
    tc = LANE
    c0 = col0 // tc
    kk = conv_w.shape[0]
    return pl.pallas_call(
        _conv_silu_kernel,
        grid=(b, width // tc),
        in_specs=[pl.BlockSpec((None, n, tc), lambda bi, j: (bi, 0, c0 + j)),
                  pl.BlockSpec((kk, tc), lambda bi, j: (0, j)),
                  pl.BlockSpec((1, tc), lambda bi, j: (0, j))],
        out_specs=pl.BlockSpec((None, n, tc), lambda bi, j: (bi, 0, j)),
        out_shape=jax.ShapeDtypeStruct((b, n, width), BF16),
        compiler_params=_params(("parallel", "parallel")),
        name="conv_silu",
    )(src3, conv_w, conv_b.reshape(1, width))


def _softplus(v):
    return jnp.maximum(v, 0.0) + jnp.log(1.0 + jnp.exp(-jnp.abs(v)))


def _ssd_direction(u_ref, dt_ref, a_row, dtb_row, s_ref, y_ref, d, reverse):
    q = SSD_CHUNK
    dt_all = _softplus(dt_ref[...].astype(F32) + dtb_row)
    cum = dt_all * a_row
    row = lax.broadcasted_iota(jnp.int32, (q, LANE), 0)
    s = 1
    while s < q:
        if reverse:
            cum = cum + jnp.where(row < q - s, pltpu.roll(cum, q - s, 0), 0.0)
        else:
            cum = cum + jnp.where(row >= s, pltpu.roll(cum, s, 0), 0.0)
        s *= 2
    tot = cum[0:1, :] if reverse else cum[q - 1:q, :]
    cum_t = cum.T
    dt_t = dt_all.T
    e_cum = jnp.exp(cum)
    w_end = jnp.exp(tot - cum) * dt_all
    e_tot = jnp.exp(tot)

    ri = lax.broadcasted_iota(jnp.int32, (q, q), 0)
    ci = lax.broadcasted_iota(jnp.int32, (q, q), 1)
    tri = (ri <= ci) if reverse else (ri >= ci)
    lane = lax.broadcasted_iota(jnp.int32, (q, LANE), 1)
    lo = lane < HEAD_DIM
    lane_row = lax.broadcasted_iota(jnp.int32, (1, LANE), 1)

    def pair_cols(v, c0, c1):
        return jnp.where(lo, jnp.broadcast_to(v[:, c0:c0 + 1], (q, LANE)),
                         jnp.broadcast_to(v[:, c1:c1 + 1], (q, LANE)))

    heads_per_group = SSD_HEADS // SSD_GROUPS
    for g in range(SSD_GROUPS):
        b0 = SSD_D_INNER + g * SSD_STATE
        c0 = SSD_D_INNER + SSD_GROUPS * SSD_STATE + g * SSD_STATE
        bg = u_ref[:, b0:b0 + SSD_STATE]
        cg = u_ref[:, c0:c0 + SSD_STATE]
        cb = lax.dot_general(cg, bg, (((1,), (1,)), ((), ())), preferred_element_type=F32)
        bg_t = bg.astype(F32).T.astype(BF16)
        for hp in range(heads_per_group // 2):
            h0 = g * heads_per_group + 2 * hp
            col0, col1 = d * SSD_HEADS + h0, d * SSD_HEADS + h0 + 1
            l0 = h0 * HEAD_DIM
            xp = u_ref[:, l0:l0 + LANE]
            ws = []
            for col in (col0, col1):
                seg = cum[:, col:col + 1] - cum_t[col:col + 1, :]
                dec = jnp.exp(jnp.where(tri, seg, NEG))
                ws.append((dec * cb * dt_t[col:col + 1, :]).astype(BF16))
            w_pair = jnp.concatenate(ws, axis=1)
            zero = jnp.zeros_like(xp)
            x_bd = jnp.concatenate([jnp.where(lo, xp, zero), jnp.where(lo, zero, xp)], axis=0)
            y_diag = jnp.dot(w_pair, x_bd, preferred_element_type=F32)
            st = s_ref[d, :, l0:l0 + LANE]
            y_off = jnp.dot(cg, st.astype(BF16), preferred_element_type=F32) * pair_cols(e_cum, col0, col1)
            y_ref[:, l0:l0 + LANE] = y_diag + y_off
            xw = (xp.astype(F32) * pair_cols(w_end, col0, col1)).astype(BF16)
            upd = jnp.dot(bg_t, xw, preferred_element_type=F32)
            tot_pair = jnp.where(lane_row < HEAD_DIM,
                                 jnp.broadcast_to(e_tot[:, col0:col0 + 1], (1, LANE)),
                                 jnp.broadcast_to(e_tot[:, col1:col1 + 1], (1, LANE)))
            s_ref[d, :, l0:l0 + LANE] = st * tot_pair + upd


def _ssd_kernel(uf_ref, ub_ref, dtf_ref, dtb_ref, alog_ref, dtbias_ref, h0_ref,
                yf_ref, yb_ref, hT_ref, s_ref):
    t = pl.program_id(1)

    @pl.when(t == 0)
    def _():
        s_ref[...] = h0_ref[...]

    a_row = -jnp.exp(alog_ref[...])
    dtb_row = dtbias_ref[...]
    _ssd_direction(uf_ref, dtf_ref, a_row, dtb_row, s_ref, yf_ref, 0, False)
    _ssd_direction(ub_ref, dtb_ref, a_row, dtb_row, s_ref, yb_ref, 1, True)

    @pl.when(t == pl.num_programs(1) - 1)
    def _():
        hT_ref[...] = s_ref[...]


def _ssd_scan(u3, p3, dt_col0, alog_row, dtbias_row, h0):
    b, n, _ = u3.shape
    q = SSD_CHUNK
    nt = n // q
    dtb = dt_col0 // LANE
    hp = SSD_D_INNER
    return pl.pallas_call(
        _ssd_kernel,
        grid=(b, nt),
        in_specs=[pl.BlockSpec((None, q, SSD_XBC), lambda bi, t: (bi, t, 0)),
                  pl.BlockSpec((None, q, SSD_XBC), lambda bi, t: (bi, nt - 1 - t, 0)),
                  pl.BlockSpec((None, q, LANE), lambda bi, t: (bi, t, dtb)),
                  pl.BlockSpec((None, q, LANE), lambda bi, t: (bi, nt - 1 - t, dtb)),
                  pl.BlockSpec((1, LANE), lambda bi, t: (0, 0)),
                  pl.BlockSpec((1, LANE), lambda bi, t: (0, 0)),
                  pl.BlockSpec((None, 2, SSD_STATE, hp), lambda bi, t: (bi, 0, 0, 0))],
        out_specs=[pl.BlockSpec((None, q, hp), lambda bi, t: (bi, t, 0)),
                   pl.BlockSpec((None, q, hp), lambda bi, t: (bi, nt - 1 - t, 0)),
                   pl.BlockSpec((None, 2, SSD_STATE, hp), lambda bi, t: (bi, 0, 0, 0))],
        out_shape=[jax.ShapeDtypeStruct((b, n, hp), F32),
                   jax.ShapeDtypeStruct((b, n, hp), F32),
                   jax.ShapeDtypeStruct((b, 2, SSD_STATE, hp), F32)],
        scratch_shapes=[pltpu.VMEM((2, SSD_STATE, hp), F32)],
        compiler_params=_params(("parallel", "arbitrary")),
        name="ssd_scan",
    )(u3, u3, p3, p3, alog_row, dtbias_row, h0)


def _ssd_finish_kernel(yf_ref, yb_ref, x_ref, z_ref, dskip_ref, g_ref, o_ref):
    y = dskip_ref[...] * x_ref[...].astype(F32) + yf_ref[...] + yb_ref[...]
    y = y * _silu(z_ref[...].astype(F32))
    r = lax.rsqrt(jnp.mean(y * y, axis=-1, keepdims=True) + LN_EPS)
    o_ref[...] = (y * r * g_ref[...]).astype(o_ref.dtype)


def _ssd_finish(yf2, yb2, u2, p2, z_col0, dskip_row, g_row):
    m, w = yf2.shape
    tm = _pick(m, (1024, 512, 256, 128))
    zb = z_col0 // w
    row = lambda i: (i, 0)
    return pl.pallas_call(
        _ssd_finish_kernel,
        grid=(m // tm,),
        in_specs=[pl.BlockSpec((tm, w), row), pl.BlockSpec((tm, w), row),
                  pl.BlockSpec((tm, w), row),
                  pl.BlockSpec((tm, w), lambda i: (i, zb)),
                  pl.BlockSpec((1, w), lambda i: (0, 0)),
                  pl.BlockSpec((1, w), lambda i: (0, 0))],
        out_specs=pl.BlockSpec((tm, w), row),
        out_shape=jax.ShapeDtypeStruct((m, w), BF16),
        compiler_params=_params(("parallel",)),
        name="ssd_finish",
    )(yf2, yb2, u2, p2, dskip_row, g_row)


def _block_diag_rows(kv):
    lane = lax.broadcasted_iota(jnp.int32, kv.shape, 1)
    lo = lane < HEAD_DIM
    zero = jnp.zeros_like(kv)
    return jnp.concatenate([jnp.where(lo, kv, zero), jnp.where(lo, zero, kv)], axis=0)


def _dup_group(kv, g):
    lane = lax.broadcasted_iota(jnp.int32, kv.shape, 1)
    rolled = pltpu.roll(kv, HEAD_DIM, 1)
    return jnp.where(lane // HEAD_DIM == g, kv, rolled)


def _qk(q, kbd):
    return lax.dot_general(q, kbd, (((1,), (1,)), ((), ())), preferred_element_type=F32)


def _pair_softmax_pv(s_w, s_c, bias, vbd, vcbd, sink, out_dtype):
    nk = s_w.shape[1] // 2
    nc = s_c.shape[1] // 2
    tq = s_w.shape[0]
    pws, pcs, invs = [], [], []
    for h in range(2):
        sw = s_w[:, h * nk:(h + 1) * nk] + bias[h]
        sc = s_c[:, h * nc:(h + 1) * nc]
        m = jnp.maximum(jnp.max(sw, axis=-1, keepdims=True), jnp.max(sc, axis=-1, keepdims=True))
        if sink is not None:
            m = jnp.maximum(m, sink[h])
        pw = jnp.exp(sw - m)
        pc = jnp.exp(sc - m)
        l = jnp.sum(pw, axis=-1, keepdims=True) + jnp.sum(pc, axis=-1, keepdims=True)
        if sink is not None:
            l = l + jnp.exp(sink[h] - m)
        pws.append(pw.astype(BF16))
        pcs.append(pc.astype(BF16))
        invs.append(1.0 / l)
    o = jnp.dot(jnp.concatenate(pws, axis=1), vbd, preferred_element_type=F32)
    o = o + jnp.dot(jnp.concatenate(pcs, axis=1), vcbd, preferred_element_type=F32)
    lane = lax.broadcasted_iota(jnp.int32, (tq, LANE), 1)
    inv = jnp.where(lane < HEAD_DIM, jnp.broadcast_to(invs[0], (tq, LANE)),
                    jnp.broadcast_to(invs[1], (tq, LANE)))
    return (o * inv).astype(out_dtype)


def _na_kernel(q_ref, k_ref, v_ref, kc_ref, vc_ref, bias_ref, o_ref, *, rows):
    qb = pl.program_id(2)
    nk = NA_K_ROWS * GRID_W
    ks = jnp.clip(NA_Q_ROWS * qb - (NA_K_ROWS - NA_Q_ROWS) // 2, 0, rows - NA_K_ROWS) * GRID_W
    ks = pl.multiple_of(ks, GRID_W)
    for p in range(q_ref.shape[1] // LANE):
        cs = slice(p * LANE, (p + 1) * LANE)
        q = q_ref[:, cs] * (HEAD_DIM ** -0.5)
        kbd = _block_diag_rows(k_ref[pl.ds(ks, nk), cs])
        vbd = _block_diag_rows(v_ref[pl.ds(ks, nk), cs])
        kcbd = _block_diag_rows(kc_ref[:, cs])
        vcbd = _block_diag_rows(vc_ref[:, cs])
        s_w = _qk(q, kbd)
        s_c = _qk(q, kcbd)
        o_ref[:, cs] = _pair_softmax_pv(s_w, s_c, [bias_ref[2 * p], bias_ref[2 * p + 1]], vbd, vcbd,
                                        None, o_ref.dtype)


def _na_key_start(qb, rows):
    return min(max(NA_Q_ROWS * qb - (NA_K_ROWS - NA_Q_ROWS) // 2, 0), rows - NA_K_ROWS)


def _na_bias_tables(rpb, rows):
    nqb = rows // NA_Q_ROWS
    variants = [0, 1 if nqb > 2 else 0, nqb - 1]
    nh = rpb.shape[0]
    w = GRID_W
    pad = w - NA_WIN_COLS
    rp = jnp.pad(rpb.astype(F32), ((0, 0), (0, 0), (pad, pad)))
    toep = jnp.stack([rp[:, :, w - 1 - qc:2 * w - 1 - qc] for qc in range(w)], axis=2)
    qc, kc = np.arange(w)[:, None], np.arange(w)[None, :]
    ws = np.clip(qc - NA_WIN_COLS // 2, 0, w - NA_WIN_COLS)
    toep = jnp.where(jnp.asarray((kc >= ws) & (kc < ws + NA_WIN_COLS)), toep, NEG)
    masked = jnp.full((nh, w, w), NEG, F32)
    tabs = []
    for qb in variants:
        block_rows = []
        for qr_l in range(NA_Q_ROWS):
            qr = NA_Q_ROWS * qb + qr_l
            rs = min(max(qr - NA_WIN_ROWS // 2, 0), rows - NA_WIN_ROWS)
            blocks = []
            for kr_l in range(NA_K_ROWS):
                kr = _na_key_start(qb, rows) + kr_l
                inside = rs <= kr < rs + NA_WIN_ROWS
                blocks.append(toep[:, kr - qr + NA_WIN_ROWS - 1] if inside else masked)
            block_rows.append(jnp.concatenate(blocks, axis=-1))
        tabs.append(jnp.concatenate(block_rows, axis=-2))
    return jnp.stack(tabs, axis=0)


def _na_attention(p3, pc3, q_col0, k_col0, v_col0, bias_tab):
    b, n, _ = p3.shape
    nc = pc3.shape[1]
    rows = n // GRID_W
    tq = NA_Q_ROWS * GRID_W
    nk = NA_K_ROWS * GRID_W
    nqb = n // tq
    wb = 2 * LANE
    qc, kc, vc = q_col0 // wb, k_col0 // wb, v_col0 // wb
    ngrp = 2

    def variant(qb):
        return jnp.where(qb == 0, 0, jnp.where(qb == nqb - 1, 2, 1))

    return pl.pallas_call(
        functools.partial(_na_kernel, rows=rows),
        grid=(b, ngrp, nqb),
        in_specs=[pl.BlockSpec((None, tq, wb), lambda bi, h, i: (bi, i, qc + h)),
                  pl.BlockSpec((None, n, wb), lambda bi, h, i: (bi, 0, kc + h)),
                  pl.BlockSpec((None, n, wb), lambda bi, h, i: (bi, 0, vc + h)),
                  pl.BlockSpec((None, nc, wb), lambda bi, h, i: (bi, 0, kc + h)),
                  pl.BlockSpec((None, nc, wb), lambda bi, h, i: (bi, 0, vc + h)),
                  pl.BlockSpec((None, 4, tq, nk), lambda bi, h, i: (variant(i), h, 0, 0))],
        out_specs=pl.BlockSpec((None, tq, wb), lambda bi, h, i: (bi, i, h)),
        out_shape=jax.ShapeDtypeStruct((b, n, ngrp * wb), BF16),
        compiler_params=_params(("parallel", "parallel", "arbitrary")),
        name="na_attention",
    )(p3, p3, p3, pc3, pc3, bias_tab)


def _rope(x, cos, sin_signed):
    lane = lax.broadcasted_iota(jnp.int32, x.shape, 1)
    first = (lane % 32) < 16
    partner = jnp.where(first, pltpu.roll(x, LANE - 16, 1), pltpu.roll(x, 16, 1))
    return x * cos + partner * sin_signed


def _sink_pair(sink_ref):
    s = sink_ref[...]
    return [s[:, 0:1], s[:, HEAD_DIM:HEAD_DIM + 1]]


def _swa_kernel(q_ref, k_ref, v_ref, kc_ref, vc_ref, cos_ref, sin_ref, sink_ref, o_ref, *, n):
    g = pl.program_id(1)
    qb = pl.program_id(2)
    q0 = pl.multiple_of(qb * SWA_Q, SWA_Q)
    ks = pl.multiple_of(jnp.clip(qb * SWA_Q - SWA_WINDOW, 0, n - SWA_K), SWA_WINDOW)
    cos_q, sin_q = cos_ref[pl.ds(q0, SWA_Q), :], sin_ref[pl.ds(q0, SWA_Q), :]
    kk = _rope(k_ref[pl.ds(ks, SWA_K), :].astype(F32), cos_ref[pl.ds(ks, SWA_K), :],
               sin_ref[pl.ds(ks, SWA_K), :])
    kbd = _block_diag_rows(_dup_group(kk, g).astype(BF16))
    vbd = _block_diag_rows(_dup_group(v_ref[pl.ds(ks, SWA_K), :].astype(F32), g).astype(BF16))
    kcbd = _block_diag_rows(_dup_group(kc_ref[...].astype(F32), g).astype(BF16))
    vcbd = _block_diag_rows(_dup_group(vc_ref[...].astype(F32), g).astype(BF16))
    qpos = q0 + lax.broadcasted_iota(jnp.int32, (SWA_Q, SWA_K), 0)
    kpos = ks + lax.broadcasted_iota(jnp.int32, (SWA_Q, SWA_K), 1)
    bias = jnp.where(jnp.abs(kpos - qpos) <= SWA_WINDOW, 0.0, NEG)
    for p in range(q_ref.shape[1] // LANE):
        cs = slice(p * LANE, (p + 1) * LANE)
        q = _rope(q_ref[:, cs].astype(F32), cos_q, sin_q)
        q = (q * (HEAD_DIM ** -0.5)).astype(BF16)
        s_w = _qk(q, kbd)
        s_c = _qk(q, kcbd)
        o_ref[:, cs] = _pair_softmax_pv(s_w, s_c, [bias, bias], vbd, vcbd, _sink_pair(sink_ref.at[p]),
                                        o_ref.dtype)


def _swa_attention(p3, pc3, q_col0, kv_col0, cos_tab, sin_tab, sink_rows):
    b, n, _ = p3.shape
    nc = pc3.shape[1]
    nqb = n // SWA_Q
    wb = 2 * LANE
    qc, kc = q_col0 // wb, kv_col0 // LANE
    ngrp = 2
    return pl.pallas_call(
        functools.partial(_swa_kernel, n=n),
        grid=(b, ngrp, nqb),
        in_specs=[pl.BlockSpec((None, SWA_Q, wb), lambda bi, h, i: (bi, i, qc + h)),
                  pl.BlockSpec((None, n, LANE), lambda bi, h, i: (bi, 0, kc)),
                  pl.BlockSpec((None, n, LANE), lambda bi, h, i: (bi, 0, kc + 1)),
                  pl.BlockSpec((None, nc, LANE), lambda bi, h, i: (bi, 0, kc)),
                  pl.BlockSpec((None, nc, LANE), lambda bi, h, i: (bi, 0, kc + 1)),
                  pl.BlockSpec((n, LANE), lambda bi, h, i: (0, 0)),
                  pl.BlockSpec((n, LANE), lambda bi, h, i: (0, 0)),
                  pl.BlockSpec((2, 1, LANE), lambda bi, h, i: (h, 0, 0))],
        out_specs=pl.BlockSpec((None, SWA_Q, wb), lambda bi, h, i: (bi, i, h)),
        out_shape=jax.ShapeDtypeStruct((b, n, ngrp * wb), BF16),
        compiler_params=_params(("parallel", "parallel", "arbitrary")),
        name="swa_attention",
    )(p3, p3, p3, pc3, pc3, cos_tab, sin_tab, sink_rows)


def _ctx_attn_kernel(q_ref, k_ref, v_ref, sink_ref, o_ref, *, grouped):
    hp = pl.program_id(1)
    q = q_ref[...] * (HEAD_DIM ** -0.5)
    k, v = k_ref[...], v_ref[...]
    if grouped:
        g = hp // 2
        k = _dup_group(k.astype(F32), g).astype(BF16)
        v = _dup_group(v.astype(F32), g).astype(BF16)
    kbd, vbd = _block_diag_rows(k), _block_diag_rows(v)
    s = _qk(q, kbd)
    t = q.shape[0]
    sink = _sink_pair(sink_ref) if grouped else None
    ps, invs = [], []
    for h in range(2):
        sh = s[:, h * t:(h + 1) * t]
        m = jnp.max(sh, axis=-1, keepdims=True)
        if sink is not None:
            m = jnp.maximum(m, sink[h])
        p = jnp.exp(sh - m)
        l = jnp.sum(p, axis=-1, keepdims=True)
        if sink is not None:
            l = l + jnp.exp(sink[h] - m)
        ps.append(p.astype(BF16))
        invs.append(1.0 / l)
    o = jnp.dot(jnp.concatenate(ps, axis=1), vbd, preferred_element_type=F32)
    lane = lax.broadcasted_iota(jnp.int32, (t, LANE), 1)
    inv = jnp.where(lane < HEAD_DIM, jnp.broadcast_to(invs[0], (t, LANE)),
                    jnp.broadcast_to(invs[1], (t, LANE)))
    o_ref[...] = (o * inv).astype(o_ref.dtype)


def _ctx_attention(pc3, q_col0, k_col0, v_col0, sink_rows, grouped):
    b, nc, _ = pc3.shape
    qc, kc, vc = q_col0 // LANE, k_col0 // LANE, v_col0 // LANE
    npair = 4
    kv_blk = (lambda h: 0) if grouped else (lambda h: h)
    return pl.pallas_call(
        functools.partial(_ctx_attn_kernel, grouped=grouped),
        grid=(b, npair),
        in_specs=[pl.BlockSpec((None, nc, LANE), lambda bi, h: (bi, 0, qc + h)),
                  pl.BlockSpec((None, nc, LANE), lambda bi, h: (bi, 0, kc + kv_blk(h))),
                  pl.BlockSpec((None, nc, LANE), lambda bi, h: (bi, 0, vc + kv_blk(h))),
                  pl.BlockSpec((None, 1, LANE), lambda bi, h: (h, 0, 0))],
        out_specs=pl.BlockSpec((None, nc, LANE), lambda bi, h: (bi, 0, h)),
        out_shape=jax.ShapeDtypeStruct((b, nc, npair * LANE), BF16),
        compiler_params=_params(("parallel", "parallel")),
        name="ctx_attention",
    )(pc3, pc3, pc3, sink_rows)


def _fn_channel_kernel(x_ref, cs_ref, o_ref):
    w = x_ref.shape[1]
    ngroups = w // FN_GROUP_DIM
    for g in range(ngroups):
        xg = x_ref[:, g * FN_GROUP_DIM:(g + 1) * FN_GROUP_DIM]
        ab = jnp.dot(xg, cs_ref[...], preferred_element_type=F32)
        o_ref[:, g * FN_GROUP_DIM:(g + 1) * FN_GROUP_DIM] = ab[:, :FN_GROUP_DIM].astype(o_ref.dtype)
        o_ref[:, w + g * FN_GROUP_DIM:w + (g + 1) * FN_GROUP_DIM] = ab[:, FN_GROUP_DIM:].astype(o_ref.dtype)


def _fn_channel(p2, col0, width):
    m = p2.shape[0]
    tm = _pick(m, (1024, 512, 256, 128))
    k = np.arange(FN_GROUP_DIM)
    ang = 2.0 * np.pi * ((k[:, None] * k[None, :]) % FN_GROUP_DIM) / FN_GROUP_DIM
    cs = jnp.asarray(np.concatenate([np.cos(ang), np.sin(ang)], axis=1), BF16)
    cb = col0 // width
    return pl.pallas_call(
        _fn_channel_kernel,
        grid=(m // tm,),
        in_specs=[pl.BlockSpec((tm, width), lambda i: (i, cb)),
                  pl.BlockSpec((FN_GROUP_DIM, 2 * FN_GROUP_DIM), lambda i: (0, 0))],
        out_specs=pl.BlockSpec((tm, 2 * width), lambda i: (i, 0)),
        out_shape=jax.ShapeDtypeStruct((m, 2 * width), BF16),
        compiler_params=_params(("parallel",)),
        name="fn_channel",
    )(p2, cs)


def _fn_position_kernel(ac_ref, as_ref, bc_ref, bs_ref, ab_ref, o_ref, c_ref, s_ref, *, scale):
    @pl.when(pl.program_id(1) == 0)
    def _():
        for j in range(ac_ref.shape[0]):
            ca, sa = ac_ref[j:j + 1, :], as_ref[j:j + 1, :]
            cb, sb = bc_ref[...], bs_ref[...]
            c_ref[j * DFT_ROWS:(j + 1) * DFT_ROWS, :] = (ca * cb - sa * sb).astype(BF16)
            s_ref[j * DFT_ROWS:(j + 1) * DFT_ROWS, :] = (sa * cb + ca * sb).astype(BF16)

    w = o_ref.shape[1]
    y = jnp.dot(c_ref[...], ab_ref[:, :w], preferred_element_type=F32)
    y = y - jnp.dot(s_ref[...], ab_ref[:, w:], preferred_element_type=F32)
    o_ref[...] = (y * scale).astype(o_ref.dtype)


def _fn_position(ab3):
    b, n, w2 = ab3.shape
    w = w2 // 2
    tm = _pick(n, (512, 256, 128, 64))
    jc = tm // DFT_ROWS
    n1 = n // DFT_ROWS
    k = np.arange(n, dtype=np.int64)
    j1 = np.arange(n1, dtype=np.int64)
    j2 = np.arange(DFT_ROWS, dtype=np.int64)
    ang_a = 2.0 * np.pi * ((j1[:, None] * DFT_ROWS * k[None, :]) % n) / n
    ang_b = 2.0 * np.pi * ((j2[:, None] * k[None, :]) % n) / n
    ac, as_ = jnp.asarray(np.cos(ang_a), F32), jnp.asarray(np.sin(ang_a), F32)
    bc, bs = jnp.asarray(np.cos(ang_b), F32), jnp.asarray(np.sin(ang_b), F32)
    scale = 1.0 / math.sqrt(n * FN_GROUP_DIM)
    if n1 % 8 != 0 and jc != n1:
        raise ValueError("unsupported sequence length for the position DFT tiling")
    return pl.pallas_call(
        functools.partial(_fn_position_kernel, scale=scale),
        grid=(n // tm, b),
        in_specs=[pl.BlockSpec((jc, n), lambda i, bi: (i, 0)),
                  pl.BlockSpec((jc, n), lambda i, bi: (i, 0)),
                  pl.BlockSpec((DFT_ROWS, n), lambda i, bi: (0, 0)),
                  pl.BlockSpec((DFT_ROWS, n), lambda i, bi: (0, 0)),
                  pl.BlockSpec((None, n, w2), lambda i, bi: (bi, 0, 0))],
        out_specs=pl.BlockSpec((None, tm, w), lambda i, bi: (bi, i, 0)),
        out_shape=jax.ShapeDtypeStruct((b, n, w), BF16),
        scratch_shapes=[pltpu.VMEM((tm, n), BF16), pltpu.VMEM((tm, n), BF16)],
        compiler_params=_params(("parallel", "arbitrary")),
        name="fn_position",
    )(ac, as_, bc, bs, ab3)


def _residual_ln(o_ref, x_ref, gate_ref, lng_ref, lnb_ref, alpha):
    v = alpha * x_ref[...] + gate_ref[...] * o_ref[...]
    o_ref[...] = _layer_norm(v) * lng_ref[...] + lnb_ref[...]


def _merge_out_kernel(y0, y1, y2, y3, g0, g1, g2, g3, wb_ref, bg_ref, wo_ref, x_ref, gate_ref, lng_ref,
                      lnb_ref, o_ref, *, alpha):
    j = pl.program_id(1)

    @pl.when(j == 0)
    def _():
        o_ref[...] = jnp.zeros_like(o_ref)

    merged = None
    for i, (y, g) in enumerate(((y0, g0), (y1, g1), (y2, g2), (y3, g3))):
        gate = _sigmoid(g[...].astype(F32) + bg_ref[i])
        term = gate * jnp.dot(y[...], wb_ref[i], preferred_element_type=F32)
        merged = term if merged is None else merged + term
    o_ref[...] += jnp.dot(merged.astype(BF16), wo_ref[...], preferred_element_type=F32)

    @pl.when(j == pl.num_programs(1) - 1)
    def _():
        _residual_ln(o_ref, x_ref, gate_ref, lng_ref, lnb_ref, alpha)


def _merge_out(ys, p2, gate_col0, wb, bg3, w_out, x2, mod3, mod_row, gate_chunk, ln_g, ln_b, alpha):
    m = p2.shape[0]
    d = wb.shape[2]
    tm = mod_row.tm
    tn = _pick(d, (1024, 512, 256))
    gb = gate_col0 // tn
    per = d // tn
    bw = ys[0].shape[1]
    y_specs = [pl.BlockSpec((tm, bw), lambda i, j: (i, 0)) for _ in range(N_BRANCH)]
    g_specs = [pl.BlockSpec((tm, tn), functools.partial(lambda i, j, br: (i, gb + br * per + j), br=br))
               for br in range(N_BRANCH)]
    return pl.pallas_call(
        functools.partial(_merge_out_kernel, alpha=alpha),
        grid=(m // tm, per),
        in_specs=y_specs + g_specs + [
            pl.BlockSpec((N_BRANCH, bw, tn), lambda i, j: (0, 0, j)),
            pl.BlockSpec((N_BRANCH, 1, tn), lambda i, j: (0, 0, j)),
            pl.BlockSpec((tn, d), lambda i, j: (j, 0)),
            pl.BlockSpec((tm, d), lambda i, j: (i, 0)),
            pl.BlockSpec((None, 1, d), lambda i, j: (mod_row(i), 0, gate_chunk)),
            pl.BlockSpec((1, d), lambda i, j: (0, 0)),
            pl.BlockSpec((1, d), lambda i, j: (0, 0))],
        out_specs=pl.BlockSpec((tm, d), lambda i, j: (i, 0)),
        out_shape=jax.ShapeDtypeStruct((m, d), F32),
        compiler_params=_params(("parallel", "arbitrary")),
        name="merge_out",
    )(*ys, p2, p2, p2, p2, wb, bg3, w_out, x2, mod3, ln_g.reshape(1, d), ln_b.reshape(1, d))


def _mm_resln_kernel(a_ref, w_ref, x_ref, gate_ref, lng_ref, lnb_ref, o_ref, *, alpha):
    k = pl.program_id(1)

    @pl.when(k == 0)
    def _():
        o_ref[...] = jnp.zeros_like(o_ref)

    o_ref[...] += jnp.dot(a_ref[...], w_ref[...], preferred_element_type=F32)

    @pl.when(k == pl.num_programs(1) - 1)
    def _():
        _residual_ln(o_ref, x_ref, gate_ref, lng_ref, lnb_ref, alpha)


def _mm_resln(a2, w, x2, mod3, mod_row, gate_chunk, ln_g, ln_b, alpha, tk):
    m, kdim = a2.shape
    d = w.shape[1]
    tm = mod_row.tm
    return pl.pallas_call(
        functools.partial(_mm_resln_kernel, alpha=alpha),
        grid=(m // tm, kdim // tk),
        in_specs=[pl.BlockSpec((tm, tk), lambda i, k: (i, k)),
                  pl.BlockSpec((tk, d), lambda i, k: (k, 0)),
                  pl.BlockSpec((tm, d), lambda i, k: (i, 0)),
                  pl.BlockSpec((None, 1, d), lambda i, k: (mod_row(i), 0, gate_chunk)),
                  pl.BlockSpec((1, d), lambda i, k: (0, 0)),
                  pl.BlockSpec((1, d), lambda i, k: (0, 0))],
        out_specs=pl.BlockSpec((tm, d), lambda i, k: (i, 0)),
        out_shape=jax.ShapeDtypeStruct((m, d), F32),
        compiler_params=_params(("parallel", "arbitrary")),
        name="matmul_residual_ln",
    )(a2, w, x2, mod3, ln_g.reshape(1, d), ln_b.reshape(1, d))


HALO_ROWS = 8


def _ffn_up_kernel(x_ref, xp_ref, xn_ref, sh_ref, sc_ref, wg_ref, wu_ref, cw_ref, cb_ref, o_ref, h_ref,
                   *, seq):
    i = pl.program_id(0)
    tm = x_ref.shape[0]

    @pl.when(pl.program_id(1) == 0)
    def _():
        scale, shift = 1.0 + sc_ref[...], sh_ref[...]
        h_ref[0:tm, :] = (_layer_norm(x_ref[...]) * scale + shift).astype(BF16)
        halo = jnp.concatenate([xp_ref[...], xn_ref[...]], axis=0)
        h_ref[tm:tm + 2 * HALO_ROWS, :] = (_layer_norm(halo) * scale + shift).astype(BF16)

    g_all = jnp.dot(h_ref[...], wg_ref[...], preferred_element_type=F32)
    u = jnp.dot(h_ref[0:tm, :], wu_ref[...], preferred_element_type=F32)
    g = g_all[0:tm, :]
    has_prev = jnp.where((i * tm) % seq == 0, 0.0, 1.0)
    has_next = jnp.where(((i + 1) * tm) % seq == 0, 0.0, 1.0)
    g_prev = g_all[tm + HALO_ROWS - 1:tm + HALO_ROWS, :] * has_prev
    g_next = g_all[tm + HALO_ROWS:tm + HALO_ROWS + 1, :] * has_next
    row = lax.broadcasted_iota(jnp.int32, g.shape, 0)
    above = jnp.where(row == 0, g_prev, pltpu.roll(g, 1, 0))
    below = jnp.where(row == tm - 1, g_next, pltpu.roll(g, tm - 1, 0))
    y = above * cw_ref[0:1, :] + g * cw_ref[1:2, :] + below * cw_ref[2:3, :] + cb_ref[...]
    o_ref[...] = (_silu(y) * u).astype(o_ref.dtype)


def _ffn_up(x2, mod3, mod_row, sh_chunk, sc_chunk, w_up, conv_w, conv_b, tn, seq):
    m, d = x2.shape
    dff = w_up.shape[1] // 2
    tm = mod_row.tm
    if conv_w.shape[0] != 3 or seq % tm != 0 or tm % HALO_ROWS != 0:
        raise ValueError("unsupported ConvFFN tiling")
    ub = dff // tn
    hb = tm // HALO_ROWS
    last_hb = m // HALO_ROWS - 1
    return pl.pallas_call(
        functools.partial(_ffn_up_kernel, seq=seq),
        grid=(m // tm, dff // tn),
        in_specs=[pl.BlockSpec((tm, d), lambda i, j: (i, 0)),
                  pl.BlockSpec((HALO_ROWS, d), lambda i, j: (jnp.maximum(i * hb - 1, 0), 0)),
                  pl.BlockSpec((HALO_ROWS, d), lambda i, j: (jnp.minimum((i + 1) * hb, last_hb), 0)),
                  pl.BlockSpec((None, 1, d), lambda i, j: (mod_row(i), 0, sh_chunk)),
                  pl.BlockSpec((None, 1, d), lambda i, j: (mod_row(i), 0, sc_chunk)),
                  pl.BlockSpec((d, tn), lambda i, j: (0, j)),
                  pl.BlockSpec((d, tn), lambda i, j: (0, ub + j)),
                  pl.BlockSpec((3, tn), lambda i, j: (0, j)),
                  pl.BlockSpec((1, tn), lambda i, j: (0, j))],
        out_specs=pl.BlockSpec((tm, tn), lambda i, j: (i, j)),
        out_shape=jax.ShapeDtypeStruct((m, dff), BF16),
        scratch_shapes=[pltpu.VMEM((tm + 2 * HALO_ROWS, d), BF16)],
        compiler_params=_params(("parallel", "arbitrary")),
        name="ffn_up",
    )(x2, x2, x2, mod3, mod3, w_up, w_up, conv_w, conv_b.reshape(1, dff))


def _rope_tables(n):
    t = jnp.arange(n)
    rows = (t // GRID_W).astype(F32)
    cols = (t % GRID_W).astype(F32)
    n_freq = HEAD_DIM // 4
    inv = ROPE_BASE ** (-jnp.arange(n_freq, dtype=F32) / n_freq)
    ar, ac = rows[:, None] * inv, cols[:, None] * inv
    cos = jnp.concatenate([jnp.cos(ar), jnp.cos(ar), jnp.cos(ac), jnp.cos(ac)], axis=1)
    sin = jnp.concatenate([-jnp.sin(ar), jnp.sin(ar), -jnp.sin(ac), jnp.sin(ac)], axis=1)
    return jnp.tile(cos, (1, 2)), jnp.tile(sin, (1, 2))


def _branch_params(conv_w, conv_b, a_log, dt_bias, d_skip, norm_g, rpb, sink, rows, cos_tab, sin_tab):
    pad = jnp.zeros((1, LANE - 2 * SSD_HEADS), F32)
    return dict(
        conv_w=conv_w, conv_b=conv_b,
        alog_row=jnp.concatenate([a_log.reshape(1, -1), pad], axis=1),
        dtb_row=jnp.concatenate([dt_bias.reshape(1, -1), pad], axis=1),
        dskip_row=jnp.repeat(d_skip, HEAD_DIM).reshape(1, SSD_D_INNER),
        g_row=norm_g.reshape(1, SSD_D_INNER),
        sink_rows=jnp.repeat(sink, HEAD_DIM).reshape(4, 1, LANE),
        bias_tab=_na_bias_tables(rpb, rows),
        cos_tab=cos_tab, sin_tab=sin_tab)


def _mixer_branches(p3, pc3, cols, prm, ctx_out):
    bsz, n, npad = p3.shape
    nctx = pc3.shape[1]
    p2 = p3.reshape(bsz * n, npad)
    pc2 = pc3.reshape(bsz * nctx, npad)
    flat = lambda t: t.reshape(-1, t.shape[-1])

    u3 = _conv_silu(p3, cols["xbc"], SSD_XBC, prm["conv_w"], prm["conv_b"])
    uc3 = _conv_silu(pc3, cols["xbc"], SSD_XBC, prm["conv_w"], prm["conv_b"])
    h_zero = jnp.zeros((bsz, 2, SSD_STATE, SSD_D_INNER), F32)
    yfc, ybc, h_ctx = _ssd_scan(uc3, pc3, cols["dt"], prm["alog_row"], prm["dtb_row"], h_zero)
    yf, yb, _ = _ssd_scan(u3, p3, cols["dt"], prm["alog_row"], prm["dtb_row"], h_ctx)
    y_ssd = _ssd_finish(flat(yf), flat(yb), flat(u3), p2, cols["z"], prm["dskip_row"], prm["g_row"])
    y_na = flat(_na_attention(p3, pc3, cols["nq"], cols["nk"], cols["nv"], prm["bias_tab"]))
    y_swa = flat(_swa_attention(p3, pc3, cols["sq"], cols["skv"], prm["cos_tab"], prm["sin_tab"],
                                prm["sink_rows"]))
    y_fn = flat(_fn_position(_fn_channel(p2, cols["fn"], 512).reshape(bsz, n, -1)))
    ys = [y_ssd, y_na, y_swa, y_fn]
    if not ctx_out:
        return ys, None
    yc_ssd = _ssd_finish(flat(yfc), flat(ybc), flat(uc3), pc2, cols["z"], prm["dskip_row"], prm["g_row"])
    yc_na = flat(_ctx_attention(pc3, cols["nq"], cols["nk"], cols["nv"], prm["sink_rows"], False))
    yc_swa = flat(_ctx_attention(pc3, cols["sq"], cols["skv"], cols["skv"] + LANE, prm["sink_rows"], True))
    yc_fn = flat(_fn_position(_fn_channel(pc2, cols["fn"], 512).reshape(bsz, nctx, -1)))
    return ys, [yc_ssd, yc_na, yc_swa, yc_fn]


def kernel(x, c, ctx, c_ctx, w_ada, b_ada, w_in, b_gate, ssd_conv_w, ssd_conv_b, ssd_a_log,
           ssd_dt_bias, ssd_d, ssd_norm_g, na_rpb, swa_sink, w_branch, w_out, ln1_g, ln1_b,
           ln2_g, ln2_b, ffn_w_up, ffn_conv_w, ffn_conv_b, ffn_w_down):
    bsz, n, d = x.shape
    nctx = ctx.shape[1]
    depth = w_ada.shape[0]
    dff = ffn_w_down.shape[1]
    alpha = (2.0 * depth) ** 0.25
    rows = n // GRID_W

    g_col = 0
    z_col = N_BRANCH * d
    fn_col = z_col + 512
    sq_col = fn_col + 512
    nq_col = sq_col + 512
    nk_col = nq_col + 512
    nv_col = nk_col + 512
    xbc_col = nv_col + 512
    skv_col = xbc_col + SSD_XBC
    dt_col = skv_col + 256
    n_used = dt_col + LANE
    n_in = -(-n_used // 256) * 256
    tn_in = _pick(n_in, (1280, 1024, 768, 512, 256))

    o_z, o_xbc, o_dt = 0, 512, 512 + SSD_XBC
    o_na = o_dt + 2 * SSD_HEADS
    o_sq = o_na + 3 * 512
    o_skv = o_sq + 512
    o_fn = o_skv + 256
    o_g = o_fn + 512

    in_segments = [(o_g, N_BRANCH * d), (o_z, 512), (o_fn, 512), (o_sq, 512), (o_na, 3 * 512),
                   (o_xbc, SSD_XBC), (o_skv, 256), (o_dt, 2 * SSD_HEADS)]

    n_rows = -(-(bsz + 1) // 8) * 8
    cvec = jnp.concatenate([c, c_ctx[None], jnp.zeros((n_rows - bsz - 1, d), F32)], axis=0)
    mods = _ada_mods(cvec, w_ada, b_ada)

    lat_row = _ModRow(_pick(n, (1024, 512, 256)), seq=n)
    ctx_row = _ModRow(_pick(bsz * nctx, (1024, 512, 256)), fixed=bsz)
    lat_row_k = _ModRow(_pick(n, (512, 256)), seq=n)
    ctx_row_k = _ModRow(_pick(bsz * nctx, (512, 256)), fixed=bsz)
    ctx_row_seq = _ModRow(_pick(nctx, (1024, 512, 256)), fixed=bsz)
    tn_up = _pick(dff, (512, 256, 128))
    tk_down = _pick(dff, (1408, 1024, 512, 256, 128))

    cos_tab, sin_tab = _rope_tables(n)

    x2 = x.reshape(bsz * n, d)
    xc2 = ctx.reshape(bsz * nctx, d)
    for l in range(depth):
        ctx_out = l < depth - 1
        mod3 = mods[l].reshape(n_rows, 1, 6 * d)
        w_in_p = _stage_permuted(w_in, l, in_segments, n_in)
        wb = _stage_weight(w_branch.reshape(depth, N_BRANCH * BRANCH_WIDTH, d), l)
        wb = wb.reshape(N_BRANCH, BRANCH_WIDTH, d)
        bg3 = b_gate[l].reshape(N_BRANCH, 1, d)
        w_out_b = _stage_weight(w_out, l)
        w_up_b = _stage_weight(ffn_w_up, l)
        w_down_b = _stage_weight(ffn_w_down, l)
        prm = _branch_params(ssd_conv_w[l], ssd_conv_b[l], ssd_a_log[l], ssd_dt_bias[l], ssd_d[l],
                             ssd_norm_g[l], na_rpb[l], swa_sink[l], rows, cos_tab, sin_tab)

        p2 = _lnmod_matmul(x2, mod3, lat_row, 0, 1, w_in_p, tn_in)
        pc2 = _lnmod_matmul(xc2, mod3, ctx_row, 0, 1, w_in_p, tn_in)
        p3 = p2.reshape(bsz, n, n_in)
        pc3 = pc2.reshape(bsz, nctx, n_in)

        cols = dict(z=z_col, fn=fn_col, sq=sq_col, nq=nq_col, nk=nk_col, nv=nv_col, xbc=xbc_col,
                    skv=skv_col, dt=dt_col)
        ys, ycs = _mixer_branches(p3, pc3, cols, prm, ctx_out)

        x_mid = _merge_out(ys, p2, g_col, wb, bg3, w_out_b, x2, mod3, lat_row_k, 2, ln1_g[l], ln1_b[l], alpha)
        act = _ffn_up(x_mid, mod3, lat_row, 3, 4, w_up_b, ffn_conv_w[l], ffn_conv_b[l], tn_up, n)
        x2 = _mm_resln(act, w_down_b, x_mid, mod3, lat_row_k, 5, ln2_g[l], ln2_b[l], alpha, tk_down)

        if ctx_out:
            xc_mid = _merge_out(ycs, pc2, g_col, wb, bg3, w_out_b, xc2, mod3, ctx_row_k, 2,
                                ln1_g[l], ln1_b[l], alpha)
            act_c = _ffn_up(xc_mid, mod3, ctx_row_seq, 3, 4, w_up_b, ffn_conv_w[l], ffn_conv_b[l],
                            tn_up, nctx)
            xc2 = _mm_resln(act_c, w_down_b, xc_mid, mod3, ctx_row_k, 5, ln2_g[l], ln2_b[l], alpha,
                            tk_down)

    return x2.reshape(bsz, n, d)
```

```python
import functools
import math

import numpy as np
import jax
import jax.numpy as jnp
from jax import lax
from jax.experimental import pallas as pl
from jax.experimental.pallas import tpu as pltpu

F32 = jnp.float32
BF16 = jnp.bfloat16

GRID_W = 64
HEAD_DIM = 64
SSD_HEADS = 8
SSD_D_INNER = 512
SSD_STATE = 128
SSD_GROUPS = 2
SSD_CHUNK = 128
SSD_XBC = 1024
NA_WIN_ROWS = 8
NA_WIN_COLS = 16
NA_Q_ROWS = 4
NA_K_ROWS = 12
SWA_WINDOW = 128
SWA_Q = 256
SWA_K = 512
FN_GROUP_DIM = 128
N_BRANCH = 4
BRANCH_WIDTH = 512
ROPE_BASE = 10000.0
LN_EPS = 1e-6
NEG = -1e30
DFT_ROWS = 64

LANE = 128
VMEM_LIMIT = 56 * 1024 * 1024


def _pick(dim, prefs):
    for p in prefs:
        if p <= dim and dim % p == 0:
            return p
    return dim


def _params(sem):
    return pltpu.CompilerParams(dimension_semantics=sem, vmem_limit_bytes=VMEM_LIMIT)


def _layer_norm(v):
    mu = jnp.mean(v, axis=-1, keepdims=True)
    vc = v - mu
    var = jnp.mean(vc * vc, axis=-1, keepdims=True)
    return vc * lax.rsqrt(var + LN_EPS)


def _sigmoid(v):
    return 0.5 * jnp.tanh(0.5 * v) + 0.5


def _silu(v):
    return v * _sigmoid(v)


def _cast_kernel(w_ref, o_ref):
    o_ref[...] = w_ref[...].astype(o_ref.dtype)


def _stage_weight(w_stack, l):
    _, r, c = w_stack.shape
    tr = _pick(r, (512, 256, 128))
    tc = _pick(c, (2048, 1408, 1024, 512, 256, 128))
    return pl.pallas_call(
        _cast_kernel,
        grid=(r // tr, c // tc),
        in_specs=[pl.BlockSpec((None, tr, tc), lambda i, j: (l, i, j))],
        out_specs=pl.BlockSpec((tr, tc), lambda i, j: (i, j)),
        out_shape=jax.ShapeDtypeStruct((r, c), BF16),
        compiler_params=_params(("parallel", "parallel")),
        name="stage_weight",
    )(w_stack)


def _ada_kernel(c_ref, w_ref, b_ref, o_ref):
    s = _silu(c_ref[...])
    o_ref[...] = jnp.dot(s.astype(BF16), w_ref[...].astype(BF16),
                         preferred_element_type=F32) + b_ref[...]


def _ada_mods(cvec, w_ada, b_ada):
    depth, d, n6 = w_ada.shape
    r = cvec.shape[0]
    tn = _pick(n6, (1024, 512, 256, 128))
    return pl.pallas_call(
        _ada_kernel,
        grid=(depth, n6 // tn),
        in_specs=[pl.BlockSpec((r, d), lambda l, j: (0, 0)),
                  pl.BlockSpec((None, d, tn), lambda l, j: (l, 0, j)),
                  pl.BlockSpec((None, 1, tn), lambda l, j: (l, 0, j))],
        out_specs=pl.BlockSpec((None, r, tn), lambda l, j: (l, 0, j)),
        out_shape=jax.ShapeDtypeStruct((depth, r, n6), F32),
        compiler_params=_params(("parallel", "parallel")),
        name="ada_mods",
    )(cvec, w_ada, b_ada.reshape(depth, 1, n6))


def _lnmod_mm_kernel(x_ref, sh_ref, sc_ref, w_ref, *rest, gated):
    if gated:
        b_ref, o_ref, h_ref = rest
    else:
        o_ref, h_ref = rest

    @pl.when(pl.program_id(1) == 0)
    def _():
        y = _layer_norm(x_ref[...])
        h_ref[...] = (y * (1.0 + sc_ref[...]) + sh_ref[...]).astype(BF16)

    acc = jnp.dot(h_ref[...], w_ref[...], preferred_element_type=F32)
    if gated:
        acc = _sigmoid(acc + b_ref[...])
    o_ref[...] = acc.astype(o_ref.dtype)


def _lnmod_matmul(x2, mod3, mod_row, sh_chunk, sc_chunk, w, tn, gate_bias=None):
    m, d = x2.shape
    n = w.shape[1]
    tm = mod_row.tm
    gated = gate_bias is not None
    in_specs = [pl.BlockSpec((tm, d), lambda i, j: (i, 0)),
                pl.BlockSpec((None, 1, d), lambda i, j: (mod_row(i), 0, sh_chunk)),
                pl.BlockSpec((None, 1, d), lambda i, j: (mod_row(i), 0, sc_chunk)),
                pl.BlockSpec((d, tn), lambda i, j: (0, j))]
    args = [x2, mod3, mod3, w]
    if gated:
        in_specs.append(pl.BlockSpec((1, tn), lambda i, j: (0, j)))
        args.append(gate_bias)
    return pl.pallas_call(
        functools.partial(_lnmod_mm_kernel, gated=gated),
        grid=(m // tm, n // tn),
        in_specs=in_specs,
        out_specs=pl.BlockSpec((tm, tn), lambda i, j: (i, j)),
        out_shape=jax.ShapeDtypeStruct((m, n), BF16),
        scratch_shapes=[pltpu.VMEM((tm, d), BF16)],
        compiler_params=_params(("parallel", "arbitrary")),
        name="lnmod_gates" if gated else "lnmod_matmul",
    )(*args)


class _ModRow:
    def __init__(self, tm, seq=None, fixed=None):
        self.tm, self.seq, self.fixed = tm, seq, fixed

    def __call__(self, i):
        if self.fixed is not None:
            return self.fixed
        return (i * self.tm) // self.seq


def _dwconv_rows(x, w_ref):
    n = x.shape[0]
    k = w_ref.shape[0]
    row = lax.broadcasted_iota(jnp.int32, x.shape, 0)
    acc = x * w_ref[k // 2:k // 2 + 1, :]
    for t in range(k):
        off = t - k // 2
        if off == 0:
            continue
        shifted = pltpu.roll(x, (-off) % n, 0)
        valid = (row + off >= 0) & (row + off < n)
        acc = acc + jnp.where(valid, shifted, 0.0) * w_ref[t:t + 1, :]
    return acc


def _conv_silu_kernel(x_ref, w_ref, b_ref, o_ref):
    y = _dwconv_rows(x_ref[...].astype(F32), w_ref) + b_ref[...]
    o_ref[...] = _silu(y).astype(o_ref.dtype)


def _conv_silu(src3, col0, width, conv_w, conv_b):
    b, n, _ = src3.shape
    tc = LANE
    c0 = col0 // tc
    kk = conv_w.shape[0]
    return pl.pallas_call(
        _conv_silu_kernel,
        grid=(b, width // tc),
        in_specs=[pl.BlockSpec((None, n, tc), lambda bi, j: (bi, 0, c0 + j)),
                  pl.BlockSpec((kk, tc), lambda bi, j: (0, j)),
                  pl.BlockSpec((1, tc), lambda bi, j: (0, j))],
        out_specs=pl.BlockSpec((None, n, tc), lambda bi, j: (bi, 0, j)),
        out_shape=jax.ShapeDtypeStruct((b, n, width), BF16),
        compiler_params=_params(("parallel", "parallel")),
        name="conv_silu",
    )(src3, conv_w, conv_b.reshape(1, width))


def _softplus(v):
    return jnp.maximum(v, 0.0) + jnp.log(1.0 + jnp.exp(-jnp.abs(v)))


def _ssd_direction(u_ref, dt_ref, a_row, dtb_row, s_ref, y_ref, d, reverse):
    q = SSD_CHUNK
    dt_all = _softplus(dt_ref[...].astype(F32) + dtb_row)
    cum = dt_all * a_row
    row = lax.broadcasted_iota(jnp.int32, (q, LANE), 0)
    s = 1
    while s < q:
        if reverse:
            cum = cum + jnp.where(row < q - s, pltpu.roll(cum, q - s, 0), 0.0)
        else:
            cum = cum + jnp.where(row >= s, pltpu.roll(cum, s, 0), 0.0)
        s *= 2
    tot = cum[0:1, :] if reverse else cum[q - 1:q, :]
    cum_t = cum.T
    dt_t = dt_all.T
    e_cum = jnp.exp(cum)
    w_end = jnp.exp(tot - cum) * dt_all
    e_tot = jnp.exp(tot)

    ri = lax.broadcasted_iota(jnp.int32, (q, q), 0)
    ci = lax.broadcasted_iota(jnp.int32, (q, q), 1)
    tri = (ri <= ci) if reverse else (ri >= ci)
    lane = lax.broadcasted_iota(jnp.int32, (q, LANE), 1)
    lo = lane < HEAD_DIM
    lane_row = lax.broadcasted_iota(jnp.int32, (1, LANE), 1)

    def pair_cols(v, c0, c1):
        return jnp.where(lo, jnp.broadcast_to(v[:, c0:c0 + 1], (q, LANE)),
                         jnp.broadcast_to(v[:, c1:c1 + 1], (q, LANE)))

    heads_per_group = SSD_HEADS // SSD_GROUPS
    for g in range(SSD_GROUPS):
        b0 = SSD_D_INNER + g * SSD_STATE
        c0 = SSD_D_INNER + SSD_GROUPS * SSD_STATE + g * SSD_STATE
        bg = u_ref[:, b0:b0 + SSD_STATE]
        cg = u_ref[:, c0:c0 + SSD_STATE]
        cb = lax.dot_general(cg, bg, (((1,), (1,)), ((), ())), preferred_element_type=F32)
        bg_t = bg.astype(F32).T.astype(BF16)
        for hp in range(heads_per_group // 2):
            h0 = g * heads_per_group + 2 * hp
            col0, col1 = d * SSD_HEADS + h0, d * SSD_HEADS + h0 + 1
            l0 = h0 * HEAD_DIM
            xp = u_ref[:, l0:l0 + LANE]
            ws = []
            for col in (col0, col1):
                seg = cum[:, col:col + 1] - cum_t[col:col + 1, :]
                dec = jnp.exp(jnp.where(tri, seg, NEG))
                ws.append((dec * cb * dt_t[col:col + 1, :]).astype(BF16))
            w_pair = jnp.concatenate(ws, axis=1)
            zero = jnp.zeros_like(xp)
            x_bd = jnp.concatenate([jnp.where(lo, xp, zero), jnp.where(lo, zero, xp)], axis=0)
            y_diag = jnp.dot(w_pair, x_bd, preferred_element_type=F32)
            st = s_ref[d, :, l0:l0 + LANE]
            y_off = jnp.dot(cg, st.astype(BF16), preferred_element_type=F32) * pair_cols(e_cum, col0, col1)
            y_ref[:, l0:l0 + LANE] = y_diag + y_off
            xw = (xp.astype(F32) * pair_cols(w_end, col0, col1)).astype(BF16)
            upd = jnp.dot(bg_t, xw, preferred_element_type=F32)
            tot_pair = jnp.where(lane_row < HEAD_DIM,
                                 jnp.broadcast_to(e_tot[:, col0:col0 + 1], (1, LANE)),
                                 jnp.broadcast_to(e_tot[:, col1:col1 + 1], (1, LANE)))
            s_ref[d, :, l0:l0 + LANE] = st * tot_pair + upd


def _ssd_kernel(uf_ref, ub_ref, dtf_ref, dtb_ref, alog_ref, dtbias_ref, h0_ref,
                yf_ref, yb_ref, hT_ref, s_ref):
    t = pl.program_id(1)

    @pl.when(t == 0)
    def _():
        s_ref[...] = h0_ref[...]

    a_row = -jnp.exp(alog_ref[...])
    dtb_row = dtbias_ref[...]
    _ssd_direction(uf_ref, dtf_ref, a_row, dtb_row, s_ref, yf_ref, 0, False)
    _ssd_direction(ub_ref, dtb_ref, a_row, dtb_row, s_ref, yb_ref, 1, True)

    @pl.when(t == pl.num_programs(1) - 1)
    def _():
        hT_ref[...] = s_ref[...]


def _ssd_scan(u3, p3, dt_col0, alog_row, dtbias_row, h0):
    b, n, _ = u3.shape
    q = SSD_CHUNK
    nt = n // q
    dtb = dt_col0 // LANE
    hp = SSD_D_INNER
    return pl.pallas_call(
        _ssd_kernel,
        grid=(b, nt),
        in_specs=[pl.BlockSpec((None, q, SSD_XBC), lambda bi, t: (bi, t, 0)),
                  pl.BlockSpec((None, q, SSD_XBC), lambda bi, t: (bi, nt - 1 - t, 0)),
                  pl.BlockSpec((None, q, LANE), lambda bi, t: (bi, t, dtb)),
                  pl.BlockSpec((None, q, LANE), lambda bi, t: (bi, nt - 1 - t, dtb)),
                  pl.BlockSpec((1, LANE), lambda bi, t: (0, 0)),
                  pl.BlockSpec((1, LANE), lambda bi, t: (0, 0)),
                  pl.BlockSpec((None, 2, SSD_STATE, hp), lambda bi, t: (bi, 0, 0, 0))],
        out_specs=[pl.BlockSpec((None, q, hp), lambda bi, t: (bi, t, 0)),
                   pl.BlockSpec((None, q, hp), lambda bi, t: (bi, nt - 1 - t, 0)),
                   pl.BlockSpec((None, 2, SSD_STATE, hp), lambda bi, t: (bi, 0, 0, 0))],
        out_shape=[jax.ShapeDtypeStruct((b, n, hp), F32),
                   jax.ShapeDtypeStruct((b, n, hp), F32),
                   jax.ShapeDtypeStruct((b, 2, SSD_STATE, hp), F32)],
        scratch_shapes=[pltpu.VMEM((2, SSD_STATE, hp), F32)],
        compiler_params=_params(("parallel", "arbitrary")),
        name="ssd_scan",
    )(u3, u3, p3, p3, alog_row, dtbias_row, h0)


def _ssd_finish_kernel(yf_ref, yb_ref, x_ref, z_ref, dskip_ref, g_ref, o_ref):
    y = dskip_ref[...] * x_ref[...].astype(F32) + yf_ref[...] + yb_ref[...]
    y = y * _silu(z_ref[...].astype(F32))
    r = lax.rsqrt(jnp.mean(y * y, axis=-1, keepdims=True) + LN_EPS)
    o_ref[...] = (y * r * g_ref[...]).astype(o_ref.dtype)


def _ssd_finish(yf2, yb2, u2, p2, z_col0, dskip_row, g_row):
    m, w = yf2.shape
    tm = _pick(m, (1024, 512, 256, 128))
    zb = z_col0 // w
    row = lambda i: (i, 0)
    return pl.pallas_call(
        _ssd_finish_kernel,
        grid=(m // tm,),
        in_specs=[pl.BlockSpec((tm, w), row), pl.BlockSpec((tm, w), row),
                  pl.BlockSpec((tm, w), row),
                  pl.BlockSpec((tm, w), lambda i: (i, zb)),
                  pl.BlockSpec((1, w), lambda i: (0, 0)),
                  pl.BlockSpec((1, w), lambda i: (0, 0))],
        out_specs=pl.BlockSpec((tm, w), row),
        out_shape=jax.ShapeDtypeStruct((m, w), BF16),
        compiler_params=_params(("parallel",)),
        name="ssd_finish",
    )(yf2, yb2, u2, p2, dskip_row, g_row)


def _block_diag_rows(kv):
    lane = lax.broadcasted_iota(jnp.int32, kv.shape, 1)
    lo = lane < HEAD_DIM
    zero = jnp.zeros_like(kv)
    return jnp.concatenate([jnp.where(lo, kv, zero), jnp.where(lo, zero, kv)], axis=0)


def _dup_group(kv, g):
    lane = lax.broadcasted_iota(jnp.int32, kv.shape, 1)
    rolled = pltpu.roll(kv, HEAD_DIM, 1)
    return jnp.where(lane // HEAD_DIM == g, kv, rolled)


def _qk(q, kbd):
    return lax.dot_general(q, kbd, (((1,), (1,)), ((), ())), preferred_element_type=F32)


def _pair_softmax_pv(s_w, s_c, bias, vbd, vcbd, sink, out_dtype):
    nk = s_w.shape[1] // 2
    nc = s_c.shape[1] // 2
    tq = s_w.shape[0]
    pws, pcs, invs = [], [], []
    for h in range(2):
        sw = s_w[:, h * nk:(h + 1) * nk] + bias[h]
        sc = s_c[:, h * nc:(h + 1) * nc]
        m = jnp.maximum(jnp.max(sw, axis=-1, keepdims=True), jnp.max(sc, axis=-1, keepdims=True))
        if sink is not None:
            m = jnp.maximum(m, sink[h])
        pw = jnp.exp(sw - m)
        pc = jnp.exp(sc - m)
        l = jnp.sum(pw, axis=-1, keepdims=True) + jnp.sum(pc, axis=-1, keepdims=True)
        if sink is not None:
            l = l + jnp.exp(sink[h] - m)
        pws.append(pw.astype(BF16))
        pcs.append(pc.astype(BF16))
        invs.append(1.0 / l)
    o = jnp.dot(jnp.concatenate(pws, axis=1), vbd, preferred_element_type=F32)
    o = o + jnp.dot(jnp.concatenate(pcs, axis=1), vcbd, preferred_element_type=F32)
    lane = lax.broadcasted_iota(jnp.int32, (tq, LANE), 1)
    inv = jnp.where(lane < HEAD_DIM, jnp.broadcast_to(invs[0], (tq, LANE)),
                    jnp.broadcast_to(invs[1], (tq, LANE)))
    return (o * inv).astype(out_dtype)


def _na_kernel(q_ref, k_ref, v_ref, kc_ref, vc_ref, bias_ref, o_ref, *, rows):
    qb = pl.program_id(2)
    nk = NA_K_ROWS * GRID_W
    ks = jnp.clip(NA_Q_ROWS * qb - (NA_K_ROWS - NA_Q_ROWS) // 2, 0, rows - NA_K_ROWS) * GRID_W
    ks = pl.multiple_of(ks, GRID_W)
    for p in range(q_ref.shape[1] // LANE):
        cs = slice(p * LANE, (p + 1) * LANE)
        q = q_ref[:, cs] * (HEAD_DIM ** -0.5)
        kbd = _block_diag_rows(k_ref[pl.ds(ks, nk), cs])
        vbd = _block_diag_rows(v_ref[pl.ds(ks, nk), cs])
        kcbd = _block_diag_rows(kc_ref[:, cs])
        vcbd = _block_diag_rows(vc_ref[:, cs])
        s_w = _qk(q, kbd)
        s_c = _qk(q, kcbd)
        o_ref[:, cs] = _pair_softmax_pv(s_w, s_c, [bias_ref[2 * p], bias_ref[2 * p + 1]], vbd, vcbd,
                                        None, o_ref.dtype)


def _na_key_start(qb, rows):
    return min(max(NA_Q_ROWS * qb - (NA_K_ROWS - NA_Q_ROWS) // 2, 0), rows - NA_K_ROWS)


def _na_bias_tables(rpb, rows):
    nqb = rows // NA_Q_ROWS
    variants = [0, 1 if nqb > 2 else 0, nqb - 1]
    nh = rpb.shape[0]
    w = GRID_W
    pad = w - NA_WIN_COLS
    rp = jnp.pad(rpb.astype(F32), ((0, 0), (0, 0), (pad, pad)))
    toep = jnp.stack([rp[:, :, w - 1 - qc:2 * w - 1 - qc] for qc in range(w)], axis=2)
    qc, kc = np.arange(w)[:, None], np.arange(w)[None, :]
    ws = np.clip(qc - NA_WIN_COLS // 2, 0, w - NA_WIN_COLS)
    toep = jnp.where(jnp.asarray((kc >= ws) & (kc < ws + NA_WIN_COLS)), toep, NEG)
    masked = jnp.full((nh, w, w), NEG, F32)
    tabs = []
    for qb in variants:
        block_rows = []
        for qr_l in range(NA_Q_ROWS):
            qr = NA_Q_ROWS * qb + qr_l
            rs = min(max(qr - NA_WIN_ROWS // 2, 0), rows - NA_WIN_ROWS)
            blocks = []
            for kr_l in range(NA_K_ROWS):
                kr = _na_key_start(qb, rows) + kr_l
                inside = rs <= kr < rs + NA_WIN_ROWS
                blocks.append(toep[:, kr - qr + NA_WIN_ROWS - 1] if inside else masked)
            block_rows.append(jnp.concatenate(blocks, axis=-1))
        tabs.append(jnp.concatenate(block_rows, axis=-2))
    return jnp.stack(tabs, axis=0)


def _na_attention(p3, pc3, q_col0, k_col0, v_col0, bias_tab):
    b, n, _ = p3.shape
    nc = pc3.shape[1]
    rows = n // GRID_W
    tq = NA_Q_ROWS * GRID_W
    nk = NA_K_ROWS * GRID_W
    nqb = n // tq
    wb = 2 * LANE
    qc, kc, vc = q_col0 // wb, k_col0 // wb, v_col0 // wb
    ngrp = 2

    def variant(qb):
        return jnp.where(qb == 0, 0, jnp.where(qb == nqb - 1, 2, 1))

    return pl.pallas_call(
        functools.partial(_na_kernel, rows=rows),
        grid=(b, ngrp, nqb),
        in_specs=[pl.BlockSpec((None, tq, wb), lambda bi, h, i: (bi, i, qc + h)),
                  pl.BlockSpec((None, n, wb), lambda bi, h, i: (bi, 0, kc + h)),
                  pl.BlockSpec((None, n, wb), lambda bi, h, i: (bi, 0, vc + h)),
                  pl.BlockSpec((None, nc, wb), lambda bi, h, i: (bi, 0, kc + h)),
                  pl.BlockSpec((None, nc, wb), lambda bi, h, i: (bi, 0, vc + h)),
                  pl.BlockSpec((None, 4, tq, nk), lambda bi, h, i: (variant(i), h, 0, 0))],
        out_specs=pl.BlockSpec((None, tq, wb), lambda bi, h, i: (bi, i, h)),
        out_shape=jax.ShapeDtypeStruct((b, n, ngrp * wb), BF16),
        compiler_params=_params(("parallel", "parallel", "arbitrary")),
        name="na_attention",
    )(p3, p3, p3, pc3, pc3, bias_tab)


def _rope(x, cos, sin_signed):
    lane = lax.broadcasted_iota(jnp.int32, x.shape, 1)
    first = (lane % 32) < 16
    partner = jnp.where(first, pltpu.roll(x, LANE - 16, 1), pltpu.roll(x, 16, 1))
    return x * cos + partner * sin_signed


def _sink_pair(sink_ref):
    s = sink_ref[...]
    return [s[:, 0:1], s[:, HEAD_DIM:HEAD_DIM + 1]]


def _swa_kernel(q_ref, k_ref, v_ref, kc_ref, vc_ref, cos_ref, sin_ref, sink_ref, o_ref, *, n):
    g = pl.program_id(1)
    qb = pl.program_id(2)
    q0 = pl.multiple_of(qb * SWA_Q, SWA_Q)
    ks = pl.multiple_of(jnp.clip(qb * SWA_Q - SWA_WINDOW, 0, n - SWA_K), SWA_WINDOW)
    cos_q, sin_q = cos_ref[pl.ds(q0, SWA_Q), :], sin_ref[pl.ds(q0, SWA_Q), :]
    kk = _rope(k_ref[pl.ds(ks, SWA_K), :].astype(F32), cos_ref[pl.ds(ks, SWA_K), :],
               sin_ref[pl.ds(ks, SWA_K), :])
    kbd = _block_diag_rows(_dup_group(kk, g).astype(BF16))
    vbd = _block_diag_rows(_dup_group(v_ref[pl.ds(ks, SWA_K), :].astype(F32), g).astype(BF16))
    kcbd = _block_diag_rows(_dup_group(kc_ref[...].astype(F32), g).astype(BF16))
    vcbd = _block_diag_rows(_dup_group(vc_ref[...].astype(F32), g).astype(BF16))
    qpos = q0 + lax.broadcasted_iota(jnp.int32, (SWA_Q, SWA_K), 0)
    kpos = ks + lax.broadcasted_iota(jnp.int32, (SWA_Q, SWA_K), 1)
    bias = jnp.where(jnp.abs(kpos - qpos) <= SWA_WINDOW, 0.0, NEG)
    for p in range(q_ref.shape[1] // LANE):
        cs = slice(p * LANE, (p + 1) * LANE)
        q = _rope(q_ref[:, cs].astype(F32), cos_q, sin_q)
        q = (q * (HEAD_DIM ** -0.5)).astype(BF16)
        s_w = _qk(q, kbd)
        s_c = _qk(q, kcbd)
        o_ref[:, cs] = _pair_softmax_pv(s_w, s_c, [bias, bias], vbd, vcbd, _sink_pair(sink_ref.at[p]),
                                        o_ref.dtype)


def _swa_attention(p3, pc3, q_col0, kv_col0, cos_tab, sin_tab, sink_rows):
    b, n, _ = p3.shape
    nc = pc3.shape[1]
    nqb = n // SWA_Q
    wb = 2 * LANE
    qc, kc = q_col0 // wb, kv_col0 // LANE
    ngrp = 2
    return pl.pallas_call(
        functools.partial(_swa_kernel, n=n),
        grid=(b, ngrp, nqb),
        in_specs=[pl.BlockSpec((None, SWA_Q, wb), lambda bi, h, i: (bi, i, qc + h)),
                  pl.BlockSpec((None, n, LANE), lambda bi, h, i: (bi, 0, kc)),
                  pl.BlockSpec((None, n, LANE), lambda bi, h, i: (bi, 0, kc + 1)),
                  pl.BlockSpec((None, nc, LANE), lambda bi, h, i: (bi, 0, kc)),
                  pl.BlockSpec((None, nc, LANE), lambda bi, h, i: (bi, 0, kc + 1)),
                  pl.BlockSpec((n, LANE), lambda bi, h, i: (0, 0)),
                  pl.BlockSpec((n, LANE), lambda bi, h, i: (0, 0)),
                  pl.BlockSpec((2, 1, LANE), lambda bi, h, i: (h, 0, 0))],
        out_specs=pl.BlockSpec((None, SWA_Q, wb), lambda bi, h, i: (bi, i, h)),
        out_shape=jax.ShapeDtypeStruct((b, n, ngrp * wb), BF16),
        compiler_params=_params(("parallel", "parallel", "arbitrary")),
        name="swa_attention",
    )(p3, p3, p3, pc3, pc3, cos_tab, sin_tab, sink_rows)


def _ctx_attn_kernel(q_ref, k_ref, v_ref, sink_ref, o_ref, *, grouped):
    hp = pl.program_id(1)
    q = q_ref[...] * (HEAD_DIM ** -0.5)
    k, v = k_ref[...], v_ref[...]
    if grouped:
        g = hp // 2
        k = _dup_group(k.astype(F32), g).astype(BF16)
        v = _dup_group(v.astype(F32), g).astype(BF16)
    kbd, vbd = _block_diag_rows(k), _block_diag_rows(v)
    s = _qk(q, kbd)
    t = q.shape[0]
    sink = _sink_pair(sink_ref) if grouped else None
    ps, invs = [], []
    for h in range(2):
        sh = s[:, h * t:(h + 1) * t]
        m = jnp.max(sh, axis=-1, keepdims=True)
        if sink is not None:
            m = jnp.maximum(m, sink[h])
        p = jnp.exp(sh - m)
        l = jnp.sum(p, axis=-1, keepdims=True)
        if sink is not None:
            l = l + jnp.exp(sink[h] - m)
        ps.append(p.astype(BF16))
        invs.append(1.0 / l)
    o = jnp.dot(jnp.concatenate(ps, axis=1), vbd, preferred_element_type=F32)
    lane = lax.broadcasted_iota(jnp.int32, (t, LANE), 1)
    inv = jnp.where(lane < HEAD_DIM, jnp.broadcast_to(invs[0], (t, LANE)),
                    jnp.broadcast_to(invs[1], (t, LANE)))
    o_ref[...] = (o * inv).astype(o_ref.dtype)


def _ctx_attention(pc3, q_col0, k_col0, v_col0, sink_rows, grouped):
    b, nc, _ = pc3.shape
    qc, kc, vc = q_col0 // LANE, k_col0 // LANE, v_col0 // LANE
    npair = 4
    kv_blk = (lambda h: 0) if grouped else (lambda h: h)
    return pl.pallas_call(
        functools.partial(_ctx_attn_kernel, grouped=grouped),
        grid=(b, npair),
        in_specs=[pl.BlockSpec((None, nc, LANE), lambda bi, h: (bi, 0, qc + h)),
                  pl.BlockSpec((None, nc, LANE), lambda bi, h: (bi, 0, kc + kv_blk(h))),
                  pl.BlockSpec((None, nc, LANE), lambda bi, h: (bi, 0, vc + kv_blk(h))),
                  pl.BlockSpec((None, 1, LANE), lambda bi, h: (h, 0, 0))],
        out_specs=pl.BlockSpec((None, nc, LANE), lambda bi, h: (bi, 0, h)),
        out_shape=jax.ShapeDtypeStruct((b, nc, npair * LANE), BF16),
        compiler_params=_params(("parallel", "parallel")),
        name="ctx_attention",
    )(pc3, pc3, pc3, sink_rows)


def _fn_channel_kernel(x_ref, cs_ref, o_ref):
    w = x_ref.shape[1]
    ngroups = w // FN_GROUP_DIM
    for g in range(ngroups):
        xg = x_ref[:, g * FN_GROUP_DIM:(g + 1) * FN_GROUP_DIM]
        ab = jnp.dot(xg, cs_ref[...], preferred_element_type=F32)
        o_ref[:, g * FN_GROUP_DIM:(g + 1) * FN_GROUP_DIM] = ab[:, :FN_GROUP_DIM].astype(o_ref.dtype)
        o_ref[:, w + g * FN_GROUP_DIM:w + (g + 1) * FN_GROUP_DIM] = ab[:, FN_GROUP_DIM:].astype(o_ref.dtype)


def _fn_channel(p2, col0, width):
    m = p2.shape[0]
    tm = _pick(m, (1024, 512, 256, 128))
    k = np.arange(FN_GROUP_DIM)
    ang = 2.0 * np.pi * ((k[:, None] * k[None, :]) % FN_GROUP_DIM) / FN_GROUP_DIM
    cs = jnp.asarray(np.concatenate([np.cos(ang), np.sin(ang)], axis=1), BF16)
    cb = col0 // width
    return pl.pallas_call(
        _fn_channel_kernel,
        grid=(m // tm,),
        in_specs=[pl.BlockSpec((tm, width), lambda i: (i, cb)),
                  pl.BlockSpec((FN_GROUP_DIM, 2 * FN_GROUP_DIM), lambda i: (0, 0))],
        out_specs=pl.BlockSpec((tm, 2 * width), lambda i: (i, 0)),
        out_shape=jax.ShapeDtypeStruct((m, 2 * width), BF16),
        compiler_params=_params(("parallel",)),
        name="fn_channel",
    )(p2, cs)


def _fn_position_kernel(ac_ref, as_ref, bc_ref, bs_ref, ab_ref, o_ref, c_ref, s_ref, *, scale):
    @pl.when(pl.program_id(1) == 0)
    def _():
        for j in range(ac_ref.shape[0]):
            ca, sa = ac_ref[j:j + 1, :], as_ref[j:j + 1, :]
            cb, sb = bc_ref[...], bs_ref[...]
            c_ref[j * DFT_ROWS:(j + 1) * DFT_ROWS, :] = (ca * cb - sa * sb).astype(BF16)
            s_ref[j * DFT_ROWS:(j + 1) * DFT_ROWS, :] = (sa * cb + ca * sb).astype(BF16)

    w = o_ref.shape[1]
    y = jnp.dot(c_ref[...], ab_ref[:, :w], preferred_element_type=F32)
    y = y - jnp.dot(s_ref[...], ab_ref[:, w:], preferred_element_type=F32)
    o_ref[...] = (y * scale).astype(o_ref.dtype)


def _fn_position(ab3):
    b, n, w2 = ab3.shape
    w = w2 // 2
    tm = _pick(n, (512, 256, 128, 64))
    jc = tm // DFT_ROWS
    n1 = n // DFT_ROWS
    k = np.arange(n, dtype=np.int64)
    j1 = np.arange(n1, dtype=np.int64)
    j2 = np.arange(DFT_ROWS, dtype=np.int64)
    ang_a = 2.0 * np.pi * ((j1[:, None] * DFT_ROWS * k[None, :]) % n) / n
    ang_b = 2.0 * np.pi * ((j2[:, None] * k[None, :]) % n) / n
    ac, as_ = jnp.asarray(np.cos(ang_a), F32), jnp.asarray(np.sin(ang_a), F32)
    bc, bs = jnp.asarray(np.cos(ang_b), F32), jnp.asarray(np.sin(ang_b), F32)
    scale = 1.0 / math.sqrt(n * FN_GROUP_DIM)
    if n1 % 8 != 0 and jc != n1:
        raise ValueError("unsupported sequence length for the position DFT tiling")
    return pl.pallas_call(
        functools.partial(_fn_position_kernel, scale=scale),
        grid=(n // tm, b),
        in_specs=[pl.BlockSpec((jc, n), lambda i, bi: (i, 0)),
                  pl.BlockSpec((jc, n), lambda i, bi: (i, 0)),
                  pl.BlockSpec((DFT_ROWS, n), lambda i, bi: (0, 0)),
                  pl.BlockSpec((DFT_ROWS, n), lambda i, bi: (0, 0)),
                  pl.BlockSpec((None, n, w2), lambda i, bi: (bi, 0, 0))],
        out_specs=pl.BlockSpec((None, tm, w), lambda i, bi: (bi, i, 0)),
        out_shape=jax.ShapeDtypeStruct((b, n, w), BF16),
        scratch_shapes=[pltpu.VMEM((tm, n), BF16), pltpu.VMEM((tm, n), BF16)],
        compiler_params=_params(("parallel", "arbitrary")),
        name="fn_position",
    )(ac, as_, bc, bs, ab3)


def _residual_ln(o_ref, x_ref, gate_ref, lng_ref, lnb_ref, alpha):
    v = alpha * x_ref[...] + gate_ref[...] * o_ref[...]
    o_ref[...] = _layer_norm(v) * lng_ref[...] + lnb_ref[...]


def _merge_out_kernel(y0, y1, y2, y3, g0, g1, g2, g3, wb_ref, wo_ref, x_ref, gate_ref, lng_ref,
                      lnb_ref, o_ref, *, alpha):
    j = pl.program_id(1)

    @pl.when(j == 0)
    def _():
        o_ref[...] = jnp.zeros_like(o_ref)

    merged = None
    for i, (y, g) in enumerate(((y0, g0), (y1, g1), (y2, g2), (y3, g3))):
        term = g[...].astype(F32) * jnp.dot(y[...], wb_ref[i], preferred_element_type=F32)
        merged = term if merged is None else merged + term
    o_ref[...] += jnp.dot(merged.astype(BF16), wo_ref[...], preferred_element_type=F32)

    @pl.when(j == pl.num_programs(1) - 1)
    def _():
        _residual_ln(o_ref, x_ref, gate_ref, lng_ref, lnb_ref, alpha)


def _merge_out(ys, gates2, wb, w_out, x2, mod3, mod_row, gate_chunk, ln_g, ln_b, alpha):
    m = gates2.shape[0]
    d = wb.shape[2]
    tm = mod_row.tm
    tn = _pick(d, (1024, 512, 256, 128))
    per = d // tn
    bw = ys[0].shape[1]
    y_specs = [pl.BlockSpec((tm, bw), lambda i, j: (i, 0)) for _ in range(N_BRANCH)]
    g_specs = [pl.BlockSpec((tm, tn), functools.partial(lambda i, j, br: (i, br * per + j), br=br))
               for br in range(N_BRANCH)]
    return pl.pallas_call(
        functools.partial(_merge_out_kernel, alpha=alpha),
        grid=(m // tm, per),
        in_specs=y_specs + g_specs + [
            pl.BlockSpec((N_BRANCH, bw, tn), lambda i, j: (0, 0, j)),
            pl.BlockSpec((tn, d), lambda i, j: (j, 0)),
            pl.BlockSpec((tm, d), lambda i, j: (i, 0)),
            pl.BlockSpec((None, 1, d), lambda i, j: (mod_row(i), 0, gate_chunk)),
            pl.BlockSpec((1, d), lambda i, j: (0, 0)),
            pl.BlockSpec((1, d), lambda i, j: (0, 0))],
        out_specs=pl.BlockSpec((tm, d), lambda i, j: (i, 0)),
        out_shape=jax.ShapeDtypeStruct((m, d), F32),
        compiler_params=_params(("parallel", "arbitrary")),
        name="merge_out",
    )(*ys, gates2, gates2, gates2, gates2, wb, w_out, x2, mod3, ln_g.reshape(1, d), ln_b.reshape(1, d))


def _mm_resln_kernel(a_ref, w_ref, x_ref, gate_ref, lng_ref, lnb_ref, o_ref, *, alpha):
    k = pl.program_id(1)

    @pl.when(k == 0)
    def _():
        o_ref[...] = jnp.zeros_like(o_ref)

    o_ref[...] += jnp.dot(a_ref[...], w_ref[...], preferred_element_type=F32)

    @pl.when(k == pl.num_programs(1) - 1)
    def _():
        _residual_ln(o_ref, x_ref, gate_ref, lng_ref, lnb_ref, alpha)


def _mm_resln(a2, w, x2, mod3, mod_row, gate_chunk, ln_g, ln_b, alpha, tk):
    m, kdim = a2.shape
    d = w.shape[1]
    tm = mod_row.tm
    return pl.pallas_call(
        functools.partial(_mm_resln_kernel, alpha=alpha),
        grid=(m // tm, kdim // tk),
        in_specs=[pl.BlockSpec((tm, tk), lambda i, k: (i, k)),
                  pl.BlockSpec((tk, d), lambda i, k: (k, 0)),
                  pl.BlockSpec((tm, d), lambda i, k: (i, 0)),
                  pl.BlockSpec((None, 1, d), lambda i, k: (mod_row(i), 0, gate_chunk)),
                  pl.BlockSpec((1, d), lambda i, k: (0, 0)),
                  pl.BlockSpec((1, d), lambda i, k: (0, 0))],
        out_specs=pl.BlockSpec((tm, d), lambda i, k: (i, 0)),
        out_shape=jax.ShapeDtypeStruct((m, d), F32),
        compiler_params=_params(("parallel", "arbitrary")),
        name="matmul_residual_ln",
    )(a2, w, x2, mod3, ln_g.reshape(1, d), ln_b.reshape(1, d))


HALO_ROWS = 8


def _ffn_up_kernel(x_ref, xp_ref, xn_ref, sh_ref, sc_ref, wg_ref, wu_ref, cw_ref, cb_ref, o_ref, h_ref,
                   *, seq):
    i = pl.program_id(0)
    tm = x_ref.shape[0]

    @pl.when(pl.program_id(1) == 0)
    def _():
        scale, shift = 1.0 + sc_ref[...], sh_ref[...]
        h_ref[0:tm, :] = (_layer_norm(x_ref[...]) * scale + shift).astype(BF16)
        halo = jnp.concatenate([xp_ref[...], xn_ref[...]], axis=0)
        h_ref[tm:tm + 2 * HALO_ROWS, :] = (_layer_norm(halo) * scale + shift).astype(BF16)

    g_all = jnp.dot(h_ref[...], wg_ref[...], preferred_element_type=F32)
    u = jnp.dot(h_ref[0:tm, :], wu_ref[...], preferred_element_type=F32)
    g = g_all[0:tm, :]
    has_prev = jnp.where((i * tm) % seq == 0, 0.0, 1.0)
    has_next = jnp.where(((i + 1) * tm) % seq == 0, 0.0, 1.0)
    g_prev = g_all[tm + HALO_ROWS - 1:tm + HALO_ROWS, :] * has_prev
    g_next = g_all[tm + HALO_ROWS:tm + HALO_ROWS + 1, :] * has_next
    row = lax.broadcasted_iota(jnp.int32, g.shape, 0)
    above = jnp.where(row == 0, g_prev, pltpu.roll(g, 1, 0))
    below = jnp.where(row == tm - 1, g_next, pltpu.roll(g, tm - 1, 0))
    y = above * cw_ref[0:1, :] + g * cw_ref[1:2, :] + below * cw_ref[2:3, :] + cb_ref[...]
    o_ref[...] = (_silu(y) * u).astype(o_ref.dtype)


def _ffn_up(x2, mod3, mod_row, sh_chunk, sc_chunk, w_up, conv_w, conv_b, tn, seq):
    m, d = x2.shape
    dff = w_up.shape[1] // 2
    tm = mod_row.tm
    if conv_w.shape[0] != 3 or seq % tm != 0 or tm % HALO_ROWS != 0:
        raise ValueError("unsupported ConvFFN tiling")
    ub = dff // tn
    hb = tm // HALO_ROWS
    last_hb = m // HALO_ROWS - 1
    return pl.pallas_call(
        functools.partial(_ffn_up_kernel, seq=seq),
        grid=(m // tm, dff // tn),
        in_specs=[pl.BlockSpec((tm, d), lambda i, j: (i, 0)),
                  pl.BlockSpec((HALO_ROWS, d), lambda i, j: (jnp.maximum(i * hb - 1, 0), 0)),
                  pl.BlockSpec((HALO_ROWS, d), lambda i, j: (jnp.minimum((i + 1) * hb, last_hb), 0)),
                  pl.BlockSpec((None, 1, d), lambda i, j: (mod_row(i), 0, sh_chunk)),
                  pl.BlockSpec((None, 1, d), lambda i, j: (mod_row(i), 0, sc_chunk)),
                  pl.BlockSpec((d, tn), lambda i, j: (0, j)),
                  pl.BlockSpec((d, tn), lambda i, j: (0, ub + j)),
                  pl.BlockSpec((3, tn), lambda i, j: (0, j)),
                  pl.BlockSpec((1, tn), lambda i, j: (0, j))],
        out_specs=pl.BlockSpec((tm, tn), lambda i, j: (i, j)),
        out_shape=jax.ShapeDtypeStruct((m, dff), BF16),
        scratch_shapes=[pltpu.VMEM((tm + 2 * HALO_ROWS, d), BF16)],
        compiler_params=_params(("parallel", "arbitrary")),
        name="ffn_up",
    )(x2, x2, x2, mod3, mod3, w_up, w_up, conv_w, conv_b.reshape(1, dff))


def _rope_tables(n):
    t = jnp.arange(n)
    rows = (t // GRID_W).astype(F32)
    cols = (t % GRID_W).astype(F32)
    n_freq = HEAD_DIM // 4
    inv = ROPE_BASE ** (-jnp.arange(n_freq, dtype=F32) / n_freq)
    ar, ac = rows[:, None] * inv, cols[:, None] * inv
    cos = jnp.concatenate([jnp.cos(ar), jnp.cos(ar), jnp.cos(ac), jnp.cos(ac)], axis=1)
    sin = jnp.concatenate([-jnp.sin(ar), jnp.sin(ar), -jnp.sin(ac), jnp.sin(ac)], axis=1)
    return jnp.tile(cos, (1, 2)), jnp.tile(sin, (1, 2))


def _branch_params(conv_w, conv_b, a_log, dt_bias, d_skip, norm_g, rpb, sink, rows, cos_tab, sin_tab):
    pad = jnp.zeros((1, LANE - 2 * SSD_HEADS), F32)
    return dict(
        conv_w=conv_w, conv_b=conv_b,
        alog_row=jnp.concatenate([a_log.reshape(1, -1), pad], axis=1),
        dtb_row=jnp.concatenate([dt_bias.reshape(1, -1), pad], axis=1),
        dskip_row=jnp.repeat(d_skip, HEAD_DIM).reshape(1, SSD_D_INNER),
        g_row=norm_g.reshape(1, SSD_D_INNER),
        sink_rows=jnp.repeat(sink, HEAD_DIM).reshape(4, 1, LANE),
        bias_tab=_na_bias_tables(rpb, rows),
        cos_tab=cos_tab, sin_tab=sin_tab)


def _mixer_branches(p3, pc3, cols, prm, ctx_out):
    bsz, n, npad = p3.shape
    nctx = pc3.shape[1]
    p2 = p3.reshape(bsz * n, npad)
    pc2 = pc3.reshape(bsz * nctx, npad)
    flat = lambda t: t.reshape(-1, t.shape[-1])

    u3 = _conv_silu(p3, cols["xbc"], SSD_XBC, prm["conv_w"], prm["conv_b"])
    uc3 = _conv_silu(pc3, cols["xbc"], SSD_XBC, prm["conv_w"], prm["conv_b"])
    h_zero = jnp.zeros((bsz, 2, SSD_STATE, SSD_D_INNER), F32)
    yfc, ybc, h_ctx = _ssd_scan(uc3, pc3, cols["dt"], prm["alog_row"], prm["dtb_row"], h_zero)
    yf, yb, _ = _ssd_scan(u3, p3, cols["dt"], prm["alog_row"], prm["dtb_row"], h_ctx)
    y_ssd = _ssd_finish(flat(yf), flat(yb), flat(u3), p2, cols["z"], prm["dskip_row"], prm["g_row"])
    y_na = flat(_na_attention(p3, pc3, cols["nq"], cols["nk"], cols["nv"], prm["bias_tab"]))
    y_swa = flat(_swa_attention(p3, pc3, cols["sq"], cols["skv"], prm["cos_tab"], prm["sin_tab"],
                                prm["sink_rows"]))
    y_fn = flat(_fn_position(_fn_channel(p2, cols["fn"], 512).reshape(bsz, n, -1)))
    ys = [y_ssd, y_na, y_swa, y_fn]
    if not ctx_out:
        return ys, None
    yc_ssd = _ssd_finish(flat(yfc), flat(ybc), flat(uc3), pc2, cols["z"], prm["dskip_row"], prm["g_row"])
    yc_na = flat(_ctx_attention(pc3, cols["nq"], cols["nk"], cols["nv"], prm["sink_rows"], False))
    yc_swa = flat(_ctx_attention(pc3, cols["sq"], cols["skv"], cols["skv"] + LANE, prm["sink_rows"], True))
    yc_fn = flat(_fn_position(_fn_channel(pc2, cols["fn"], 512).reshape(bsz, nctx, -1)))
    return ys, [yc_ssd, yc_na, yc_swa, yc_fn]


def kernel(x, c, ctx, c_ctx, w_ada, b_ada, w_in, b_gate, ssd_conv_w, ssd_conv_b, ssd_a_log,
           ssd_dt_bias, ssd_d, ssd_norm_g, na_rpb, swa_sink, w_branch, w_out, ln1_g, ln1_b,
           ln2_g, ln2_b, ffn_w_up, ffn_conv_w, ffn_conv_b, ffn_w_down):
    bsz, n, d = x.shape
    nctx = ctx.shape[1]
    depth = w_ada.shape[0]
    dff = ffn_w_down.shape[1]
    alpha = (2.0 * depth) ** 0.25
    rows = n // GRID_W

    z_col = 0
    fn_col = z_col + 512
    sq_col = fn_col + 512
    nq_col = sq_col + 512
    nk_col = nq_col + 512
    nv_col = nk_col + 512
    xbc_col = nv_col + 512
    skv_col = xbc_col + SSD_XBC
    dt_col = skv_col + 256
    n_used = dt_col + LANE
    tn_in = 1536
    n_in = -(-n_used // tn_in) * tn_in
    tn_gate = _pick(N_BRANCH * d, (1024, 512, 256))
    cols = dict(z=z_col, fn=fn_col, sq=sq_col, nq=nq_col, nk=nk_col, nv=nv_col, xbc=xbc_col,
                skv=skv_col, dt=dt_col)

    o_z, o_xbc, o_dt = 0, 512, 512 + SSD_XBC
    o_na = o_dt + 2 * SSD_HEADS
    o_sq = o_na + 3 * 512
    o_skv = o_sq + 512
    o_fn = o_skv + 256
    o_g = o_fn + 512

    def stage_w_in(w):
        parts = [w[:, o_z:o_z + 512], w[:, o_fn:o_fn + 512], w[:, o_sq:o_sq + 512],
                 w[:, o_na:o_na + 3 * 512], w[:, o_xbc:o_xbc + SSD_XBC], w[:, o_skv:o_skv + 256],
                 w[:, o_dt:o_dt + 2 * SSD_HEADS],
                 jnp.zeros((d, n_in - n_used + LANE - 2 * SSD_HEADS), w.dtype)]
        return (jnp.concatenate(parts, axis=1).astype(BF16),
                w[:, o_g:o_g + N_BRANCH * d].astype(BF16))

    n_rows = -(-(bsz + 1) // 8) * 8
    cvec = jnp.concatenate([c, c_ctx[None], jnp.zeros((n_rows - bsz - 1, d), F32)], axis=0)
    mods = _ada_mods(cvec, w_ada, b_ada)

    lat_row = _ModRow(_pick(n, (1024, 512, 256)), seq=n)
    ctx_row = _ModRow(_pick(bsz * nctx, (1024, 512, 256)), fixed=bsz)
    lat_row_h = _ModRow(_pick(n, (512, 256)), seq=n)
    ctx_row_h = _ModRow(_pick(bsz * nctx, (512, 256)), fixed=bsz)
    ctx_row_seq = _ModRow(_pick(nctx, (1024, 512, 256)), fixed=bsz)
    tn_up = _pick(dff, (512, 256, 128))
    tk_down = _pick(dff, (512, 256, 128))

    cos_tab, sin_tab = _rope_tables(n)

    x2 = x.reshape(bsz * n, d)
    xc2 = ctx.reshape(bsz * nctx, d)
    for l in range(depth):
        ctx_out = l < depth - 1
        mod3 = mods[l].reshape(n_rows, 1, 6 * d)
        w_in_p, w_gate_b = stage_w_in(w_in[l])
        wb = _stage_weight(w_branch.reshape(depth, N_BRANCH * BRANCH_WIDTH, d), l)
        wb = wb.reshape(N_BRANCH, BRANCH_WIDTH, d)
        bg_row = b_gate[l].reshape(1, N_BRANCH * d)
        w_out_b = _stage_weight(w_out, l)
        w_up_b = _stage_weight(ffn_w_up, l)
        w_down_b = _stage_weight(ffn_w_down, l)
        prm = _branch_params(ssd_conv_w[l], ssd_conv_b[l], ssd_a_log[l], ssd_dt_bias[l], ssd_d[l],
                             ssd_norm_g[l], na_rpb[l], swa_sink[l], rows, cos_tab, sin_tab)

        p2 = _lnmod_matmul(x2, mod3, lat_row, 0, 1, w_in_p, tn_in)
        pc2 = _lnmod_matmul(xc2, mod3, ctx_row, 0, 1, w_in_p, tn_in)
        p3 = p2.reshape(bsz, n, n_in)
        pc3 = pc2.reshape(bsz, nctx, n_in)

        ys, ycs = _mixer_branches(p3, pc3, cols, prm, ctx_out)

        gates = _lnmod_matmul(x2, mod3, lat_row, 0, 1, w_gate_b, tn_gate, gate_bias=bg_row)
        x_mid = _merge_out(ys, gates, wb, w_out_b, x2, mod3, lat_row_h, 2, ln1_g[l], ln1_b[l], alpha)
        act = _ffn_up(x_mid, mod3, lat_row, 3, 4, w_up_b, ffn_conv_w[l], ffn_conv_b[l], tn_up, n)
        x2 = _mm_resln(act, w_down_b, x_mid, mod3, lat_row, 5, ln2_g[l], ln2_b[l], alpha, tk_down)

        if ctx_out:
            gates_c = _lnmod_matmul(xc2, mod3, ctx_row, 0, 1, w_gate_b, tn_gate, gate_bias=bg_row)
            xc_mid = _merge_out(ycs, gates_c, wb, w_out_b, xc2, mod3, ctx_row_h, 2, ln1_g[l], ln1_b[l],
                                alpha)
            act_c = _ffn_up(xc_mid, mod3, ctx_row_seq, 3, 4, w_up_b, ffn_conv_w[l], ffn_conv_b[l],
                            tn_up, nctx)
            xc2 = _mm_resln(act_c, w_down_b, xc_mid, mod3, ctx_row, 5, ln2_g[l], ln2_b[l], alpha,
                            tk_down)

    return x2.reshape(bsz, n, d)
```

```python
import functools
import math

import numpy as np
import jax
import jax.numpy as jnp
from jax import lax
from jax.experimental import pallas as pl
from jax.experimental.pallas import tpu as pltpu

F32 = jnp.float32
BF16 = jnp.bfloat16

GRID_W = 64
HEAD_DIM = 64
SSD_HEADS = 8
SSD_D_INNER = 512
SSD_STATE = 128
SSD_GROUPS = 2
SSD_CHUNK = 128
SSD_XBC = 1024
NA_WIN_ROWS = 8
NA_WIN_COLS = 16
NA_Q_ROWS = 4
NA_K_ROWS = 12
SWA_WINDOW = 128
SWA_Q = 256
SWA_K = 512
FN_GROUP_DIM = 128
N_BRANCH = 4
BRANCH_WIDTH = 512
ROPE_BASE = 10000.0
LN_EPS = 1e-6
NEG = -1e30
DFT_ROWS = 64

LANE = 128
VMEM_LIMIT = 56 * 1024 * 1024


def _pick(dim, prefs):
    for p in prefs:
        if p <= dim and dim % p == 0:
            return p
    return dim


def _params(sem):
    return pltpu.CompilerParams(dimension_semantics=sem, vmem_limit_bytes=VMEM_LIMIT)


def _layer_norm(v):
    mu = jnp.mean(v, axis=-1, keepdims=True)
    vc = v - mu
    var = jnp.mean(vc * vc, axis=-1, keepdims=True)
    return vc * lax.rsqrt(var + LN_EPS)


def _sigmoid(v):
    return 0.5 * jnp.tanh(0.5 * v) + 0.5


def _silu(v):
    return v * _sigmoid(v)


def _cast_kernel(w_ref, o_ref):
    o_ref[...] = w_ref[...].astype(o_ref.dtype)


def _stage_weight(w_stack, l):
    _, r, c = w_stack.shape
    tr = _pick(r, (512, 256, 128))
    tc = _pick(c, (2048, 1408, 1024, 512, 256, 128))
    return pl.pallas_call(
        _cast_kernel,
        grid=(r // tr, c // tc),
        in_specs=[pl.BlockSpec((None, tr, tc), lambda i, j: (l, i, j))],
        out_specs=pl.BlockSpec((tr, tc), lambda i, j: (i, j)),
        out_shape=jax.ShapeDtypeStruct((r, c), BF16),
        compiler_params=_params(("parallel", "parallel")),
        name="stage_weight",
    )(w_stack)


def _ada_kernel(c_ref, w_ref, b_ref, o_ref):
    s = _silu(c_ref[...])
    o_ref[...] = jnp.dot(s.astype(BF16), w_ref[...].astype(BF16),
                         preferred_element_type=F32) + b_ref[...]


def _ada_mods(cvec, w_ada, b_ada):
    depth, d, n6 = w_ada.shape
    r = cvec.shape[0]
    tn = _pick(n6, (1024, 512, 256, 128))
    return pl.pallas_call(
        _ada_kernel,
        grid=(depth, n6 // tn),
        in_specs=[pl.BlockSpec((r, d), lambda l, j: (0, 0)),
                  pl.BlockSpec((None, d, tn), lambda l, j: (l, 0, j)),
                  pl.BlockSpec((None, 1, tn), lambda l, j: (l, 0, j))],
        out_specs=pl.BlockSpec((None, r, tn), lambda l, j: (l, 0, j)),
        out_shape=jax.ShapeDtypeStruct((depth, r, n6), F32),
        compiler_params=_params(("parallel", "parallel")),
        name="ada_mods",
    )(cvec, w_ada, b_ada.reshape(depth, 1, n6))


def _lnmod_mm_kernel(x_ref, sh_ref, sc_ref, w_ref, o_ref, h_ref):
    @pl.when(pl.program_id(1) == 0)
    def _():
        y = _layer_norm(x_ref[...])
        h_ref[...] = (y * (1.0 + sc_ref[...]) + sh_ref[...]).astype(BF16)

    o_ref[...] = jnp.dot(h_ref[...], w_ref[...], preferred_element_type=F32).astype(o_ref.dtype)


def _lnmod_matmul(x2, mod3, mod_row, sh_chunk, sc_chunk, w, tn):
    m, d = x2.shape
    n = w.shape[1]
    tm = mod_row.tm
    return pl.pallas_call(
        _lnmod_mm_kernel,
        grid=(m // tm, n // tn),
        in_specs=[pl.BlockSpec((tm, d), lambda i, j: (i, 0)),
                  pl.BlockSpec((None, 1, d), lambda i, j: (mod_row(i), 0, sh_chunk)),
                  pl.BlockSpec((None, 1, d), lambda i, j: (mod_row(i), 0, sc_chunk)),
                  pl.BlockSpec((d, tn), lambda i, j: (0, j))],
        out_specs=pl.BlockSpec((tm, tn), lambda i, j: (i, j)),
        out_shape=jax.ShapeDtypeStruct((m, n), BF16),
        scratch_shapes=[pltpu.VMEM((tm, d), BF16)],
        compiler_params=_params(("parallel", "arbitrary")),
        name="lnmod_matmul",
    )(x2, mod3, mod3, w)


class _ModRow:
    def __init__(self, tm, seq=None, fixed=None):
        self.tm, self.seq, self.fixed = tm, seq, fixed

    def __call__(self, i):
        if self.fixed is not None:
            return self.fixed
        return (i * self.tm) // self.seq


def _dwconv_rows(x, w_ref):
    n = x.shape[0]
    k = w_ref.shape[0]
    row = lax.broadcasted_iota(jnp.int32, x.shape, 0)
    acc = x * w_ref[k // 2:k // 2 + 1, :]
    for t in range(k):
        off = t - k // 2
        if off == 0:
            continue
        shifted = pltpu.roll(x, (-off) % n, 0)
        valid = (row + off >= 0) & (row + off < n)
        acc = acc + jnp.where(valid, shifted, 0.0) * w_ref[t:t + 1, :]
    return acc


def _conv_silu_kernel(x_ref, w_ref, b_ref, o_ref):
    y = _dwconv_rows(x_ref[...].astype(F32), w_ref) + b_ref[...]
    o_ref[...] = _silu(y).astype(o_ref.dtype)


def _conv_silu(src3, col0, width, conv_w, conv_b):
    b, n, _ = src3.shape
    tc = LANE
    c0 = col0 // tc
    kk = conv_w.shape[0]
    return pl.pallas_call(
        _conv_silu_kernel,
        grid=(b, width // tc),
        in_specs=[pl.BlockSpec((None, n, tc), lambda bi, j: (bi, 0, c0 + j)),
                  pl.BlockSpec((kk, tc), lambda bi, j: (0, j)),
                  pl.BlockSpec((1, tc), lambda bi, j: (0, j))],
        out_specs=pl.BlockSpec((None, n, tc), lambda bi, j: (bi, 0, j)),
        out_shape=jax.ShapeDtypeStruct((b, n, width), BF16),
        compiler_params=_params(("parallel", "parallel")),
        name="conv_silu",
    )(src3, conv_w, conv_b.reshape(1, width))


def _softplus(v):
    return jnp.maximum(v, 0.0) + jnp.log(1.0 + jnp.exp(-jnp.abs(v)))


def _ssd_direction(u_ref, dt_ref, a_row, dtb_row, s_ref, y_ref, d, reverse):
    q = SSD_CHUNK
    dt_all = _softplus(dt_ref[...].astype(F32) + dtb_row)
    cum = dt_all * a_row
    row = lax.broadcasted_iota(jnp.int32, (q, LANE), 0)
    s = 1
    while s < q:
        if reverse:
            cum = cum + jnp.where(row < q - s, pltpu.roll(cum, q - s, 0), 0.0)
        else:
            cum = cum + jnp.where(row >= s, pltpu.roll(cum, s, 0), 0.0)
        s *= 2
    tot = cum[0:1, :] if reverse else cum[q - 1:q, :]
    cum_t = cum.T
    dt_t = dt_all.T
    e_cum = jnp.exp(cum)
    w_end = jnp.exp(tot - cum) * dt_all
    e_tot = jnp.exp(tot)

    ri = lax.broadcasted_iota(jnp.int32, (q, q), 0)
    ci = lax.broadcasted_iota(jnp.int32, (q, q), 1)
    tri = (ri <= ci) if reverse else (ri >= ci)
    lane = lax.broadcasted_iota(jnp.int32, (q, LANE), 1)
    lo = lane < HEAD_DIM
    lane_row = lax.broadcasted_iota(jnp.int32, (1, LANE), 1)

    def pair_cols(v, c0, c1):
        return jnp.where(lo, jnp.broadcast_to(v[:, c0:c0 + 1], (q, LANE)),
                         jnp.broadcast_to(v[:, c1:c1 + 1], (q, LANE)))

    heads_per_group = SSD_HEADS // SSD_GROUPS
    for g in range(SSD_GROUPS):
        b0 = SSD_D_INNER + g * SSD_STATE
        c0 = SSD_D_INNER + SSD_GROUPS * SSD_STATE + g * SSD_STATE
        bg = u_ref[:, b0:b0 + SSD_STATE]
        cg = u_ref[:, c0:c0 + SSD_STATE]
        cb = lax.dot_general(cg, bg, (((1,), (1,)), ((), ())), preferred_element_type=F32)
        bg_t = bg.astype(F32).T.astype(BF16)
        for hp in range(heads_per_group // 2):
            h0 = g * heads_per_group + 2 * hp
            col0, col1 = d * SSD_HEADS + h0, d * SSD_HEADS + h0 + 1
            l0 = h0 * HEAD_DIM
            xp = u_ref[:, l0:l0 + LANE]
            ws = []
            for col in (col0, col1):
                seg = cum[:, col:col + 1] - cum_t[col:col + 1, :]
                dec = jnp.exp(jnp.where(tri, seg, NEG))
                ws.append((dec * cb * dt_t[col:col + 1, :]).astype(BF16))
            w_pair = jnp.concatenate(ws, axis=1)
            zero = jnp.zeros_like(xp)
            x_bd = jnp.concatenate([jnp.where(lo, xp, zero), jnp.where(lo, zero, xp)], axis=0)
            y_diag = jnp.dot(w_pair, x_bd, preferred_element_type=F32)
            st = s_ref[d, :, l0:l0 + LANE]
            y_off = jnp.dot(cg, st.astype(BF16), preferred_element_type=F32) * pair_cols(e_cum, col0, col1)
            y_ref[:, l0:l0 + LANE] = y_diag + y_off
            xw = (xp.astype(F32) * pair_cols(w_end, col0, col1)).astype(BF16)
            upd = jnp.dot(bg_t, xw, preferred_element_type=F32)
            tot_pair = jnp.where(lane_row < HEAD_DIM,
                                 jnp.broadcast_to(e_tot[:, col0:col0 + 1], (1, LANE)),
                                 jnp.broadcast_to(e_tot[:, col1:col1 + 1], (1, LANE)))
            s_ref[d, :, l0:l0 + LANE] = st * tot_pair + upd


def _ssd_kernel(uf_ref, ub_ref, dtf_ref, dtb_ref, alog_ref, dtbias_ref, h0_ref,
                yf_ref, yb_ref, hT_ref, s_ref):
    t = pl.program_id(1)

    @pl.when(t == 0)
    def _():
        s_ref[...] = h0_ref[...]

    a_row = -jnp.exp(alog_ref[...])
    dtb_row = dtbias_ref[...]
    _ssd_direction(uf_ref, dtf_ref, a_row, dtb_row, s_ref, yf_ref, 0, False)
    _ssd_direction(ub_ref, dtb_ref, a_row, dtb_row, s_ref, yb_ref, 1, True)

    @pl.when(t == pl.num_programs(1) - 1)
    def _():
        hT_ref[...] = s_ref[...]


def _ssd_scan(u3, p3, dt_col0, alog_row, dtbias_row, h0):
    b, n, _ = u3.shape
    q = SSD_CHUNK
    nt = n // q
    dtb = dt_col0 // LANE
    hp = SSD_D_INNER
    return pl.pallas_call(
        _ssd_kernel,
        grid=(b, nt),
        in_specs=[pl.BlockSpec((None, q, SSD_XBC), lambda bi, t: (bi, t, 0)),
                  pl.BlockSpec((None, q, SSD_XBC), lambda bi, t: (bi, nt - 1 - t, 0)),
                  pl.BlockSpec((None, q, LANE), lambda bi, t: (bi, t, dtb)),
                  pl.BlockSpec((None, q, LANE), lambda bi, t: (bi, nt - 1 - t, dtb)),
                  pl.BlockSpec((1, LANE), lambda bi, t: (0, 0)),
                  pl.BlockSpec((1, LANE), lambda bi, t: (0, 0)),
                  pl.BlockSpec((None, 2, SSD_STATE, hp), lambda bi, t: (bi, 0, 0, 0))],
        out_specs=[pl.BlockSpec((None, q, hp), lambda bi, t: (bi, t, 0)),
                   pl.BlockSpec((None, q, hp), lambda bi, t: (bi, nt - 1 - t, 0)),
                   pl.BlockSpec((None, 2, SSD_STATE, hp), lambda bi, t: (bi, 0, 0, 0))],
        out_shape=[jax.ShapeDtypeStruct((b, n, hp), F32),
                   jax.ShapeDtypeStruct((b, n, hp), F32),
                   jax.ShapeDtypeStruct((b, 2, SSD_STATE, hp), F32)],
        scratch_shapes=[pltpu.VMEM((2, SSD_STATE, hp), F32)],
        compiler_params=_params(("parallel", "arbitrary")),
        name="ssd_scan",
    )(u3, u3, p3, p3, alog_row, dtbias_row, h0)


def _ssd_finish_kernel(yf_ref, yb_ref, x_ref, z_ref, dskip_ref, g_ref, o_ref):
    y = dskip_ref[...] * x_ref[...].astype(F32) + yf_ref[...] + yb_ref[...]
    y = y * _silu(z_ref[...].astype(F32))
    r = lax.rsqrt(jnp.mean(y * y, axis=-1, keepdims=True) + LN_EPS)
    o_ref[...] = (y * r * g_ref[...]).astype(o_ref.dtype)


def _ssd_finish(yf2, yb2, u2, p2, z_col0, dskip_row, g_row):
    m, w = yf2.shape
    tm = _pick(m, (1024, 512, 256, 128))
    zb = z_col0 // w
    row = lambda i: (i, 0)
    return pl.pallas_call(
        _ssd_finish_kernel,
        grid=(m // tm,),
        in_specs=[pl.BlockSpec((tm, w), row), pl.BlockSpec((tm, w), row),
                  pl.BlockSpec((tm, w), row),
                  pl.BlockSpec((tm, w), lambda i: (i, zb)),
                  pl.BlockSpec((1, w), lambda i: (0, 0)),
                  pl.BlockSpec((1, w), lambda i: (0, 0))],
        out_specs=pl.BlockSpec((tm, w), row),
        out_shape=jax.ShapeDtypeStruct((m, w), BF16),
        compiler_params=_params(("parallel",)),
        name="ssd_finish",
    )(yf2, yb2, u2, p2, dskip_row, g_row)


def _block_diag_rows(kv):
    lane = lax.broadcasted_iota(jnp.int32, kv.shape, 1)
    lo = lane < HEAD_DIM
    zero = jnp.zeros_like(kv)
    return jnp.concatenate([jnp.where(lo, kv, zero), jnp.where(lo, zero, kv)], axis=0)


def _dup_group(kv, g):
    lane = lax.broadcasted_iota(jnp.int32, kv.shape, 1)
    rolled = pltpu.roll(kv, HEAD_DIM, 1)
    return jnp.where(lane // HEAD_DIM == g, kv, rolled)


def _qk(q, kbd):
    return lax.dot_general(q, kbd, (((1,), (1,)), ((), ())), preferred_element_type=F32)


def _pair_softmax_pv(s_w, s_c, bias, vbd, vcbd, sink, out_dtype):
    nk = s_w.shape[1] // 2
    nc = s_c.shape[1] // 2
    tq = s_w.shape[0]
    pws, pcs, invs = [], [], []
    for h in range(2):
        sw = s_w[:, h * nk:(h + 1) * nk] + bias[h]
        sc = s_c[:, h * nc:(h + 1) * nc]
        m = jnp.maximum(jnp.max(sw, axis=-1, keepdims=True), jnp.max(sc, axis=-1, keepdims=True))
        if sink is not None:
            m = jnp.maximum(m, sink[h])
        pw = jnp.exp(sw - m)
        pc = jnp.exp(sc - m)
        l = jnp.sum(pw, axis=-1, keepdims=True) + jnp.sum(pc, axis=-1, keepdims=True)
        if sink is not None:
            l = l + jnp.exp(sink[h] - m)
        pws.append(pw.astype(BF16))
        pcs.append(pc.astype(BF16))
        invs.append(1.0 / l)
    o = jnp.dot(jnp.concatenate(pws, axis=1), vbd, preferred_element_type=F32)
    o = o + jnp.dot(jnp.concatenate(pcs, axis=1), vcbd, preferred_element_type=F32)
    lane = lax.broadcasted_iota(jnp.int32, (tq, LANE), 1)
    inv = jnp.where(lane < HEAD_DIM, jnp.broadcast_to(invs[0], (tq, LANE)),
                    jnp.broadcast_to(invs[1], (tq, LANE)))
    return (o * inv).astype(out_dtype)


def _na_kernel(q_ref, k_ref, v_ref, kc_ref, vc_ref, bias_ref, o_ref, *, rows):
    qb = pl.program_id(2)
    nk = NA_K_ROWS * GRID_W
    ks = jnp.clip(NA_Q_ROWS * qb - (NA_K_ROWS - NA_Q_ROWS) // 2, 0, rows - NA_K_ROWS) * GRID_W
    ks = pl.multiple_of(ks, GRID_W)
    for p in range(q_ref.shape[1] // LANE):
        cs = slice(p * LANE, (p + 1) * LANE)
        q = q_ref[:, cs] * (HEAD_DIM ** -0.5)
        kbd = _block_diag_rows(k_ref[pl.ds(ks, nk), cs])
        vbd = _block_diag_rows(v_ref[pl.ds(ks, nk), cs])
        kcbd = _block_diag_rows(kc_ref[:, cs])
        vcbd = _block_diag_rows(vc_ref[:, cs])
        s_w = _qk(q, kbd)
        s_c = _qk(q, kcbd)
        o_ref[:, cs] = _pair_softmax_pv(s_w, s_c, [bias_ref[2 * p], bias_ref[2 * p + 1]], vbd, vcbd,
                                        None, o_ref.dtype)


def _na_key_start(qb, rows):
    return min(max(NA_Q_ROWS * qb - (NA_K_ROWS - NA_Q_ROWS) // 2, 0), rows - NA_K_ROWS)


def _na_bias_tables(rpb, rows):
    nqb = rows // NA_Q_ROWS
    variants = [0, 1 if nqb > 2 else 0, nqb - 1]
    nh = rpb.shape[0]
    w = GRID_W
    pad = w - NA_WIN_COLS
    rp = jnp.pad(rpb.astype(F32), ((0, 0), (0, 0), (pad, pad)))
    toep = jnp.stack([rp[:, :, w - 1 - qc:2 * w - 1 - qc] for qc in range(w)], axis=2)
    qc, kc = np.arange(w)[:, None], np.arange(w)[None, :]
    ws = np.clip(qc - NA_WIN_COLS // 2, 0, w - NA_WIN_COLS)
    toep = jnp.where(jnp.asarray((kc >= ws) & (kc < ws + NA_WIN_COLS)), toep, NEG)
    masked = jnp.full((nh, w, w), NEG, F32)
    tabs = []
    for qb in variants:
        block_rows = []
        for qr_l in range(NA_Q_ROWS):
            qr = NA_Q_ROWS * qb + qr_l
            rs = min(max(qr - NA_WIN_ROWS // 2, 0), rows - NA_WIN_ROWS)
            blocks = []
            for kr_l in range(NA_K_ROWS):
                kr = _na_key_start(qb, rows) + kr_l
                inside = rs <= kr < rs + NA_WIN_ROWS
                blocks.append(toep[:, kr - qr + NA_WIN_ROWS - 1] if inside else masked)
            block_rows.append(jnp.concatenate(blocks, axis=-1))
        tabs.append(jnp.concatenate(block_rows, axis=-2))
    return jnp.stack(tabs, axis=0)


def _na_attention(p3, pc3, q_col0, k_col0, v_col0, bias_tab):
    b, n, _ = p3.shape
    nc = pc3.shape[1]
    rows = n // GRID_W
    tq = NA_Q_ROWS * GRID_W
    nk = NA_K_ROWS * GRID_W
    nqb = n // tq
    wb = 4 * LANE
    qc, kc, vc = q_col0 // wb, k_col0 // wb, v_col0 // wb
    ngrp = 1

    def variant(qb):
        return jnp.where(qb == 0, 0, jnp.where(qb == nqb - 1, 2, 1))

    return pl.pallas_call(
        functools.partial(_na_kernel, rows=rows),
        grid=(b, ngrp, nqb),
        in_specs=[pl.BlockSpec((None, tq, wb), lambda bi, h, i: (bi, i, qc + h)),
                  pl.BlockSpec((None, n, wb), lambda bi, h, i: (bi, 0, kc + h)),
                  pl.BlockSpec((None, n, wb), lambda bi, h, i: (bi, 0, vc + h)),
                  pl.BlockSpec((None, nc, wb), lambda bi, h, i: (bi, 0, kc + h)),
                  pl.BlockSpec((None, nc, wb), lambda bi, h, i: (bi, 0, vc + h)),
                  pl.BlockSpec((None, wb // HEAD_DIM, tq, nk), lambda bi, h, i: (variant(i), h, 0, 0))],
        out_specs=pl.BlockSpec((None, tq, wb), lambda bi, h, i: (bi, i, h)),
        out_shape=jax.ShapeDtypeStruct((b, n, ngrp * wb), BF16),
        compiler_params=_params(("parallel", "parallel", "arbitrary")),
        name="na_attention",
    )(p3, p3, p3, pc3, pc3, bias_tab)


def _rope(x, cos, sin_signed):
    lane = lax.broadcasted_iota(jnp.int32, x.shape, 1)
    first = (lane % 32) < 16
    partner = jnp.where(first, pltpu.roll(x, LANE - 16, 1), pltpu.roll(x, 16, 1))
    return x * cos + partner * sin_signed


def _sink_pair(sink_ref):
    s = sink_ref[...]
    return [s[:, 0:1], s[:, HEAD_DIM:HEAD_DIM + 1]]


def _swa_kernel(q_ref, k_ref, v_ref, kc_ref, vc_ref, cos_ref, sin_ref, sink_ref, o_ref, *, n):
    g = pl.program_id(1)
    qb = pl.program_id(2)
    q0 = pl.multiple_of(qb * SWA_Q, SWA_Q)
    ks = pl.multiple_of(jnp.clip(qb * SWA_Q - SWA_WINDOW, 0, n - SWA_K), SWA_WINDOW)
    cos_q, sin_q = cos_ref[pl.ds(q0, SWA_Q), :], sin_ref[pl.ds(q0, SWA_Q), :]
    kk = _rope(k_ref[pl.ds(ks, SWA_K), :].astype(F32), cos_ref[pl.ds(ks, SWA_K), :],
               sin_ref[pl.ds(ks, SWA_K), :])
    kbd = _block_diag_rows(_dup_group(kk, g).astype(BF16))
    vbd = _block_diag_rows(_dup_group(v_ref[pl.ds(ks, SWA_K), :].astype(F32), g).astype(BF16))
    kcbd = _block_diag_rows(_dup_group(kc_ref[...].astype(F32), g).astype(BF16))
    vcbd = _block_diag_rows(_dup_group(vc_ref[...].astype(F32), g).astype(BF16))
    qpos = q0 + lax.broadcasted_iota(jnp.int32, (SWA_Q, SWA_K), 0)
    kpos = ks + lax.broadcasted_iota(jnp.int32, (SWA_Q, SWA_K), 1)
    bias = jnp.where(jnp.abs(kpos - qpos) <= SWA_WINDOW, 0.0, NEG)
    for p in range(q_ref.shape[1] // LANE):
        cs = slice(p * LANE, (p + 1) * LANE)
        q = _rope(q_ref[:, cs].astype(F32), cos_q, sin_q)
        q = (q * (HEAD_DIM ** -0.5)).astype(BF16)
        s_w = _qk(q, kbd)
        s_c = _qk(q, kcbd)
        o_ref[:, cs] = _pair_softmax_pv(s_w, s_c, [bias, bias], vbd, vcbd, _sink_pair(sink_ref.at[p]),
                                        o_ref.dtype)


def _swa_attention(p3, pc3, q_col0, kv_col0, cos_tab, sin_tab, sink_rows):
    b, n, _ = p3.shape
    nc = pc3.shape[1]
    nqb = n // SWA_Q
    wb = 2 * LANE
    qc, kc = q_col0 // wb, kv_col0 // LANE
    ngrp = 2
    return pl.pallas_call(
        functools.partial(_swa_kernel, n=n),
        grid=(b, ngrp, nqb),
        in_specs=[pl.BlockSpec((None, SWA_Q, wb), lambda bi, h, i: (bi, i, qc + h)),
                  pl.BlockSpec((None, n, LANE), lambda bi, h, i: (bi, 0, kc)),
                  pl.BlockSpec((None, n, LANE), lambda bi, h, i: (bi, 0, kc + 1)),
                  pl.BlockSpec((None, nc, LANE), lambda bi, h, i: (bi, 0, kc)),
                  pl.BlockSpec((None, nc, LANE), lambda bi, h, i: (bi, 0, kc + 1)),
                  pl.BlockSpec((n, LANE), lambda bi, h, i: (0, 0)),
                  pl.BlockSpec((n, LANE), lambda bi, h, i: (0, 0)),
                  pl.BlockSpec((2, 1, LANE), lambda bi, h, i: (h, 0, 0))],
        out_specs=pl.BlockSpec((None, SWA_Q, wb), lambda bi, h, i: (bi, i, h)),
        out_shape=jax.ShapeDtypeStruct((b, n, ngrp * wb), BF16),
        compiler_params=_params(("parallel", "parallel", "arbitrary")),
        name="swa_attention",
    )(p3, p3, p3, pc3, pc3, cos_tab, sin_tab, sink_rows)


def _ctx_attn_kernel(q_ref, k_ref, v_ref, sink_ref, o_ref, *, grouped):
    hp = pl.program_id(1)
    q = q_ref[...] * (HEAD_DIM ** -0.5)
    k, v = k_ref[...], v_ref[...]
    if grouped:
        g = hp // 2
        k = _dup_group(k.astype(F32), g).astype(BF16)
        v = _dup_group(v.astype(F32), g).astype(BF16)
    kbd, vbd = _block_diag_rows(k), _block_diag_rows(v)
    s = _qk(q, kbd)
    t = q.shape[0]
    sink = _sink_pair(sink_ref) if grouped else None
    ps, invs = [], []
    for h in range(2):
        sh = s[:, h * t:(h + 1) * t]
        m = jnp.max(sh, axis=-1, keepdims=True)
        if sink is not None:
            m = jnp.maximum(m, sink[h])
        p = jnp.exp(sh - m)
        l = jnp.sum(p, axis=-1, keepdims=True)
        if sink is not None:
            l = l + jnp.exp(sink[h] - m)
        ps.append(p.astype(BF16))
        invs.append(1.0 / l)
    o = jnp.dot(jnp.concatenate(ps, axis=1), vbd, preferred_element_type=F32)
    lane = lax.broadcasted_iota(jnp.int32, (t, LANE), 1)
    inv = jnp.where(lane < HEAD_DIM, jnp.broadcast_to(invs[0], (t, LANE)),
                    jnp.broadcast_to(invs[1], (t, LANE)))
    o_ref[...] = (o * inv).astype(o_ref.dtype)


def _ctx_attention(pc3, q_col0, k_col0, v_col0, sink_rows, grouped):
    b, nc, _ = pc3.shape
    qc, kc, vc = q_col0 // LANE, k_col0 // LANE, v_col0 // LANE
    npair = 4
    kv_blk = (lambda h: 0) if grouped else (lambda h: h)
    return pl.pallas_call(
        functools.partial(_ctx_attn_kernel, grouped=grouped),
        grid=(b, npair),
        in_specs=[pl.BlockSpec((None, nc, LANE), lambda bi, h: (bi, 0, qc + h)),
                  pl.BlockSpec((None, nc, LANE), lambda bi, h: (bi, 0, kc + kv_blk(h))),
                  pl.BlockSpec((None, nc, LANE), lambda bi, h: (bi, 0, vc + kv_blk(h))),
                  pl.BlockSpec((None, 1, LANE), lambda bi, h: (h, 0, 0))],
        out_specs=pl.BlockSpec((None, nc, LANE), lambda bi, h: (bi, 0, h)),
        out_shape=jax.ShapeDtypeStruct((b, nc, npair * LANE), BF16),
        compiler_params=_params(("parallel", "parallel")),
        name="ctx_attention",
    )(pc3, pc3, pc3, sink_rows)


def _fn_channel_kernel(x_ref, cs_ref, o_ref):
    w = x_ref.shape[1]
    ngroups = w // FN_GROUP_DIM
    for g in range(ngroups):
        xg = x_ref[:, g * FN_GROUP_DIM:(g + 1) * FN_GROUP_DIM]
        ab = jnp.dot(xg, cs_ref[...], preferred_element_type=F32)
        o_ref[:, g * FN_GROUP_DIM:(g + 1) * FN_GROUP_DIM] = ab[:, :FN_GROUP_DIM].astype(o_ref.dtype)
        o_ref[:, w + g * FN_GROUP_DIM:w + (g + 1) * FN_GROUP_DIM] = ab[:, FN_GROUP_DIM:].astype(o_ref.dtype)


def _fn_channel(p2, col0, width):
    m = p2.shape[0]
    tm = _pick(m, (1024, 512, 256, 128))
    k = np.arange(FN_GROUP_DIM)
    ang = 2.0 * np.pi * ((k[:, None] * k[None, :]) % FN_GROUP_DIM) / FN_GROUP_DIM
    cs = jnp.asarray(np.concatenate([np.cos(ang), np.sin(ang)], axis=1), BF16)
    cb = col0 // width
    return pl.pallas_call(
        _fn_channel_kernel,
        grid=(m // tm,),
        in_specs=[pl.BlockSpec((tm, width), lambda i: (i, cb)),
                  pl.BlockSpec((FN_GROUP_DIM, 2 * FN_GROUP_DIM), lambda i: (0, 0))],
        out_specs=pl.BlockSpec((tm, 2 * width), lambda i: (i, 0)),
        out_shape=jax.ShapeDtypeStruct((m, 2 * width), BF16),
        compiler_params=_params(("parallel",)),
        name="fn_channel",
    )(p2, cs)


def _fn_position_kernel(ac_ref, as_ref, bc_ref, bs_ref, ab_ref, o_ref, c_ref, s_ref, *, scale):
    @pl.when(pl.program_id(1) == 0)
    def _():
        for j in range(ac_ref.shape[0]):
            ca, sa = ac_ref[j:j + 1, :], as_ref[j:j + 1, :]
            cb, sb = bc_ref[...], bs_ref[...]
            c_ref[j * DFT_ROWS:(j + 1) * DFT_ROWS, :] = (ca * cb - sa * sb).astype(BF16)
            s_ref[j * DFT_ROWS:(j + 1) * DFT_ROWS, :] = (sa * cb + ca * sb).astype(BF16)

    w = o_ref.shape[1]
    y = jnp.dot(c_ref[...], ab_ref[:, :w], preferred_element_type=F32)
    y = y - jnp.dot(s_ref[...], ab_ref[:, w:], preferred_element_type=F32)
    o_ref[...] = (y * scale).astype(o_ref.dtype)


def _fn_position(ab3):
    b, n, w2 = ab3.shape
    w = w2 // 2
    tm = _pick(n, (512, 256, 128, 64))
    jc = tm // DFT_ROWS
    n1 = n // DFT_ROWS
    k = np.arange(n, dtype=np.int64)
    j1 = np.arange(n1, dtype=np.int64)
    j2 = np.arange(DFT_ROWS, dtype=np.int64)
    ang_a = 2.0 * np.pi * ((j1[:, None] * DFT_ROWS * k[None, :]) % n) / n
    ang_b = 2.0 * np.pi * ((j2[:, None] * k[None, :]) % n) / n
    ac, as_ = jnp.asarray(np.cos(ang_a), F32), jnp.asarray(np.sin(ang_a), F32)
    bc, bs = jnp.asarray(np.cos(ang_b), F32), jnp.asarray(np.sin(ang_b), F32)
    scale = 1.0 / math.sqrt(n * FN_GROUP_DIM)
    if n1 % 8 != 0 and jc != n1:
        raise ValueError("unsupported sequence length for the position DFT tiling")
    return pl.pallas_call(
        functools.partial(_fn_position_kernel, scale=scale),
        grid=(n // tm, b),
        in_specs=[pl.BlockSpec((jc, n), lambda i, bi: (i, 0)),
                  pl.BlockSpec((jc, n), lambda i, bi: (i, 0)),
                  pl.BlockSpec((DFT_ROWS, n), lambda i, bi: (0, 0)),
                  pl.BlockSpec((DFT_ROWS, n), lambda i, bi: (0, 0)),
                  pl.BlockSpec((None, n, w2), lambda i, bi: (bi, 0, 0))],
        out_specs=pl.BlockSpec((None, tm, w), lambda i, bi: (bi, i, 0)),
        out_shape=jax.ShapeDtypeStruct((b, n, w), BF16),
        scratch_shapes=[pltpu.VMEM((tm, n), BF16), pltpu.VMEM((tm, n), BF16)],
        compiler_params=_params(("parallel", "arbitrary")),
        name="fn_position",
    )(ac, as_, bc, bs, ab3)


def _residual_ln(o_ref, x_ref, gate_ref, lng_ref, lnb_ref, alpha):
    v = alpha * x_ref[...] + gate_ref[...] * o_ref[...]
    o_ref[...] = _layer_norm(v) * lng_ref[...] + lnb_ref[...]


def _merge_out_kernel(y0, y1, y2, y3, g0, g1, g2, g3, wb_ref, bg_ref, wo_ref, x_ref, gate_ref, lng_ref,
                      lnb_ref, o_ref, *, alpha):
    j = pl.program_id(1)

    @pl.when(j == 0)
    def _():
        o_ref[...] = jnp.zeros_like(o_ref)

    merged = None
    for i, (y, g) in enumerate(((y0, g0), (y1, g1), (y2, g2), (y3, g3))):
        gate = _sigmoid(g[...].astype(F32) + bg_ref[i])
        term = gate * jnp.dot(y[...], wb_ref[i], preferred_element_type=F32)
        merged = term if merged is None else merged + term
    o_ref[...] += jnp.dot(merged.astype(BF16), wo_ref[...], preferred_element_type=F32)

    @pl.when(j == pl.num_programs(1) - 1)
    def _():
        _residual_ln(o_ref, x_ref, gate_ref, lng_ref, lnb_ref, alpha)


def _merge_out(ys, p2, gate_col0, wb, bg3, w_out, x2, mod3, mod_row, gate_chunk, ln_g, ln_b, alpha):
    m = p2.shape[0]
    d = wb.shape[2]
    tm = mod_row.tm
    tn = _pick(d, (1024, 512, 256, 128))
    gb = gate_col0 // tn
    per = d // tn
    bw = ys[0].shape[1]
    y_specs = [pl.BlockSpec((tm, bw), lambda i, j: (i, 0)) for _ in range(N_BRANCH)]
    g_specs = [pl.BlockSpec((tm, tn), functools.partial(lambda i, j, br: (i, gb + br * per + j), br=br))
               for br in range(N_BRANCH)]
    return pl.pallas_call(
        functools.partial(_merge_out_kernel, alpha=alpha),
        grid=(m // tm, per),
        in_specs=y_specs + g_specs + [
            pl.BlockSpec((N_BRANCH, bw, tn), lambda i, j: (0, 0, j)),
            pl.BlockSpec((N_BRANCH, 1, tn), lambda i, j: (0, 0, j)),
            pl.BlockSpec((tn, d), lambda i, j: (j, 0)),
            pl.BlockSpec((tm, d), lambda i, j: (i, 0)),
            pl.BlockSpec((None, 1, d), lambda i, j: (mod_row(i), 0, gate_chunk)),
            pl.BlockSpec((1, d), lambda i, j: (0, 0)),
            pl.BlockSpec((1, d), lambda i, j: (0, 0))],
        out_specs=pl.BlockSpec((tm, d), lambda i, j: (i, 0)),
        out_shape=jax.ShapeDtypeStruct((m, d), F32),
        compiler_params=_params(("parallel", "arbitrary")),
        name="merge_out",
    )(*ys, p2, p2, p2, p2, wb, bg3, w_out, x2, mod3, ln_g.reshape(1, d), ln_b.reshape(1, d))


def _mm_resln_kernel(a_ref, w_ref, x_ref, gate_ref, lng_ref, lnb_ref, o_ref, *, alpha):
    k = pl.program_id(1)

    @pl.when(k == 0)
    def _():
        o_ref[...] = jnp.zeros_like(o_ref)

    o_ref[...] += jnp.dot(a_ref[...], w_ref[...], preferred_element_type=F32)

    @pl.when(k == pl.num_programs(1) - 1)
    def _():
        _residual_ln(o_ref, x_ref, gate_ref, lng_ref, lnb_ref, alpha)


def _mm_resln(a2, w, x2, mod3, mod_row, gate_chunk, ln_g, ln_b, alpha, tk):
    m, kdim = a2.shape
    d = w.shape[1]
    tm = mod_row.tm
    return pl.pallas_call(
        functools.partial(_mm_resln_kernel, alpha=alpha),
        grid=(m // tm, kdim // tk),
        in_specs=[pl.BlockSpec((tm, tk), lambda i, k: (i, k)),
                  pl.BlockSpec((tk, d), lambda i, k: (k, 0)),
                  pl.BlockSpec((tm, d), lambda i, k: (i, 0)),
                  pl.BlockSpec((None, 1, d), lambda i, k: (mod_row(i), 0, gate_chunk)),
                  pl.BlockSpec((1, d), lambda i, k: (0, 0)),
                  pl.BlockSpec((1, d), lambda i, k: (0, 0))],
        out_specs=pl.BlockSpec((tm, d), lambda i, k: (i, 0)),
        out_shape=jax.ShapeDtypeStruct((m, d), F32),
        compiler_params=_params(("parallel", "arbitrary")),
        name="matmul_residual_ln",
    )(a2, w, x2, mod3, ln_g.reshape(1, d), ln_b.reshape(1, d))


HALO_ROWS = 8


def _ffn_up_kernel(x_ref, xp_ref, xn_ref, sh_ref, sc_ref, wg_ref, wu_ref, cw_ref, cb_ref, o_ref, h_ref,
                   *, seq):
    i = pl.program_id(0)
    tm = x_ref.shape[0]

    @pl.when(pl.program_id(1) == 0)
    def _():
        scale, shift = 1.0 + sc_ref[...], sh_ref[...]
        h_ref[0:tm, :] = (_layer_norm(x_ref[...]) * scale + shift).astype(BF16)
        halo = jnp.concatenate([xp_ref[...], xn_ref[...]], axis=0)
        h_ref[tm:tm + 2 * HALO_ROWS, :] = (_layer_norm(halo) * scale + shift).astype(BF16)

    g_all = jnp.dot(h_ref[...], wg_ref[...], preferred_element_type=F32)
    u = jnp.dot(h_ref[0:tm, :], wu_ref[...], preferred_element_type=F32)
    g = g_all[0:tm, :]
    has_prev = jnp.where((i * tm) % seq == 0, 0.0, 1.0)
    has_next = jnp.where(((i + 1) * tm) % seq == 0, 0.0, 1.0)
    g_prev = g_all[tm + HALO_ROWS - 1:tm + HALO_ROWS, :] * has_prev
    g_next = g_all[tm + HALO_ROWS:tm + HALO_ROWS + 1, :] * has_next
    row = lax.broadcasted_iota(jnp.int32, g.shape, 0)
    above = jnp.where(row == 0, g_prev, pltpu.roll(g, 1, 0))
    below = jnp.where(row == tm - 1, g_next, pltpu.roll(g, tm - 1, 0))
    y = above * cw_ref[0:1, :] + g * cw_ref[1:2, :] + below * cw_ref[2:3, :] + cb_ref[...]
    o_ref[...] = (_silu(y) * u).astype(o_ref.dtype)


def _ffn_up(x2, mod3, mod_row, sh_chunk, sc_chunk, w_up, conv_w, conv_b, tn, seq):
    m, d = x2.shape
    dff = w_up.shape[1] // 2
    tm = mod_row.tm
    if conv_w.shape[0] != 3 or seq % tm != 0 or tm % HALO_ROWS != 0:
        raise ValueError("unsupported ConvFFN tiling")
    ub = dff // tn
    hb = tm // HALO_ROWS
    last_hb = m // HALO_ROWS - 1
    return pl.pallas_call(
        functools.partial(_ffn_up_kernel, seq=seq),
        grid=(m // tm, dff // tn),
        in_specs=[pl.BlockSpec((tm, d), lambda i, j: (i, 0)),
                  pl.BlockSpec((HALO_ROWS, d), lambda i, j: (jnp.maximum(i * hb - 1, 0), 0)),
                  pl.BlockSpec((HALO_ROWS, d), lambda i, j: (jnp.minimum((i + 1) * hb, last_hb), 0)),
                  pl.BlockSpec((None, 1, d), lambda i, j: (mod_row(i), 0, sh_chunk)),
                  pl.BlockSpec((None, 1, d), lambda i, j: (mod_row(i), 0, sc_chunk)),
                  pl.BlockSpec((d, tn), lambda i, j: (0, j)),
                  pl.BlockSpec((d, tn), lambda i, j: (0, ub + j)),
                  pl.BlockSpec((3, tn), lambda i, j: (0, j)),
                  pl.BlockSpec((1, tn), lambda i, j: (0, j))],
        out_specs=pl.BlockSpec((tm, tn), lambda i, j: (i, j)),
        out_shape=jax.ShapeDtypeStruct((m, dff), BF16),
        scratch_shapes=[pltpu.VMEM((tm + 2 * HALO_ROWS, d), BF16)],
        compiler_params=_params(("parallel", "arbitrary")),
        name="ffn_up",
    )(x2, x2, x2, mod3, mod3, w_up, w_up, conv_w, conv_b.reshape(1, dff))


def _rope_tables(n):
    t = jnp.arange(n)
    rows = (t // GRID_W).astype(F32)
    cols = (t % GRID_W).astype(F32)
    n_freq = HEAD_DIM // 4
    inv = ROPE_BASE ** (-jnp.arange(n_freq, dtype=F32) / n_freq)
    ar, ac = rows[:, None] * inv, cols[:, None] * inv
    cos = jnp.concatenate([jnp.cos(ar), jnp.cos(ar), jnp.cos(ac), jnp.cos(ac)], axis=1)
    sin = jnp.concatenate([-jnp.sin(ar), jnp.sin(ar), -jnp.sin(ac), jnp.sin(ac)], axis=1)
    return jnp.tile(cos, (1, 2)), jnp.tile(sin, (1, 2))


def _branch_params(conv_w, conv_b, a_log, dt_bias, d_skip, norm_g, rpb, sink, rows, cos_tab, sin_tab):
    pad = jnp.zeros((1, LANE - 2 * SSD_HEADS), F32)
    return dict(
        conv_w=conv_w, conv_b=conv_b,
        alog_row=jnp.concatenate([a_log.reshape(1, -1), pad], axis=1),
        dtb_row=jnp.concatenate([dt_bias.reshape(1, -1), pad], axis=1),
        dskip_row=jnp.repeat(d_skip, HEAD_DIM).reshape(1, SSD_D_INNER),
        g_row=norm_g.reshape(1, SSD_D_INNER),
        sink_rows=jnp.repeat(sink, HEAD_DIM).reshape(4, 1, LANE),
        bias_tab=_na_bias_tables(rpb, rows),
        cos_tab=cos_tab, sin_tab=sin_tab)


def _mixer_branches(p3, pc3, cols, prm, ctx_out):
    bsz, n, npad = p3.shape
    nctx = pc3.shape[1]
    p2 = p3.reshape(bsz * n, npad)
    pc2 = pc3.reshape(bsz * nctx, npad)
    flat = lambda t: t.reshape(-1, t.shape[-1])

    u3 = _conv_silu(p3, cols["xbc"], SSD_XBC, prm["conv_w"], prm["conv_b"])
    uc3 = _conv_silu(pc3, cols["xbc"], SSD_XBC, prm["conv_w"], prm["conv_b"])
    h_zero = jnp.zeros((bsz, 2, SSD_STATE, SSD_D_INNER), F32)
    yfc, ybc, h_ctx = _ssd_scan(uc3, pc3, cols["dt"], prm["alog_row"], prm["dtb_row"], h_zero)
    yf, yb, _ = _ssd_scan(u3, p3, cols["dt"], prm["alog_row"], prm["dtb_row"], h_ctx)
    y_ssd = _ssd_finish(flat(yf), flat(yb), flat(u3), p2, cols["z"], prm["dskip_row"], prm["g_row"])
    y_na = flat(_na_attention(p3, pc3, cols["nq"], cols["nk"], cols["nv"], prm["bias_tab"]))
    y_swa = flat(_swa_attention(p3, pc3, cols["sq"], cols["skv"], prm["cos_tab"], prm["sin_tab"],
                                prm["sink_rows"]))
    y_fn = flat(_fn_position(_fn_channel(p2, cols["fn"], 512).reshape(bsz, n, -1)))
    ys = [y_ssd, y_na, y_swa, y_fn]
    if not ctx_out:
        return ys, None
    yc_ssd = _ssd_finish(flat(yfc), flat(ybc), flat(uc3), pc2, cols["z"], prm["dskip_row"], prm["g_row"])
    yc_na = flat(_ctx_attention(pc3, cols["nq"], cols["nk"], cols["nv"], prm["sink_rows"], False))
    yc_swa = flat(_ctx_attention(pc3, cols["sq"], cols["skv"], cols["skv"] + LANE, prm["sink_rows"], True))
    yc_fn = flat(_fn_position(_fn_channel(pc2, cols["fn"], 512).reshape(bsz, nctx, -1)))
    return ys, [yc_ssd, yc_na, yc_swa, yc_fn]


def kernel(x, c, ctx, c_ctx, w_ada, b_ada, w_in, b_gate, ssd_conv_w, ssd_conv_b, ssd_a_log,
           ssd_dt_bias, ssd_d, ssd_norm_g, na_rpb, swa_sink, w_branch, w_out, ln1_g, ln1_b,
           ln2_g, ln2_b, ffn_w_up, ffn_conv_w, ffn_conv_b, ffn_w_down):
    bsz, n, d = x.shape
    nctx = ctx.shape[1]
    depth = w_ada.shape[0]
    dff = ffn_w_down.shape[1]
    alpha = (2.0 * depth) ** 0.25
    rows = n // GRID_W

    g_col = 0
    z_col = N_BRANCH * d
    fn_col = z_col + 512
    sq_col = fn_col + 512
    nq_col = sq_col + 512
    nk_col = nq_col + 512
    nv_col = nk_col + 512
    xbc_col = nv_col + 512
    skv_col = xbc_col + SSD_XBC
    dt_col = skv_col + 256
    n_used = dt_col + LANE
    n_in = -(-n_used // 256) * 256
    tn_in = _pick(n_in, (1280, 1024, 768, 512, 256))
    cols = dict(z=z_col, fn=fn_col, sq=sq_col, nq=nq_col, nk=nk_col, nv=nv_col, xbc=xbc_col,
                skv=skv_col, dt=dt_col)

    o_z, o_xbc, o_dt = 0, 512, 512 + SSD_XBC
    o_na = o_dt + 2 * SSD_HEADS
    o_sq = o_na + 3 * 512
    o_skv = o_sq + 512
    o_fn = o_skv + 256
    o_g = o_fn + 512

    def stage_w_in(w):
        parts = [w[:, o_g:o_g + N_BRANCH * d], w[:, o_z:o_z + 512], w[:, o_fn:o_fn + 512],
                 w[:, o_sq:o_sq + 512], w[:, o_na:o_na + 3 * 512], w[:, o_xbc:o_xbc + SSD_XBC],
                 w[:, o_skv:o_skv + 256], w[:, o_dt:o_dt + 2 * SSD_HEADS],
                 jnp.zeros((d, n_in - n_used + LANE - 2 * SSD_HEADS), w.dtype)]
        return jnp.concatenate(parts, axis=1).astype(BF16)

    n_rows = -(-(bsz + 1) // 8) * 8
    cvec = jnp.concatenate([c, c_ctx[None], jnp.zeros((n_rows - bsz - 1, d), F32)], axis=0)
    mods = _ada_mods(cvec, w_ada, b_ada)

    lat_row = _ModRow(_pick(n, (1024, 512, 256)), seq=n)
    ctx_row = _ModRow(_pick(bsz * nctx, (1024, 512, 256)), fixed=bsz)
    lat_row_h = _ModRow(_pick(n, (512, 256)), seq=n)
    ctx_row_h = _ModRow(_pick(bsz * nctx, (512, 256)), fixed=bsz)
    ctx_row_seq = _ModRow(_pick(nctx, (1024, 512, 256)), fixed=bsz)
    tn_up = _pick(dff, (512, 256, 128))
    tk_down = _pick(dff, (512, 256, 128))

    cos_tab, sin_tab = _rope_tables(n)

    x2 = x.reshape(bsz * n, d)
    xc2 = ctx.reshape(bsz * nctx, d)
    for l in range(depth):
        ctx_out = l < depth - 1
        mod3 = mods[l].reshape(n_rows, 1, 6 * d)
        w_in_p = stage_w_in(w_in[l])
        wb = _stage_weight(w_branch.reshape(depth, N_BRANCH * BRANCH_WIDTH, d), l)
        wb = wb.reshape(N_BRANCH, BRANCH_WIDTH, d)
        bg3 = b_gate[l].reshape(N_BRANCH, 1, d)
        w_out_b = _stage_weight(w_out, l)
        w_up_b = _stage_weight(ffn_w_up, l)
        w_down_b = _stage_weight(ffn_w_down, l)
        prm = _branch_params(ssd_conv_w[l], ssd_conv_b[l], ssd_a_log[l], ssd_dt_bias[l], ssd_d[l],
                             ssd_norm_g[l], na_rpb[l], swa_sink[l], rows, cos_tab, sin_tab)

        p2 = _lnmod_matmul(x2, mod3, lat_row, 0, 1, w_in_p, tn_in)
        pc2 = _lnmod_matmul(xc2, mod3, ctx_row, 0, 1, w_in_p, tn_in)
        p3 = p2.reshape(bsz, n, n_in)
        pc3 = pc2.reshape(bsz, nctx, n_in)

        ys, ycs = _mixer_branches(p3, pc3, cols, prm, ctx_out)

        x_mid = _merge_out(ys, p2, g_col, wb, bg3, w_out_b, x2, mod3, lat_row_h, 2, ln1_g[l], ln1_b[l],
                           alpha)
        act = _ffn_up(x_mid, mod3, lat_row, 3, 4, w_up_b, ffn_conv_w[l], ffn_conv_b[l], tn_up, n)
        x2 = _mm_resln(act, w_down_b, x_mid, mod3, lat_row, 5, ln2_g[l], ln2_b[l], alpha, tk_down)

        if ctx_out:
            xc_mid = _merge_out(ycs, pc2, g_col, wb, bg3, w_out_b, xc2, mod3, ctx_row_h, 2,
                                ln1_g[l], ln1_b[l], alpha)
            act_c = _ffn_up(xc_mid, mod3, ctx_row_seq, 3, 4, w_up_b, ffn_conv_w[l], ffn_conv_b[l],
                            tn_up, nctx)
            xc2 = _mm_resln(act_c, w_down_b, xc_mid, mod3, ctx_row, 5, ln2_g[l], ln2_b[l], alpha,
                            tk_down)

    return x2.reshape(bsz, n, d)
```

```python
import functools
import math

import numpy as np
import jax
import jax.numpy as jnp
from jax import lax
from jax.experimental import pallas as pl
from jax.experimental.pallas import tpu as pltpu

F32 = jnp.float32
BF16 = jnp.bfloat16

GRID_W = 64
HEAD_DIM = 64
SSD_HEADS = 8
SSD_D_INNER = 512
SSD_STATE = 128
SSD_GROUPS = 2
SSD_CHUNK = 128
SSD_XBC = 1024
NA_WIN_ROWS = 8
NA_WIN_COLS = 16
NA_Q_ROWS = 4
NA_K_ROWS = 12
SWA_WINDOW = 128
SWA_Q = 256
SWA_K = 512
FN_GROUP_DIM = 128
N_BRANCH = 4
BRANCH_WIDTH = 512
ROPE_BASE = 10000.0
LN_EPS = 1e-6
NEG = -1e30
DFT_ROWS = 64

LANE = 128
VMEM_LIMIT = 56 * 1024 * 1024


def _pick(dim, prefs):
    for p in prefs:
        if p <= dim and dim % p == 0:
            return p
    return dim


def _params(sem):
    return pltpu.CompilerParams(dimension_semantics=sem, vmem_limit_bytes=VMEM_LIMIT)


def _layer_norm(v):
    mu = jnp.mean(v, axis=-1, keepdims=True)
    vc = v - mu
    var = jnp.mean(vc * vc, axis=-1, keepdims=True)
    return vc * lax.rsqrt(var + LN_EPS)


def _sigmoid(v):
    return 0.5 * jnp.tanh(0.5 * v) + 0.5


def _silu(v):
    return v * _sigmoid(v)


def _cast_kernel(w_ref, o_ref):
    o_ref[...] = w_ref[...].astype(o_ref.dtype)


def _stage_weight(w_stack, l):
    _, r, c = w_stack.shape
    tr = _pick(r, (512, 256, 128))
    tc = _pick(c, (2048, 1408, 1024, 512, 256, 128))
    return pl.pallas_call(
        _cast_kernel,
        grid=(r // tr, c // tc),
        in_specs=[pl.BlockSpec((None, tr, tc), lambda i, j: (l, i, j))],
        out_specs=pl.BlockSpec((tr, tc), lambda i, j: (i, j)),
        out_shape=jax.ShapeDtypeStruct((r, c), BF16),
        compiler_params=_params(("parallel", "parallel")),
        name="stage_weight",
    )(w_stack)


def _permute_kernel(w_ref, o_ref, *, segments, pad):
    w = w_ref[...]
    parts = [w[:, a:a + n] for a, n in segments]
    if pad:
        parts.append(jnp.zeros((w.shape[0], pad), w.dtype))
    o_ref[...] = jnp.concatenate(parts, axis=1).astype(o_ref.dtype)


def _stage_permuted(w_stack, l, segments, n_out):
    _, r, c = w_stack.shape
    tr = _pick(r, (128, 64, 32, 16))
    pad = n_out - sum(n for _, n in segments)
    return pl.pallas_call(
        functools.partial(_permute_kernel, segments=tuple(segments), pad=pad),
        grid=(r // tr,),
        in_specs=[pl.BlockSpec((None, tr, c), lambda i: (l, i, 0))],
        out_specs=pl.BlockSpec((tr, n_out), lambda i: (i, 0)),
        out_shape=jax.ShapeDtypeStruct((r, n_out), BF16),
        compiler_params=_params(("parallel",)),
        name="stage_permuted",
    )(w_stack)


def _ada_kernel(c_ref, w_ref, b_ref, o_ref):
    s = _silu(c_ref[...])
    o_ref[...] = jnp.dot(s.astype(BF16), w_ref[...].astype(BF16),
                         preferred_element_type=F32) + b_ref[...]


def _ada_mods(cvec, w_ada, b_ada):
    depth, d, n6 = w_ada.shape
    r = cvec.shape[0]
    tn = _pick(n6, (1024, 512, 256, 128))
    return pl.pallas_call(
        _ada_kernel,
        grid=(depth, n6 // tn),
        in_specs=[pl.BlockSpec((r, d), lambda l, j: (0, 0)),
                  pl.BlockSpec((None, d, tn), lambda l, j: (l, 0, j)),
                  pl.BlockSpec((None, 1, tn), lambda l, j: (l, 0, j))],
        out_specs=pl.BlockSpec((None, r, tn), lambda l, j: (l, 0, j)),
        out_shape=jax.ShapeDtypeStruct((depth, r, n6), F32),
        compiler_params=_params(("parallel", "parallel")),
        name="ada_mods",
    )(cvec, w_ada, b_ada.reshape(depth, 1, n6))


def _lnmod_mm_kernel(x_ref, sh_ref, sc_ref, w_ref, o_ref, h_ref):
    @pl.when(pl.program_id(1) == 0)
    def _():
        y = _layer_norm(x_ref[...])
        h_ref[...] = (y * (1.0 + sc_ref[...]) + sh_ref[...]).astype(BF16)

    o_ref[...] = jnp.dot(h_ref[...], w_ref[...], preferred_element_type=F32).astype(o_ref.dtype)


def _lnmod_matmul(x2, mod3, mod_row, sh_chunk, sc_chunk, w, tn):
    m, d = x2.shape
    n = w.shape[1]
    tm = mod_row.tm
    return pl.pallas_call(
        _lnmod_mm_kernel,
        grid=(m // tm, n // tn),
        in_specs=[pl.BlockSpec((tm, d), lambda i, j: (i, 0)),
                  pl.BlockSpec((None, 1, d), lambda i, j: (mod_row(i), 0, sh_chunk)),
                  pl.BlockSpec((None, 1, d), lambda i, j: (mod_row(i), 0, sc_chunk)),
                  pl.BlockSpec((d, tn), lambda i, j: (0, j))],
        out_specs=pl.BlockSpec((tm, tn), lambda i, j: (i, j)),
        out_shape=jax.ShapeDtypeStruct((m, n), BF16),
        scratch_shapes=[pltpu.VMEM((tm, d), BF16)],
        compiler_params=_params(("parallel", "arbitrary")),
        name="lnmod_matmul",
    )(x2, mod3, mod3, w)


class _ModRow:
    def __init__(self, tm, seq=None, fixed=None):
        self.tm, self.seq, self.fixed = tm, seq, fixed

    def __call__(self, i):
        if self.fixed is not None:
            return self.fixed
        return (i * self.tm) // self.seq


def _dwconv_rows(x, w_ref):
    n = x.shape[0]
    k = w_ref.shape[0]
    row = lax.broadcasted_iota(jnp.int32, x.shape, 0)
    acc = x * w_ref[k // 2:k // 2 + 1, :]
    for t in range(k):
        off = t - k // 2
        if off == 0:
            continue
        shifted = pltpu.roll(x, (-off) % n, 0)
        valid = (row + off >= 0) & (row + off < n)
        acc = acc + jnp.where(valid, shifted, 0.0) * w_ref[t:t + 1, :]
    return acc


def _conv_silu_kernel(x_ref, w_ref, b_ref, o_ref):
    y = _dwconv_rows(x_ref[...].astype(F32), w_ref) + b_ref[...]
    o_ref[...] = _silu(y).astype(o_ref.dtype)


def _conv_silu(src3, col0, width, conv_w, conv_b):
    b, n, _ = src3.shape
    tc = LANE
    c0 = col0 // tc
    kk = conv_w.shape[0]
    return pl.pallas_call(
        _conv_silu_kernel,
        grid=(b, width // tc),
        in_specs=[pl.BlockSpec((None, n, tc), lambda bi, j: (bi, 0, c0 + j)),
                  pl.BlockSpec((kk, tc), lambda bi, j: (0, j)),
                  pl.BlockSpec((1, tc), lambda bi, j: (0, j))],
        out_specs=pl.BlockSpec((None, n, tc), lambda bi, j: (bi, 0, j)),
        out_shape=jax.ShapeDtypeStruct((b, n, width), BF16),
        compiler_params=_params(("parallel", "parallel")),
        name="conv_silu",
    )(src3, conv_w, conv_b.reshape(1, width))


def _softplus(v):
    return jnp.maximum(v, 0.0) + jnp.log(1.0 + jnp.exp(-jnp.abs(v)))


def _ssd_direction(u_ref, dt_ref, a_row, dtb_row, s_ref, y_ref, d, reverse):
    q = SSD_CHUNK
    dt_all = _softplus(dt_ref[...].astype(F32) + dtb_row)
    cum = dt_all * a_row
    row = lax.broadcasted_iota(jnp.int32, (q, LANE), 0)
    s = 1
    while s < q:
        if reverse:
            cum = cum + jnp.where(row < q - s, pltpu.roll(cum, q - s, 0), 0.0)
        else:
            cum = cum + jnp.where(row >= s, pltpu.roll(cum, s, 0), 0.0)
        s *= 2
    tot = cum[0:1, :] if reverse else cum[q - 1:q, :]
    cum_t = cum.T
    dt_t = dt_all.T
    e_cum = jnp.exp(cum)
    w_end = jnp.exp(tot - cum) * dt_all
    e_tot = jnp.exp(tot)

    ri = lax.broadcasted_iota(jnp.int32, (q, q), 0)
    ci = lax.broadcasted_iota(jnp.int32, (q, q), 1)
    tri = (ri <= ci) if reverse else (ri >= ci)
    lane = lax.broadcasted_iota(jnp.int32, (q, LANE), 1)
    lo = lane < HEAD_DIM
    lane_row = lax.broadcasted_iota(jnp.int32, (1, LANE), 1)

    def pair_cols(v, c0, c1):
        return jnp.where(lo, jnp.broadcast_to(v[:, c0:c0 + 1], (q, LANE)),
                         jnp.broadcast_to(v[:, c1:c1 + 1], (q, LANE)))

    heads_per_group = SSD_HEADS // SSD_GROUPS
    for g in range(SSD_GROUPS):
        b0 = SSD_D_INNER + g * SSD_STATE
        c0 = SSD_D_INNER + SSD_GROUPS * SSD_STATE + g * SSD_STATE
        bg = u_ref[:, b0:b0 + SSD_STATE]
        cg = u_ref[:, c0:c0 + SSD_STATE]
        cb = lax.dot_general(cg, bg, (((1,), (1,)), ((), ())), preferred_element_type=F32)
        bg_t = bg.astype(F32).T.astype(BF16)
        for hp in range(heads_per_group // 2):
            h0 = g * heads_per_group + 2 * hp
            col0, col1 = d * SSD_HEADS + h0, d * SSD_HEADS + h0 + 1
            l0 = h0 * HEAD_DIM
            xp = u_ref[:, l0:l0 + LANE]
            ws = []
            for col in (col0, col1):
                seg = cum[:, col:col + 1] - cum_t[col:col + 1, :]
                dec = jnp.exp(jnp.where(tri, seg, NEG))
                ws.append((dec * cb * dt_t[col:col + 1, :]).astype(BF16))
            w_pair = jnp.concatenate(ws, axis=1)
            zero = jnp.zeros_like(xp)
            x_bd = jnp.concatenate([jnp.where(lo, xp, zero), jnp.where(lo, zero, xp)], axis=0)
            y_diag = jnp.dot(w_pair, x_bd, preferred_element_type=F32)
            st = s_ref[d, :, l0:l0 + LANE]
            y_off = jnp.dot(cg, st.astype(BF16), preferred_element_type=F32) * pair_cols(e_cum, col0, col1)
            y_ref[:, l0:l0 + LANE] = y_diag + y_off
            xw = (xp.astype(F32) * pair_cols(w_end, col0, col1)).astype(BF16)
            upd = jnp.dot(bg_t, xw, preferred_element_type=F32)
            tot_pair = jnp.where(lane_row < HEAD_DIM,
                                 jnp.broadcast_to(e_tot[:, col0:col0 + 1], (1, LANE)),
                                 jnp.broadcast_to(e_tot[:, col1:col1 + 1], (1, LANE)))
            s_ref[d, :, l0:l0 + LANE] = st * tot_pair + upd


def _ssd_kernel(uf_ref, ub_ref, dtf_ref, dtb_ref, alog_ref, dtbias_ref, h0_ref,
                yf_ref, yb_ref, hT_ref, s_ref):
    t = pl.program_id(1)

    @pl.when(t == 0)
    def _():
        s_ref[...] = h0_ref[...]

    a_row = -jnp.exp(alog_ref[...])
    dtb_row = dtbias_ref[...]
    _ssd_direction(uf_ref, dtf_ref, a_row, dtb_row, s_ref, yf_ref, 0, False)
    _ssd_direction(ub_ref, dtb_ref, a_row, dtb_row, s_ref, yb_ref, 1, True)

    @pl.when(t == pl.num_programs(1) - 1)
    def _():
        hT_ref[...] = s_ref[...]


def _ssd_scan(u3, p3, dt_col0, alog_row, dtbias_row, h0):
    b, n, _ = u3.shape
    q = SSD_CHUNK
    nt = n // q
    dtb = dt_col0 // LANE
    hp = SSD_D_INNER
    return pl.pallas_call(
        _ssd_kernel,
        grid=(b, nt),
        in_specs=[pl.BlockSpec((None, q, SSD_XBC), lambda bi, t: (bi, t, 0)),
                  pl.BlockSpec((None, q, SSD_XBC), lambda bi, t: (bi, nt - 1 - t, 0)),
                  pl.BlockSpec((None, q, LANE), lambda bi, t: (bi, t, dtb)),
                  pl.BlockSpec((None, q, LANE), lambda bi, t: (bi, nt - 1 - t, dtb)),
                  pl.BlockSpec((1, LANE), lambda bi, t: (0, 0)),
                  pl.BlockSpec((1, LANE), lambda bi, t: (0, 0)),
                  pl.BlockSpec((None, 2, SSD_STATE, hp), lambda bi, t: (bi, 0, 0, 0))],
        out_specs=[pl.BlockSpec((None, q, hp), lambda bi, t: (bi, t, 0)),
                   pl.BlockSpec((None, q, hp), lambda bi, t: (bi, nt - 1 - t, 0)),
                   pl.BlockSpec((None, 2, SSD_STATE, hp), lambda bi, t: (bi, 0, 0, 0))],
        out_shape=[jax.ShapeDtypeStruct((b, n, hp), F32),
                   jax.ShapeDtypeStruct((b, n, hp), F32),
                   jax.ShapeDtypeStruct((b, 2, SSD_STATE, hp), F32)],
        scratch_shapes=[pltpu.VMEM((2, SSD_STATE, hp), F32)],
        compiler_params=_params(("parallel", "arbitrary")),
        name="ssd_scan",
    )(u3, u3, p3, p3, alog_row, dtbias_row, h0)


def _ssd_finish_kernel(yf_ref, yb_ref, x_ref, z_ref, dskip_ref, g_ref, o_ref):
    y = dskip_ref[...] * x_ref[...].astype(F32) + yf_ref[...] + yb_ref[...]
    y = y * _silu(z_ref[...].astype(F32))
    r = lax.rsqrt(jnp.mean(y * y, axis=-1, keepdims=True) + LN_EPS)
    o_ref[...] = (y * r * g_ref[...]).astype(o_ref.dtype)


def _ssd_finish(yf2, yb2, u2, p2, z_col0, dskip_row, g_row):
    m, w = yf2.shape
    tm = _pick(m, (1024, 512, 256, 128))
    zb = z_col0 // w
    row = lambda i: (i, 0)
    return pl.pallas_call(
        _ssd_finish_kernel,
        grid=(m // tm,),
        in_specs=[pl.BlockSpec((tm, w), row), pl.BlockSpec((tm, w), row),
                  pl.BlockSpec((tm, w), row),
                  pl.BlockSpec((tm, w), lambda i: (i, zb)),
                  pl.BlockSpec((1, w), lambda i: (0, 0)),
                  pl.BlockSpec((1, w), lambda i: (0, 0))],
        out_specs=pl.BlockSpec((tm, w), row),
        out_shape=jax.ShapeDtypeStruct((m, w), BF16),
        compiler_params=_params(("parallel",)),
        name="ssd_finish",
    )(yf2, yb2, u2, p2, dskip_row, g_row)


def _block_diag_rows(kv):
    lane = lax.broadcasted_iota(jnp.int32, kv.shape, 1)
    lo = lane < HEAD_DIM
    zero = jnp.zeros_like(kv)
    return jnp.concatenate([jnp.where(lo, kv, zero), jnp.where(lo, zero, kv)], axis=0)


def _dup_group(kv, g):
    lane = lax.broadcasted_iota(jnp.int32, kv.shape, 1)
    rolled = pltpu.roll(kv, HEAD_DIM, 1)
    return jnp.where(lane // HEAD_DIM == g, kv, rolled)


def _qk(q, kbd):
    return lax.dot_general(q, kbd, (((1,), (1,)), ((), ())), preferred_element_type=F32)


def _pair_softmax_pv(s_w, s_c, bias, vbd, vcbd, sink, out_dtype):
    nk = s_w.shape[1] // 2
    nc = s_c.shape[1] // 2
    tq = s_w.shape[0]
    pws, pcs, invs = [], [], []
    for h in range(2):
        sw = s_w[:, h * nk:(h + 1) * nk] + bias[h]
        sc = s_c[:, h * nc:(h + 1) * nc]
        m = jnp.maximum(jnp.max(sw, axis=-1, keepdims=True), jnp.max(sc, axis=-1, keepdims=True))
        if sink is not None:
            m = jnp.maximum(m, sink[h])
        pw = jnp.exp(sw - m)
        pc = jnp.exp(sc - m)
        l = jnp.sum(pw, axis=-1, keepdims=True) + jnp.sum(pc, axis=-1, keepdims=True)
        if sink is not None:
            l = l + jnp.exp(sink[h] - m)
        pws.append(pw.astype(BF16))
        pcs.append(pc.astype(BF16))
        invs.append(1.0 / l)
    o = jnp.dot(jnp.concatenate(pws, axis=1), vbd, preferred_element_type=F32)
    o = o + jnp.dot(jnp.concatenate(pcs, axis=1), vcbd, preferred_element_type=F32)
    lane = lax.broadcasted_iota(jnp.int32, (tq, LANE), 1)
    inv = jnp.where(lane < HEAD_DIM, jnp.broadcast_to(invs[0], (tq, LANE)),
                    jnp.broadcast_to(invs[1], (tq, LANE)))
    return (o * inv).astype(out_dtype)


def _na_kernel(q_ref, k_ref, v_ref, kc_ref, vc_ref, bias_ref, o_ref, *, rows):
    qb = pl.program_id(2)
    nk = NA_K_ROWS * GRID_W
    ks = jnp.clip(NA_Q_ROWS * qb - (NA_K_ROWS - NA_Q_ROWS) // 2, 0, rows - NA_K_ROWS) * GRID_W
    ks = pl.multiple_of(ks, GRID_W)
    for p in range(q_ref.shape[1] // LANE):
        cs = slice(p * LANE, (p + 1) * LANE)
        q = q_ref[:, cs] * (HEAD_DIM ** -0.5)
        kbd = _block_diag_rows(k_ref[pl.ds(ks, nk), cs])
        vbd = _block_diag_rows(v_ref[pl.ds(ks, nk), cs])
        kcbd = _block_diag_rows(kc_ref[:, cs])
        vcbd = _block_diag_rows(vc_ref[:, cs])
        s_w = _qk(q, kbd)
        s_c = _qk(q, kcbd)
        o_ref[:, cs] = _pair_softmax_pv(s_w, s_c, [bias_ref[2 * p], bias_ref[2 * p + 1]], vbd, vcbd,
                                        None, o_ref.dtype)


def _na_key_start(qb, rows):
    return min(max(NA_Q_ROWS * qb - (NA_K_ROWS - NA_Q_ROWS) // 2, 0), rows - NA_K_ROWS)


def _na_bias_tables(rpb, rows):
    nqb = rows // NA_Q_ROWS
    variants = [0, 1 if nqb > 2 else 0, nqb - 1]
    nh = rpb.shape[0]
    w = GRID_W
    pad = w - NA_WIN_COLS
    rp = jnp.pad(rpb.astype(F32), ((0, 0), (0, 0), (pad, pad)))
    toep = jnp.stack([rp[:, :, w - 1 - qc:2 * w - 1 - qc] for qc in range(w)], axis=2)
    qc, kc = np.arange(w)[:, None], np.arange(w)[None, :]
    ws = np.clip(qc - NA_WIN_COLS // 2, 0, w - NA_WIN_COLS)
    toep = jnp.where(jnp.asarray((kc >= ws) & (kc < ws + NA_WIN_COLS)), toep, NEG)
    masked = jnp.full((nh, w, w), NEG, F32)
    tabs = []
    for qb in variants:
        block_rows = []
        for qr_l in range(NA_Q_ROWS):
            qr = NA_Q_ROWS * qb + qr_l
            rs = min(max(qr - NA_WIN_ROWS // 2, 0), rows - NA_WIN_ROWS)
            blocks = []
            for kr_l in range(NA_K_ROWS):
                kr = _na_key_start(qb, rows) + kr_l
                inside = rs <= kr < rs + NA_WIN_ROWS
                blocks.append(toep[:, kr - qr + NA_WIN_ROWS - 1] if inside else masked)
            block_rows.append(jnp.concatenate(blocks, axis=-1))
        tabs.append(jnp.concatenate(block_rows, axis=-2))
    return jnp.stack(tabs, axis=0)


def _na_attention(p3, pc3, q_col0, k_col0, v_col0, bias_tab):
    b, n, _ = p3.shape
    nc = pc3.shape[1]
    rows = n // GRID_W
    tq = NA_Q_ROWS * GRID_W
    nk = NA_K_ROWS * GRID_W
    nqb = n // tq
    wb = 4 * LANE
    qc, kc, vc = q_col0 // wb, k_col0 // wb, v_col0 // wb
    ngrp = 1

    def variant(qb):
        return jnp.where(qb == 0, 0, jnp.where(qb == nqb - 1, 2, 1))

    return pl.pallas_call(
        functools.partial(_na_kernel, rows=rows),
        grid=(b, ngrp, nqb),
        in_specs=[pl.BlockSpec((None, tq, wb), lambda bi, h, i: (bi, i, qc + h)),
                  pl.BlockSpec((None, n, wb), lambda bi, h, i: (bi, 0, kc + h)),
                  pl.BlockSpec((None, n, wb), lambda bi, h, i: (bi, 0, vc + h)),
                  pl.BlockSpec((None, nc, wb), lambda bi, h, i: (bi, 0, kc + h)),
                  pl.BlockSpec((None, nc, wb), lambda bi, h, i: (bi, 0, vc + h)),
                  pl.BlockSpec((None, wb // HEAD_DIM, tq, nk), lambda bi, h, i: (variant(i), h, 0, 0))],
        out_specs=pl.BlockSpec((None, tq, wb), lambda bi, h, i: (bi, i, h)),
        out_shape=jax.ShapeDtypeStruct((b, n, ngrp * wb), BF16),
        compiler_params=_params(("parallel", "parallel", "arbitrary")),
        name="na_attention",
    )(p3, p3, p3, pc3, pc3, bias_tab)


def _rope(x, cos, sin_signed):
    lane = lax.broadcasted_iota(jnp.int32, x.shape, 1)
    first = (lane % 32) < 16
    partner = jnp.where(first, pltpu.roll(x, LANE - 16, 1), pltpu.roll(x, 16, 1))
    return x * cos + partner * sin_signed


def _sink_pair(sink_ref):
    s = sink_ref[...]
    return [s[:, 0:1], s[:, HEAD_DIM:HEAD_DIM + 1]]


def _swa_kernel(q_ref, k_ref, v_ref, kc_ref, vc_ref, cos_ref, sin_ref, sink_ref, o_ref, *, n):
    qb = pl.program_id(1)
    q0 = pl.multiple_of(qb * SWA_Q, SWA_Q)
    ks = pl.multiple_of(jnp.clip(qb * SWA_Q - SWA_WINDOW, 0, n - SWA_K), SWA_WINDOW)
    cos_q, sin_q = cos_ref[pl.ds(q0, SWA_Q), :], sin_ref[pl.ds(q0, SWA_Q), :]
    kk = _rope(k_ref[pl.ds(ks, SWA_K), :].astype(F32), cos_ref[pl.ds(ks, SWA_K), :],
               sin_ref[pl.ds(ks, SWA_K), :])
    vv = v_ref[pl.ds(ks, SWA_K), :].astype(F32)
    kc, vc = kc_ref[...].astype(F32), vc_ref[...].astype(F32)
    qpos = q0 + lax.broadcasted_iota(jnp.int32, (SWA_Q, SWA_K), 0)
    kpos = ks + lax.broadcasted_iota(jnp.int32, (SWA_Q, SWA_K), 1)
    bias = jnp.where(jnp.abs(kpos - qpos) <= SWA_WINDOW, 0.0, NEG)
    pairs_per_group = 2
    for p in range(q_ref.shape[1] // LANE):
        g = p // pairs_per_group
        if p % pairs_per_group == 0:
            kbd = _block_diag_rows(_dup_group(kk, g).astype(BF16))
            vbd = _block_diag_rows(_dup_group(vv, g).astype(BF16))
            kcbd = _block_diag_rows(_dup_group(kc, g).astype(BF16))
            vcbd = _block_diag_rows(_dup_group(vc, g).astype(BF16))
        cs = slice(p * LANE, (p + 1) * LANE)
        q = _rope(q_ref[:, cs].astype(F32), cos_q, sin_q)
        q = (q * (HEAD_DIM ** -0.5)).astype(BF16)
        s_w = _qk(q, kbd)
        s_c = _qk(q, kcbd)
        o_ref[:, cs] = _pair_softmax_pv(s_w, s_c, [bias, bias], vbd, vcbd, _sink_pair(sink_ref.at[p]),
                                        o_ref.dtype)


def _swa_attention(p3, pc3, q_col0, kv_col0, cos_tab, sin_tab, sink_rows):
    b, n, _ = p3.shape
    nc = pc3.shape[1]
    nqb = n // SWA_Q
    wb = 4 * LANE
    qc, kc = q_col0 // wb, kv_col0 // LANE
    return pl.pallas_call(
        functools.partial(_swa_kernel, n=n),
        grid=(b, nqb),
        in_specs=[pl.BlockSpec((None, SWA_Q, wb), lambda bi, i: (bi, i, qc)),
                  pl.BlockSpec((None, n, LANE), lambda bi, i: (bi, 0, kc)),
                  pl.BlockSpec((None, n, LANE), lambda bi, i: (bi, 0, kc + 1)),
                  pl.BlockSpec((None, nc, LANE), lambda bi, i: (bi, 0, kc)),
                  pl.BlockSpec((None, nc, LANE), lambda bi, i: (bi, 0, kc + 1)),
                  pl.BlockSpec((n, LANE), lambda bi, i: (0, 0)),
                  pl.BlockSpec((n, LANE), lambda bi, i: (0, 0)),
                  pl.BlockSpec((wb // LANE, 1, LANE), lambda bi, i: (0, 0, 0))],
        out_specs=pl.BlockSpec((None, SWA_Q, wb), lambda bi, i: (bi, i, 0)),
        out_shape=jax.ShapeDtypeStruct((b, n, wb), BF16),
        compiler_params=_params(("parallel", "arbitrary")),
        name="swa_attention",
    )(p3, p3, p3, pc3, pc3, cos_tab, sin_tab, sink_rows)


def _ctx_attn_kernel(q_ref, k_ref, v_ref, sink_ref, o_ref, *, grouped):
    hp = pl.program_id(1)
    q = q_ref[...] * (HEAD_DIM ** -0.5)
    k, v = k_ref[...], v_ref[...]
    if grouped:
        g = hp // 2
        k = _dup_group(k.astype(F32), g).astype(BF16)
        v = _dup_group(v.astype(F32), g).astype(BF16)
    kbd, vbd = _block_diag_rows(k), _block_diag_rows(v)
    s = _qk(q, kbd)
    t = q.shape[0]
    sink = _sink_pair(sink_ref) if grouped else None
    ps, invs = [], []
    for h in range(2):
        sh = s[:, h * t:(h + 1) * t]
        m = jnp.max(sh, axis=-1, keepdims=True)
        if sink is not None:
            m = jnp.maximum(m, sink[h])
        p = jnp.exp(sh - m)
        l = jnp.sum(p, axis=-1, keepdims=True)
        if sink is not None:
            l = l + jnp.exp(sink[h] - m)
        ps.append(p.astype(BF16))
        invs.append(1.0 / l)
    o = jnp.dot(jnp.concatenate(ps, axis=1), vbd, preferred_element_type=F32)
    lane = lax.broadcasted_iota(jnp.int32, (t, LANE), 1)
    inv = jnp.where(lane < HEAD_DIM, jnp.broadcast_to(invs[0], (t, LANE)),
                    jnp.broadcast_to(invs[1], (t, LANE)))
    o_ref[...] = (o * inv).astype(o_ref.dtype)


def _ctx_attention(pc3, q_col0, k_col0, v_col0, sink_rows, grouped):
    b, nc, _ = pc3.shape
    qc, kc, vc = q_col0 // LANE, k_col0 // LANE, v_col0 // LANE
    npair = 4
    kv_blk = (lambda h: 0) if grouped else (lambda h: h)
    return pl.pallas_call(
        functools.partial(_ctx_attn_kernel, grouped=grouped),
        grid=(b, npair),
        in_specs=[pl.BlockSpec((None, nc, LANE), lambda bi, h: (bi, 0, qc + h)),
                  pl.BlockSpec((None, nc, LANE), lambda bi, h: (bi, 0, kc + kv_blk(h))),
                  pl.BlockSpec((None, nc, LANE), lambda bi, h: (bi, 0, vc + kv_blk(h))),
                  pl.BlockSpec((None, 1, LANE), lambda bi, h: (h, 0, 0))],
        out_specs=pl.BlockSpec((None, nc, LANE), lambda bi, h: (bi, 0, h)),
        out_shape=jax.ShapeDtypeStruct((b, nc, npair * LANE), BF16),
        compiler_params=_params(("parallel", "parallel")),
        name="ctx_attention",
    )(pc3, pc3, pc3, sink_rows)


def _fn_channel_kernel(x_ref, cs_ref, o_ref):
    w = x_ref.shape[1]
    ngroups = w // FN_GROUP_DIM
    for g in range(ngroups):
        xg = x_ref[:, g * FN_GROUP_DIM:(g + 1) * FN_GROUP_DIM]
        ab = jnp.dot(xg, cs_ref[...], preferred_element_type=F32)
        o_ref[:, g * FN_GROUP_DIM:(g + 1) * FN_GROUP_DIM] = ab[:, :FN_GROUP_DIM].astype(o_ref.dtype)
        o_ref[:, w + g * FN_GROUP_DIM:w + (g + 1) * FN_GROUP_DIM] = ab[:, FN_GROUP_DIM:].astype(o_ref.dtype)


def _fn_channel(p2, col0, width):
    m = p2.shape[0]
    tm = _pick(m, (1024, 512, 256, 128))
    k = np.arange(FN_GROUP_DIM)
    ang = 2.0 * np.pi * ((k[:, None] * k[None, :]) % FN_GROUP_DIM) / FN_GROUP_DIM
    cs = jnp.asarray(np.concatenate([np.cos(ang), np.sin(ang)], axis=1), BF16)
    cb = col0 // width
    return pl.pallas_call(
        _fn_channel_kernel,
        grid=(m // tm,),
        in_specs=[pl.BlockSpec((tm, width), lambda i: (i, cb)),
                  pl.BlockSpec((FN_GROUP_DIM, 2 * FN_GROUP_DIM), lambda i: (0, 0))],
        out_specs=pl.BlockSpec((tm, 2 * width), lambda i: (i, 0)),
        out_shape=jax.ShapeDtypeStruct((m, 2 * width), BF16),
        compiler_params=_params(("parallel",)),
        name="fn_channel",
    )(p2, cs)


def _fn_position_kernel(ac_ref, as_ref, bc_ref, bs_ref, ab_ref, o_ref, c_ref, s_ref, *, scale):
    @pl.when(pl.program_id(1) == 0)
    def _():
        for j in range(ac_ref.shape[0]):
            ca, sa = ac_ref[j:j + 1, :], as_ref[j:j + 1, :]
            cb, sb = bc_ref[...], bs_ref[...]
            c_ref[j * DFT_ROWS:(j + 1) * DFT_ROWS, :] = (ca * cb - sa * sb).astype(BF16)
            s_ref[j * DFT_ROWS:(j + 1) * DFT_ROWS, :] = (sa * cb + ca * sb).astype(BF16)

    w = o_ref.shape[1]
    y = jnp.dot(c_ref[...], ab_ref[:, :w], preferred_element_type=F32)
    y = y - jnp.dot(s_ref[...], ab_ref[:, w:], preferred_element_type=F32)
    o_ref[...] = (y * scale).astype(o_ref.dtype)


def _fn_position(ab3):
    b, n, w2 = ab3.shape
    w = w2 // 2
    tm = _pick(n, (512, 256, 128, 64))
    jc = tm // DFT_ROWS
    n1 = n // DFT_ROWS
    k = np.arange(n, dtype=np.int64)
    j1 = np.arange(n1, dtype=np.int64)
    j2 = np.arange(DFT_ROWS, dtype=np.int64)
    ang_a = 2.0 * np.pi * ((j1[:, None] * DFT_ROWS * k[None, :]) % n) / n
    ang_b = 2.0 * np.pi * ((j2[:, None] * k[None, :]) % n) / n
    ac, as_ = jnp.asarray(np.cos(ang_a), F32), jnp.asarray(np.sin(ang_a), F32)
    bc, bs = jnp.asarray(np.cos(ang_b), F32), jnp.asarray(np.sin(ang_b), F32)
    scale = 1.0 / math.sqrt(n * FN_GROUP_DIM)
    if n1 % 8 != 0 and jc != n1:
        raise ValueError("unsupported sequence length for the position DFT tiling")
    return pl.pallas_call(
        functools.partial(_fn_position_kernel, scale=scale),
        grid=(n // tm, b),
        in_specs=[pl.BlockSpec((jc, n), lambda i, bi: (i, 0)),
                  pl.BlockSpec((jc, n), lambda i, bi: (i, 0)),
                  pl.BlockSpec((DFT_ROWS, n), lambda i, bi: (0, 0)),
                  pl.BlockSpec((DFT_ROWS, n), lambda i, bi: (0, 0)),
                  pl.BlockSpec((None, n, w2), lambda i, bi: (bi, 0, 0))],
        out_specs=pl.BlockSpec((None, tm, w), lambda i, bi: (bi, i, 0)),
        out_shape=jax.ShapeDtypeStruct((b, n, w), BF16),
        scratch_shapes=[pltpu.VMEM((tm, n), BF16), pltpu.VMEM((tm, n), BF16)],
        compiler_params=_params(("parallel", "arbitrary")),
        name="fn_position",
    )(ac, as_, bc, bs, ab3)


def _residual_ln(o_ref, x_ref, gate_ref, lng_ref, lnb_ref, alpha):
    v = alpha * x_ref[...] + gate_ref[...] * o_ref[...]
    o_ref[...] = _layer_norm(v) * lng_ref[...] + lnb_ref[...]


def _merge_out_kernel(y0, y1, y2, y3, g0, g1, g2, g3, wb_ref, bg_ref, wo_ref, x_ref, gate_ref, lng_ref,
                      lnb_ref, o_ref, *, alpha):
    j = pl.program_id(1)

    @pl.when(j == 0)
    def _():
        o_ref[...] = jnp.zeros_like(o_ref)

    merged = None
    for i, (y, g) in enumerate(((y0, g0), (y1, g1), (y2, g2), (y3, g3))):
        gate = _sigmoid(g[...].astype(F32) + bg_ref[i])
        term = gate * jnp.dot(y[...], wb_ref[i], preferred_element_type=F32)
        merged = term if merged is None else merged + term
    o_ref[...] += jnp.dot(merged.astype(BF16), wo_ref[...], preferred_element_type=F32)

    @pl.when(j == pl.num_programs(1) - 1)
    def _():
        _residual_ln(o_ref, x_ref, gate_ref, lng_ref, lnb_ref, alpha)


def _merge_out(ys, p2, gate_col0, wb, bg3, w_out, x2, mod3, mod_row, gate_chunk, ln_g, ln_b, alpha):
    m = p2.shape[0]
    d = wb.shape[2]
    tm = mod_row.tm
    tn = _pick(d, (1024, 512, 256, 128))
    gb = gate_col0 // tn
    per = d // tn
    bw = ys[0].shape[1]
    y_specs = [pl.BlockSpec((tm, bw), lambda i, j: (i, 0)) for _ in range(N_BRANCH)]
    g_specs = [pl.BlockSpec((tm, tn), functools.partial(lambda i, j, br: (i, gb + br * per + j), br=br))
               for br in range(N_BRANCH)]
    return pl.pallas_call(
        functools.partial(_merge_out_kernel, alpha=alpha),
        grid=(m // tm, per),
        in_specs=y_specs + g_specs + [
            pl.BlockSpec((N_BRANCH, bw, tn), lambda i, j: (0, 0, j)),
            pl.BlockSpec((N_BRANCH, 1, tn), lambda i, j: (0, 0, j)),
            pl.BlockSpec((tn, d), lambda i, j: (j, 0)),
            pl.BlockSpec((tm, d), lambda i, j: (i, 0)),
            pl.BlockSpec((None, 1, d), lambda i, j: (mod_row(i), 0, gate_chunk)),
            pl.BlockSpec((1, d), lambda i, j: (0, 0)),
            pl.BlockSpec((1, d), lambda i, j: (0, 0))],
        out_specs=pl.BlockSpec((tm, d), lambda i, j: (i, 0)),
        out_shape=jax.ShapeDtypeStruct((m, d), F32),
        compiler_params=_params(("parallel", "arbitrary")),
        name="merge_out",
    )(*ys, p2, p2, p2, p2, wb, bg3, w_out, x2, mod3, ln_g.reshape(1, d), ln_b.reshape(1, d))


def _mm_resln_kernel(a_ref, w_ref, x_ref, gate_ref, lng_ref, lnb_ref, o_ref, *, alpha):
    k = pl.program_id(1)

    @pl.when(k == 0)
    def _():
        o_ref[...] = jnp.zeros_like(o_ref)

    o_ref[...] += jnp.dot(a_ref[...], w_ref[...], preferred_element_type=F32)

    @pl.when(k == pl.num_programs(1) - 1)
    def _():
        _residual_ln(o_ref, x_ref, gate_ref, lng_ref, lnb_ref, alpha)


def _mm_resln(a2, w, x2, mod3, mod_row, gate_chunk, ln_g, ln_b, alpha, tk):
    m, kdim = a2.shape
    d = w.shape[1]
    tm = mod_row.tm
    return pl.pallas_call(
        functools.partial(_mm_resln_kernel, alpha=alpha),
        grid=(m // tm, kdim // tk),
        in_specs=[pl.BlockSpec((tm, tk), lambda i, k: (i, k)),
                  pl.BlockSpec((tk, d), lambda i, k: (k, 0)),
                  pl.BlockSpec((tm, d), lambda i, k: (i, 0)),
                  pl.BlockSpec((None, 1, d), lambda i, k: (mod_row(i), 0, gate_chunk)),
                  pl.BlockSpec((1, d), lambda i, k: (0, 0)),
                  pl.BlockSpec((1, d), lambda i, k: (0, 0))],
        out_specs=pl.BlockSpec((tm, d), lambda i, k: (i, 0)),
        out_shape=jax.ShapeDtypeStruct((m, d), F32),
        compiler_params=_params(("parallel", "arbitrary")),
        name="matmul_residual_ln",
    )(a2, w, x2, mod3, ln_g.reshape(1, d), ln_b.reshape(1, d))


HALO_ROWS = 8


def _ffn_up_kernel(x_ref, xp_ref, xn_ref, sh_ref, sc_ref, wg_ref, wu_ref, cw_ref, cb_ref, o_ref, h_ref,
                   *, seq):
    i = pl.program_id(0)
    tm = x_ref.shape[0]

    @pl.when(pl.program_id(1) == 0)
    def _():
        scale, shift = 1.0 + sc_ref[...], sh_ref[...]
        h_ref[0:tm, :] = (_layer_norm(x_ref[...]) * scale + shift).astype(BF16)
        halo = jnp.concatenate([xp_ref[...], xn_ref[...]], axis=0)
        h_ref[tm:tm + 2 * HALO_ROWS, :] = (_layer_norm(halo) * scale + shift).astype(BF16)

    g_all = jnp.dot(h_ref[...], wg_ref[...], preferred_element_type=F32)
    u = jnp.dot(h_ref[0:tm, :], wu_ref[...], preferred_element_type=F32)
    g = g_all[0:tm, :]
    has_prev = jnp.where((i * tm) % seq == 0, 0.0, 1.0)
    has_next = jnp.where(((i + 1) * tm) % seq == 0, 0.0, 1.0)
    g_prev = g_all[tm + HALO_ROWS - 1:tm + HALO_ROWS, :] * has_prev
    g_next = g_all[tm + HALO_ROWS:tm + HALO_ROWS + 1, :] * has_next
    row = lax.broadcasted_iota(jnp.int32, g.shape, 0)
    above = jnp.where(row == 0, g_prev, pltpu.roll(g, 1, 0))
    below = jnp.where(row == tm - 1, g_next, pltpu.roll(g, tm - 1, 0))
    y = above * cw_ref[0:1, :] + g * cw_ref[1:2, :] + below * cw_ref[2:3, :] + cb_ref[...]
    o_ref[...] = (_silu(y) * u).astype(o_ref.dtype)


def _ffn_up(x2, mod3, mod_row, sh_chunk, sc_chunk, w_up, conv_w, conv_b, tn, seq):
    m, d = x2.shape
    dff = w_up.shape[1] // 2
    tm = mod_row.tm
    if conv_w.shape[0] != 3 or seq % tm != 0 or tm % HALO_ROWS != 0:
        raise ValueError("unsupported ConvFFN tiling")
    ub = dff // tn
    hb = tm // HALO_ROWS
    last_hb = m // HALO_ROWS - 1
    return pl.pallas_call(
        functools.partial(_ffn_up_kernel, seq=seq),
        grid=(m // tm, dff // tn),
        in_specs=[pl.BlockSpec((tm, d), lambda i, j: (i, 0)),
                  pl.BlockSpec((HALO_ROWS, d), lambda i, j: (jnp.maximum(i * hb - 1, 0), 0)),
                  pl.BlockSpec((HALO_ROWS, d), lambda i, j: (jnp.minimum((i + 1) * hb, last_hb), 0)),
                  pl.BlockSpec((None, 1, d), lambda i, j: (mod_row(i), 0, sh_chunk)),
                  pl.BlockSpec((None, 1, d), lambda i, j: (mod_row(i), 0, sc_chunk)),
                  pl.BlockSpec((d, tn), lambda i, j: (0, j)),
                  pl.BlockSpec((d, tn), lambda i, j: (0, ub + j)),
                  pl.BlockSpec((3, tn), lambda i, j: (0, j)),
                  pl.BlockSpec((1, tn), lambda i, j: (0, j))],
        out_specs=pl.BlockSpec((tm, tn), lambda i, j: (i, j)),
        out_shape=jax.ShapeDtypeStruct((m, dff), BF16),
        scratch_shapes=[pltpu.VMEM((tm + 2 * HALO_ROWS, d), BF16)],
        compiler_params=_params(("parallel", "arbitrary")),
        name="ffn_up",
    )(x2, x2, x2, mod3, mod3, w_up, w_up, conv_w, conv_b.reshape(1, dff))


def _rope_tables(n):
    t = jnp.arange(n)
    rows = (t // GRID_W).astype(F32)
    cols = (t % GRID_W).astype(F32)
    n_freq = HEAD_DIM // 4
    inv = ROPE_BASE ** (-jnp.arange(n_freq, dtype=F32) / n_freq)
    ar, ac = rows[:, None] * inv, cols[:, None] * inv
    cos = jnp.concatenate([jnp.cos(ar), jnp.cos(ar), jnp.cos(ac), jnp.cos(ac)], axis=1)
    sin = jnp.concatenate([-jnp.sin(ar), jnp.sin(ar), -jnp.sin(ac), jnp.sin(ac)], axis=1)
    return jnp.tile(cos, (1, 2)), jnp.tile(sin, (1, 2))


def _branch_params(conv_w, conv_b, a_log, dt_bias, d_skip, norm_g, rpb, sink, rows, cos_tab, sin_tab):
    pad = jnp.zeros((1, LANE - 2 * SSD_HEADS), F32)
    return dict(
        conv_w=conv_w, conv_b=conv_b,
        alog_row=jnp.concatenate([a_log.reshape(1, -1), pad], axis=1),
        dtb_row=jnp.concatenate([dt_bias.reshape(1, -1), pad], axis=1),
        dskip_row=jnp.repeat(d_skip, HEAD_DIM).reshape(1, SSD_D_INNER),
        g_row=norm_g.reshape(1, SSD_D_INNER),
        sink_rows=jnp.repeat(sink, HEAD_DIM).reshape(4, 1, LANE),
        bias_tab=_na_bias_tables(rpb, rows),
        cos_tab=cos_tab, sin_tab=sin_tab)


def _mixer_branches(p3, pc3, cols, prm, ctx_out):
    bsz, n, npad = p3.shape
    nctx = pc3.shape[1]
    p2 = p3.reshape(bsz * n, npad)
    pc2 = pc3.reshape(bsz * nctx, npad)
    flat = lambda t: t.reshape(-1, t.shape[-1])

    u3 = _conv_silu(p3, cols["xbc"], SSD_XBC, prm["conv_w"], prm["conv_b"])
    uc3 = _conv_silu(pc3, cols["xbc"], SSD_XBC, prm["conv_w"], prm["conv_b"])
    h_zero = jnp.zeros((bsz, 2, SSD_STATE, SSD_D_INNER), F32)
    yfc, ybc, h_ctx = _ssd_scan(uc3, pc3, cols["dt"], prm["alog_row"], prm["dtb_row"], h_zero)
    yf, yb, _ = _ssd_scan(u3, p3, cols["dt"], prm["alog_row"], prm["dtb_row"], h_ctx)
    y_ssd = _ssd_finish(flat(yf), flat(yb), flat(u3), p2, cols["z"], prm["dskip_row"], prm["g_row"])
    y_na = flat(_na_attention(p3, pc3, cols["nq"], cols["nk"], cols["nv"], prm["bias_tab"]))
    y_swa = flat(_swa_attention(p3, pc3, cols["sq"], cols["skv"], prm["cos_tab"], prm["sin_tab"],
                                prm["sink_rows"]))
    y_fn = flat(_fn_position(_fn_channel(p2, cols["fn"], 512).reshape(bsz, n, -1)))
    ys = [y_ssd, y_na, y_swa, y_fn]
    if not ctx_out:
        return ys, None
    yc_ssd = _ssd_finish(flat(yfc), flat(ybc), flat(uc3), pc2, cols["z"], prm["dskip_row"], prm["g_row"])
    yc_na = flat(_ctx_attention(pc3, cols["nq"], cols["nk"], cols["nv"], prm["sink_rows"], False))
    yc_swa = flat(_ctx_attention(pc3, cols["sq"], cols["skv"], cols["skv"] + LANE, prm["sink_rows"], True))
    yc_fn = flat(_fn_position(_fn_channel(pc2, cols["fn"], 512).reshape(bsz, nctx, -1)))
    return ys, [yc_ssd, yc_na, yc_swa, yc_fn]


def kernel(x, c, ctx, c_ctx, w_ada, b_ada, w_in, b_gate, ssd_conv_w, ssd_conv_b, ssd_a_log,
           ssd_dt_bias, ssd_d, ssd_norm_g, na_rpb, swa_sink, w_branch, w_out, ln1_g, ln1_b,
           ln2_g, ln2_b, ffn_w_up, ffn_conv_w, ffn_conv_b, ffn_w_down):
    bsz, n, d = x.shape
    nctx = ctx.shape[1]
    depth = w_ada.shape[0]
    dff = ffn_w_down.shape[1]
    alpha = (2.0 * depth) ** 0.25
    rows = n // GRID_W

    g_col = 0
    z_col = N_BRANCH * d
    fn_col = z_col + 512
    sq_col = fn_col + 512
    nq_col = sq_col + 512
    nk_col = nq_col + 512
    nv_col = nk_col + 512
    xbc_col = nv_col + 512
    skv_col = xbc_col + SSD_XBC
    dt_col = skv_col + 256
    n_used = dt_col + LANE
    n_in = -(-n_used // 256) * 256
    tn_in = _pick(n_in, (1280, 1024, 768, 512, 256))
    cols = dict(z=z_col, fn=fn_col, sq=sq_col, nq=nq_col, nk=nk_col, nv=nv_col, xbc=xbc_col,
                skv=skv_col, dt=dt_col)

    o_z, o_xbc, o_dt = 0, 512, 512 + SSD_XBC
    o_na = o_dt + 2 * SSD_HEADS
    o_sq = o_na + 3 * 512
    o_skv = o_sq + 512
    o_fn = o_skv + 256
    o_g = o_fn + 512

    in_segments = [(o_g, N_BRANCH * d), (o_z, 512), (o_fn, 512), (o_sq, 512), (o_na, 3 * 512),
                   (o_xbc, SSD_XBC), (o_skv, 256), (o_dt, 2 * SSD_HEADS)]

    n_rows = -(-(bsz + 1) // 8) * 8
    cvec = jnp.concatenate([c, c_ctx[None], jnp.zeros((n_rows - bsz - 1, d), F32)], axis=0)
    mods = _ada_mods(cvec, w_ada, b_ada)

    lat_row = _ModRow(_pick(n, (1024, 512, 256)), seq=n)
    ctx_row = _ModRow(_pick(bsz * nctx, (1024, 512, 256)), fixed=bsz)
    lat_row_h = _ModRow(_pick(n, (512, 256)), seq=n)
    ctx_row_h = _ModRow(_pick(bsz * nctx, (512, 256)), fixed=bsz)
    ctx_row_seq = _ModRow(_pick(nctx, (1024, 512, 256)), fixed=bsz)
    tn_up = _pick(dff, (512, 256, 128))
    tk_down = _pick(dff, (512, 256, 128))

    cos_tab, sin_tab = _rope_tables(n)

    x2 = x.reshape(bsz * n, d)
    xc2 = ctx.reshape(bsz * nctx, d)
    for l in range(depth):
        ctx_out = l < depth - 1
        mod3 = mods[l].reshape(n_rows, 1, 6 * d)
        w_in_p = _stage_permuted(w_in, l, in_segments, n_in)
        wb = _stage_weight(w_branch.reshape(depth, N_BRANCH * BRANCH_WIDTH, d), l)
        wb = wb.reshape(N_BRANCH, BRANCH_WIDTH, d)
        bg3 = b_gate[l].reshape(N_BRANCH, 1, d)
        w_out_b = _stage_weight(w_out, l)
        w_up_b = _stage_weight(ffn_w_up, l)
        w_down_b = _stage_weight(ffn_w_down, l)
        prm = _branch_params(ssd_conv_w[l], ssd_conv_b[l], ssd_a_log[l], ssd_dt_bias[l], ssd_d[l],
                             ssd_norm_g[l], na_rpb[l], swa_sink[l], rows, cos_tab, sin_tab)

        p2 = _lnmod_matmul(x2, mod3, lat_row, 0, 1, w_in_p, tn_in)
        pc2 = _lnmod_matmul(xc2, mod3, ctx_row, 0, 1, w_in_p, tn_in)
        p3 = p2.reshape(bsz, n, n_in)
        pc3 = pc2.reshape(bsz, nctx, n_in)

        ys, ycs = _mixer_branches(p3, pc3, cols, prm, ctx_out)

        x_mid = _merge_out(ys, p2, g_col, wb, bg3, w_out_b, x2, mod3, lat_row_h, 2, ln1_g[l], ln1_b[l],
                           alpha)
        act = _ffn_up(x_mid, mod3, lat_row, 3, 4, w_up_b, ffn_conv_w[l], ffn_conv_b[l], tn_up, n)
        x2 = _mm_resln(act, w_down_b, x_mid, mod3, lat_row, 5, ln2_g[l], ln2_b[l], alpha, tk_down)

        if ctx_out:
            xc_mid = _merge_out(ycs, pc2, g_col, wb, bg3, w_out_b, xc2, mod3, ctx_row_h, 2,
                                ln1_g[l], ln1_b[l], alpha)
            act_c = _ffn_up(xc_mid, mod3, ctx_row_seq, 3, 4, w_up_b, ffn_conv_w[l], ffn_conv_b[l],
                            tn_up, nctx)
            xc2 = _mm_resln(act_c, w_down_b, xc_mid, mod3, ctx_row, 5, ln2_g[l], ln2_b[l], alpha,
                            tk_down)

    return x2.reshape(bsz, n, d)
```

```python
import functools
import math

import numpy as np
import jax
import jax.numpy as jnp
from jax import lax
from jax.experimental import pallas as pl
from jax.experimental.pallas import tpu as pltpu

F32 = jnp.float32
BF16 = jnp.bfloat16

GRID_W = 64
HEAD_DIM = 64
SSD_HEADS = 8
SSD_D_INNER = 512
SSD_STATE = 128
SSD_GROUPS = 2
SSD_CHUNK = 128
SSD_XBC = 1024
NA_WIN_ROWS = 8
NA_WIN_COLS = 16
NA_Q_ROWS = 4
NA_K_ROWS = 12
SWA_WINDOW = 128
SWA_Q = 256
SWA_K = 512
FN_GROUP_DIM = 128
N_BRANCH = 4
BRANCH_WIDTH = 512
ROPE_BASE = 10000.0
LN_EPS = 1e-6
NEG = -1e30
DFT_ROWS = 64

LANE = 128
VMEM_LIMIT = 56 * 1024 * 1024


def _pick(dim, prefs):
    for p in prefs:
        if p <= dim and dim % p == 0:
            return p
    return dim


def _params(sem):
    return pltpu.CompilerParams(dimension_semantics=sem, vmem_limit_bytes=VMEM_LIMIT)


def _layer_norm(v):
    mu = jnp.mean(v, axis=-1, keepdims=True)
    vc = v - mu
    var = jnp.mean(vc * vc, axis=-1, keepdims=True)
    return vc * lax.rsqrt(var + LN_EPS)


def _sigmoid(v):
    return 0.5 * jnp.tanh(0.5 * v) + 0.5


def _silu(v):
    return v * _sigmoid(v)


def _cast_kernel(w_ref, o_ref):
    o_ref[...] = w_ref[...].astype(o_ref.dtype)


def _stage_weight(w_stack, l):
    _, r, c = w_stack.shape
    tr = _pick(r, (512, 256, 128))
    tc = _pick(c, (2048, 1408, 1024, 512, 256, 128))
    return pl.pallas_call(
        _cast_kernel,
        grid=(r // tr, c // tc),
        in_specs=[pl.BlockSpec((None, tr, tc), lambda i, j: (l, i, j))],
        out_specs=pl.BlockSpec((tr, tc), lambda i, j: (i, j)),
        out_shape=jax.ShapeDtypeStruct((r, c), BF16),
        compiler_params=_params(("parallel", "parallel")),
        name="stage_weight",
    )(w_stack)


def _permute_kernel(wt_ref, o_ref, *, segments, pad):
    wt = wt_ref[...]
    parts = [wt[a:a + n, :] for a, n in segments]
    if pad:
        parts.append(jnp.zeros((pad, wt.shape[1]), wt.dtype))
    o_ref[...] = jnp.concatenate(parts, axis=0).T.astype(o_ref.dtype)


def _stage_permuted(w_stack, l, segments, n_out):
    _, r, c = w_stack.shape
    tr = _pick(r, (128,))
    pad = n_out - sum(n for _, n in segments)
    return pl.pallas_call(
        functools.partial(_permute_kernel, segments=tuple(segments), pad=pad),
        grid=(r // tr,),
        in_specs=[pl.BlockSpec((None, c, tr), lambda i: (l, 0, i))],
        out_specs=pl.BlockSpec((tr, n_out), lambda i: (i, 0)),
        out_shape=jax.ShapeDtypeStruct((r, n_out), BF16),
        compiler_params=_params(("parallel",)),
        name="stage_permuted",
    )(jnp.swapaxes(w_stack, 1, 2))


def _ada_kernel(c_ref, w_ref, b_ref, o_ref):
    s = _silu(c_ref[...])
    o_ref[...] = jnp.dot(s.astype(BF16), w_ref[...].astype(BF16),
                         preferred_element_type=F32) + b_ref[...]


def _ada_mods(cvec, w_ada, b_ada):
    depth, d, n6 = w_ada.shape
    r = cvec.shape[0]
    tn = _pick(n6, (1024, 512, 256, 128))
    return pl.pallas_call(
        _ada_kernel,
        grid=(depth, n6 // tn),
        in_specs=[pl.BlockSpec((r, d), lambda l, j: (0, 0)),
                  pl.BlockSpec((None, d, tn), lambda l, j: (l, 0, j)),
                  pl.BlockSpec((None, 1, tn), lambda l, j: (l, 0, j))],
        out_specs=pl.BlockSpec((None, r, tn), lambda l, j: (l, 0, j)),
        out_shape=jax.ShapeDtypeStruct((depth, r, n6), F32),
        compiler_params=_params(("parallel", "parallel")),
        name="ada_mods",
    )(cvec, w_ada, b_ada.reshape(depth, 1, n6))


def _lnmod_mm_kernel(x_ref, sh_ref, sc_ref, w_ref, o_ref, h_ref):
    @pl.when(pl.program_id(1) == 0)
    def _():
        y = _layer_norm(x_ref[...])
        h_ref[...] = (y * (1.0 + sc_ref[...]) + sh_ref[...]).astype(BF16)

    o_ref[...] = jnp.dot(h_ref[...], w_ref[...], preferred_element_type=F32).astype(o_ref.dtype)


def _lnmod_matmul(x2, mod3, mod_row, sh_chunk, sc_chunk, w, tn):
    m, d = x2.shape
    n = w.shape[1]
    tm = mod_row.tm
    return pl.pallas_call(
        _lnmod_mm_kernel,
        grid=(m // tm, n // tn),
        in_specs=[pl.BlockSpec((tm, d), lambda i, j: (i, 0)),
                  pl.BlockSpec((None, 1, d), lambda i, j: (mod_row(i), 0, sh_chunk)),
                  pl.BlockSpec((None, 1, d), lambda i, j: (mod_row(i), 0, sc_chunk)),
                  pl.BlockSpec((d, tn), lambda i, j: (0, j))],
        out_specs=pl.BlockSpec((tm, tn), lambda i, j: (i, j)),
        out_shape=jax.ShapeDtypeStruct((m, n), BF16),
        scratch_shapes=[pltpu.VMEM((tm, d), BF16)],
        compiler_params=_params(("parallel", "arbitrary")),
        name="lnmod_matmul",
    )(x2, mod3, mod3, w)


class _ModRow:
    def __init__(self, tm, seq=None, fixed=None):
        self.tm, self.seq, self.fixed = tm, seq, fixed

    def __call__(self, i):
        if self.fixed is not None:
            return self.fixed
        return (i * self.tm) // self.seq


def _dwconv_rows(x, w_ref):
    n = x.shape[0]
    k = w_ref.shape[0]
    row = lax.broadcasted_iota(jnp.int32, x.shape, 0)
    acc = x * w_ref[k // 2:k // 2 + 1, :]
    for t in range(k):
        off = t - k // 2
        if off == 0:
            continue
        shifted = pltpu.roll(x, (-off) % n, 0)
        valid = (row + off >= 0) & (row + off < n)
        acc = acc + jnp.where(valid, shifted, 0.0) * w_ref[t:t + 1, :]
    return acc


def _conv_silu_kernel(x_ref, w_ref, b_ref, o_ref):
    y = _dwconv_rows(x_ref[...].astype(F32), w_ref) + b_ref[...]
    o_ref[...] = _silu(y).astype(o_ref.dtype)


def _conv_silu(src3, col0, width, conv_w, conv_b):
    b, n, _ = src3.shape
    tc = LANE
    c0 = col0 // tc
    kk = conv_w.shape[0]
    return pl.pallas_call(
        _conv_silu_kernel,
        grid=(b, width // tc),
        in_specs=[pl.BlockSpec((None, n, tc), lambda bi, j: (bi, 0, c0 + j)),
                  pl.BlockSpec((kk, tc), lambda bi, j: (0, j)),
                  pl.BlockSpec((1, tc), lambda bi, j: (0, j))],
        out_specs=pl.BlockSpec((None, n, tc), lambda bi, j: (bi, 0, j)),
        out_shape=jax.ShapeDtypeStruct((b, n, width), BF16),
        compiler_params=_params(("parallel", "parallel")),
        name="conv_silu",
    )(src3, conv_w, conv_b.reshape(1, width))


def _softplus(v):
    return jnp.maximum(v, 0.0) + jnp.log(1.0 + jnp.exp(-jnp.abs(v)))


def _ssd_direction(u_ref, dt_ref, a_row, dtb_row, s_ref, y_ref, d, reverse):
    q = SSD_CHUNK
    dt_all = _softplus(dt_ref[...].astype(F32) + dtb_row)
    cum = dt_all * a_row
    row = lax.broadcasted_iota(jnp.int32, (q, LANE), 0)
    s = 1
    while s < q:
        if reverse:
            cum = cum + jnp.where(row < q - s, pltpu.roll(cum, q - s, 0), 0.0)
        else:
            cum = cum + jnp.where(row >= s, pltpu.roll(cum, s, 0), 0.0)
        s *= 2
    tot = cum[0:1, :] if reverse else cum[q - 1:q, :]
    cum_t = cum.T
    dt_t = dt_all.T
    e_cum = jnp.exp(cum)
    w_end = jnp.exp(tot - cum) * dt_all
    e_tot = jnp.exp(tot)

    ri = lax.broadcasted_iota(jnp.int32, (q, q), 0)
    ci = lax.broadcasted_iota(jnp.int32, (q, q), 1)
    tri = (ri <= ci) if reverse else (ri >= ci)
    lane = lax.broadcasted_iota(jnp.int32, (q, LANE), 1)
    lo = lane < HEAD_DIM
    lane_row = lax.broadcasted_iota(jnp.int32, (1, LANE), 1)

    def pair_cols(v, c0, c1):
        return jnp.where(lo, jnp.broadcast_to(v[:, c0:c0 + 1], (q, LANE)),
                         jnp.broadcast_to(v[:, c1:c1 + 1], (q, LANE)))

    heads_per_group = SSD_HEADS // SSD_GROUPS
    for g in range(SSD_GROUPS):
        b0 = SSD_D_INNER + g * SSD_STATE
        c0 = SSD_D_INNER + SSD_GROUPS * SSD_STATE + g * SSD_STATE
        bg = u_ref[:, b0:b0 + SSD_STATE]
        cg = u_ref[:, c0:c0 + SSD_STATE]
        cb = lax.dot_general(cg, bg, (((1,), (1,)), ((), ())), preferred_element_type=F32)
        bg_t = bg.astype(F32).T.astype(BF16)
        for hp in range(heads_per_group // 2):
            h0 = g * heads_per_group + 2 * hp
            col0, col1 = d * SSD_HEADS + h0, d * SSD_HEADS + h0 + 1
            l0 = h0 * HEAD_DIM
            xp = u_ref[:, l0:l0 + LANE]
            ws = []
            for col in (col0, col1):
                seg = cum[:, col:col + 1] - cum_t[col:col + 1, :]
                dec = jnp.exp(jnp.where(tri, seg, NEG))
                ws.append((dec * cb * dt_t[col:col + 1, :]).astype(BF16))
            w_pair = jnp.concatenate(ws, axis=1)
            zero = jnp.zeros_like(xp)
            x_bd = jnp.concatenate([jnp.where(lo, xp, zero), jnp.where(lo, zero, xp)], axis=0)
            y_diag = jnp.dot(w_pair, x_bd, preferred_element_type=F32)
            st = s_ref[d, :, l0:l0 + LANE]
            y_off = jnp.dot(cg, st.astype(BF16), preferred_element_type=F32) * pair_cols(e_cum, col0, col1)
            y_ref[:, l0:l0 + LANE] = y_diag + y_off
            xw = (xp.astype(F32) * pair_cols(w_end, col0, col1)).astype(BF16)
            upd = jnp.dot(bg_t, xw, preferred_element_type=F32)
            tot_pair = jnp.where(lane_row < HEAD_DIM,
                                 jnp.broadcast_to(e_tot[:, col0:col0 + 1], (1, LANE)),
                                 jnp.broadcast_to(e_tot[:, col1:col1 + 1], (1, LANE)))
            s_ref[d, :, l0:l0 + LANE] = st * tot_pair + upd


def _ssd_kernel(uf_ref, ub_ref, dtf_ref, dtb_ref, alog_ref, dtbias_ref, h0_ref,
                yf_ref, yb_ref, hT_ref, s_ref):
    t = pl.program_id(1)

    @pl.when(t == 0)
    def _():
        s_ref[...] = h0_ref[...]

    a_row = -jnp.exp(alog_ref[...])
    dtb_row = dtbias_ref[...]
    _ssd_direction(uf_ref, dtf_ref, a_row, dtb_row, s_ref, yf_ref, 0, False)
    _ssd_direction(ub_ref, dtb_ref, a_row, dtb_row, s_ref, yb_ref, 1, True)

    @pl.when(t == pl.num_programs(1) - 1)
    def _():
        hT_ref[...] = s_ref[...]


def _ssd_scan(u3, p3, dt_col0, alog_row, dtbias_row, h0):
    b, n, _ = u3.shape
    q = SSD_CHUNK
    nt = n // q
    dtb = dt_col0 // LANE
    hp = SSD_D_INNER
    return pl.pallas_call(
        _ssd_kernel,
        grid=(b, nt),
        in_specs=[pl.BlockSpec((None, q, SSD_XBC), lambda bi, t: (bi, t, 0)),
                  pl.BlockSpec((None, q, SSD_XBC), lambda bi, t: (bi, nt - 1 - t, 0)),
                  pl.BlockSpec((None, q, LANE), lambda bi, t: (bi, t, dtb)),
                  pl.BlockSpec((None, q, LANE), lambda bi, t: (bi, nt - 1 - t, dtb)),
                  pl.BlockSpec((1, LANE), lambda bi, t: (0, 0)),
                  pl.BlockSpec((1, LANE), lambda bi, t: (0, 0)),
                  pl.BlockSpec((None, 2, SSD_STATE, hp), lambda bi, t: (bi, 0, 0, 0))],
        out_specs=[pl.BlockSpec((None, q, hp), lambda bi, t: (bi, t, 0)),
                   pl.BlockSpec((None, q, hp), lambda bi, t: (bi, nt - 1 - t, 0)),
                   pl.BlockSpec((None, 2, SSD_STATE, hp), lambda bi, t: (bi, 0, 0, 0))],
        out_shape=[jax.ShapeDtypeStruct((b, n, hp), F32),
                   jax.ShapeDtypeStruct((b, n, hp), F32),
                   jax.ShapeDtypeStruct((b, 2, SSD_STATE, hp), F32)],
        scratch_shapes=[pltpu.VMEM((2, SSD_STATE, hp), F32)],
        compiler_params=_params(("parallel", "arbitrary")),
        name="ssd_scan",
    )(u3, u3, p3, p3, alog_row, dtbias_row, h0)


def _ssd_finish_kernel(yf_ref, yb_ref, x_ref, z_ref, dskip_ref, g_ref, o_ref):
    y = dskip_ref[...] * x_ref[...].astype(F32) + yf_ref[...] + yb_ref[...]
    y = y * _silu(z_ref[...].astype(F32))
    r = lax.rsqrt(jnp.mean(y * y, axis=-1, keepdims=True) + LN_EPS)
    o_ref[...] = (y * r * g_ref[...]).astype(o_ref.dtype)


def _ssd_finish(yf2, yb2, u2, p2, z_col0, dskip_row, g_row):
    m, w = yf2.shape
    tm = _pick(m, (1024, 512, 256, 128))
    zb = z_col0 // w
    row = lambda i: (i, 0)
    return pl.pallas_call(
        _ssd_finish_kernel,
        grid=(m // tm,),
        in_specs=[pl.BlockSpec((tm, w), row), pl.BlockSpec((tm, w), row),
                  pl.BlockSpec((tm, w), row),
                  pl.BlockSpec((tm, w), lambda i: (i, zb)),
                  pl.BlockSpec((1, w), lambda i: (0, 0)),
                  pl.BlockSpec((1, w), lambda i: (0, 0))],
        out_specs=pl.BlockSpec((tm, w), row),
        out_shape=jax.ShapeDtypeStruct((m, w), BF16),
        compiler_params=_params(("parallel",)),
        name="ssd_finish",
    )(yf2, yb2, u2, p2, dskip_row, g_row)


def _block_diag_rows(kv):
    lane = lax.broadcasted_iota(jnp.int32, kv.shape, 1)
    lo = lane < HEAD_DIM
    zero = jnp.zeros_like(kv)
    return jnp.concatenate([jnp.where(lo, kv, zero), jnp.where(lo, zero, kv)], axis=0)


def _dup_group(kv, g):
    lane = lax.broadcasted_iota(jnp.int32, kv.shape, 1)
    rolled = pltpu.roll(kv, HEAD_DIM, 1)
    return jnp.where(lane // HEAD_DIM == g, kv, rolled)


def _qk(q, kbd):
    return lax.dot_general(q, kbd, (((1,), (1,)), ((), ())), preferred_element_type=F32)


def _pair_softmax_pv(s_w, s_c, bias, vbd, vcbd, sink, out_dtype):
    nk = s_w.shape[1] // 2
    nc = s_c.shape[1] // 2
    tq = s_w.shape[0]
    pws, pcs, invs = [], [], []
    for h in range(2):
        sw = s_w[:, h * nk:(h + 1) * nk] + bias[h]
        sc = s_c[:, h * nc:(h + 1) * nc]
        m = jnp.maximum(jnp.max(sw, axis=-1, keepdims=True), jnp.max(sc, axis=-1, keepdims=True))
        if sink is not None:
            m = jnp.maximum(m, sink[h])
        pw = jnp.exp(sw - m)
        pc = jnp.exp(sc - m)
        l = jnp.sum(pw, axis=-1, keepdims=True) + jnp.sum(pc, axis=-1, keepdims=True)
        if sink is not None:
            l = l + jnp.exp(sink[h] - m)
        pws.append(pw.astype(BF16))
        pcs.append(pc.astype(BF16))
        invs.append(1.0 / l)
    o = jnp.dot(jnp.concatenate(pws, axis=1), vbd, preferred_element_type=F32)
    o = o + jnp.dot(jnp.concatenate(pcs, axis=1), vcbd, preferred_element_type=F32)
    lane = lax.broadcasted_iota(jnp.int32, (tq, LANE), 1)
    inv = jnp.where(lane < HEAD_DIM, jnp.broadcast_to(invs[0], (tq, LANE)),
                    jnp.broadcast_to(invs[1], (tq, LANE)))
    return (o * inv).astype(out_dtype)


def _na_kernel(q_ref, k_ref, v_ref, kc_ref, vc_ref, bias_ref, o_ref, *, rows):
    qb = pl.program_id(2)
    nk = NA_K_ROWS * GRID_W
    ks = jnp.clip(NA_Q_ROWS * qb - (NA_K_ROWS - NA_Q_ROWS) // 2, 0, rows - NA_K_ROWS) * GRID_W
    ks = pl.multiple_of(ks, GRID_W)
    for p in range(q_ref.shape[1] // LANE):
        cs = slice(p * LANE, (p + 1) * LANE)
        q = q_ref[:, cs] * (HEAD_DIM ** -0.5)
        kbd = _block_diag_rows(k_ref[pl.ds(ks, nk), cs])
        vbd = _block_diag_rows(v_ref[pl.ds(ks, nk), cs])
        kcbd = _block_diag_rows(kc_ref[:, cs])
        vcbd = _block_diag_rows(vc_ref[:, cs])
        s_w = _qk(q, kbd)
        s_c = _qk(q, kcbd)
        o_ref[:, cs] = _pair_softmax_pv(s_w, s_c, [bias_ref[2 * p], bias_ref[2 * p + 1]], vbd, vcbd,
                                        None, o_ref.dtype)


def _na_key_start(qb, rows):
    return min(max(NA_Q_ROWS * qb - (NA_K_ROWS - NA_Q_ROWS) // 2, 0), rows - NA_K_ROWS)


def _na_bias_tables(rpb, rows):
    nqb = rows // NA_Q_ROWS
    variants = [0, 1 if nqb > 2 else 0, nqb - 1]
    nh = rpb.shape[0]
    w = GRID_W
    pad = w - NA_WIN_COLS
    rp = jnp.pad(rpb.astype(F32), ((0, 0), (0, 0), (pad, pad)))
    toep = jnp.stack([rp[:, :, w - 1 - qc:2 * w - 1 - qc] for qc in range(w)], axis=2)
    qc, kc = np.arange(w)[:, None], np.arange(w)[None, :]
    ws = np.clip(qc - NA_WIN_COLS // 2, 0, w - NA_WIN_COLS)
    toep = jnp.where(jnp.asarray((kc >= ws) & (kc < ws + NA_WIN_COLS)), toep, NEG)
    masked = jnp.full((nh, w, w), NEG, F32)
    tabs = []
    for qb in variants:
        block_rows = []
        for qr_l in range(NA_Q_ROWS):
            qr = NA_Q_ROWS * qb + qr_l
            rs = min(max(qr - NA_WIN_ROWS // 2, 0), rows - NA_WIN_ROWS)
            blocks = []
            for kr_l in range(NA_K_ROWS):
                kr = _na_key_start(qb, rows) + kr_l
                inside = rs <= kr < rs + NA_WIN_ROWS
                blocks.append(toep[:, kr - qr + NA_WIN_ROWS - 1] if inside else masked)
            block_rows.append(jnp.concatenate(blocks, axis=-1))
        tabs.append(jnp.concatenate(block_rows, axis=-2))
    return jnp.stack(tabs, axis=0)


def _na_attention(p3, pc3, q_col0, k_col0, v_col0, ck_col0, cv_col0, bias_tab):
    b, n, _ = p3.shape
    nc = pc3.shape[1]
    rows = n // GRID_W
    tq = NA_Q_ROWS * GRID_W
    nk = NA_K_ROWS * GRID_W
    nqb = n // tq
    wb = 4 * LANE
    qc, kc, vc = q_col0 // wb, k_col0 // wb, v_col0 // wb
    ckc, cvc = ck_col0 // wb, cv_col0 // wb
    ngrp = 1

    def variant(qb):
        return jnp.where(qb == 0, 0, jnp.where(qb == nqb - 1, 2, 1))

    return pl.pallas_call(
        functools.partial(_na_kernel, rows=rows),
        grid=(b, ngrp, nqb),
        in_specs=[pl.BlockSpec((None, tq, wb), lambda bi, h, i: (bi, i, qc + h)),
                  pl.BlockSpec((None, n, wb), lambda bi, h, i: (bi, 0, kc + h)),
                  pl.BlockSpec((None, n, wb), lambda bi, h, i: (bi, 0, vc + h)),
                  pl.BlockSpec((None, nc, wb), lambda bi, h, i: (bi, 0, ckc + h)),
                  pl.BlockSpec((None, nc, wb), lambda bi, h, i: (bi, 0, cvc + h)),
                  pl.BlockSpec((None, wb // HEAD_DIM, tq, nk), lambda bi, h, i: (variant(i), h, 0, 0))],
        out_specs=pl.BlockSpec((None, tq, wb), lambda bi, h, i: (bi, i, h)),
        out_shape=jax.ShapeDtypeStruct((b, n, ngrp * wb), BF16),
        compiler_params=_params(("parallel", "parallel", "arbitrary")),
        name="na_attention",
    )(p3, p3, p3, pc3, pc3, bias_tab)


def _rope(x, cos, sin_signed):
    lane = lax.broadcasted_iota(jnp.int32, x.shape, 1)
    first = (lane % 32) < 16
    partner = jnp.where(first, pltpu.roll(x, LANE - 16, 1), pltpu.roll(x, 16, 1))
    return x * cos + partner * sin_signed


def _sink_pair(sink_ref):
    s = sink_ref[...]
    return [s[:, 0:1], s[:, HEAD_DIM:HEAD_DIM + 1]]


def _swa_kernel(q_ref, k_ref, v_ref, kc_ref, vc_ref, cos_ref, sin_ref, sink_ref, o_ref, *, n):
    qb = pl.program_id(1)
    q0 = pl.multiple_of(qb * SWA_Q, SWA_Q)
    ks = pl.multiple_of(jnp.clip(qb * SWA_Q - SWA_WINDOW, 0, n - SWA_K), SWA_WINDOW)
    cos_q, sin_q = cos_ref[pl.ds(q0, SWA_Q), :], sin_ref[pl.ds(q0, SWA_Q), :]
    kk = _rope(k_ref[pl.ds(ks, SWA_K), :].astype(F32), cos_ref[pl.ds(ks, SWA_K), :],
               sin_ref[pl.ds(ks, SWA_K), :])
    vv = v_ref[pl.ds(ks, SWA_K), :].astype(F32)
    kc, vc = kc_ref[...].astype(F32), vc_ref[...].astype(F32)
    qpos = q0 + lax.broadcasted_iota(jnp.int32, (SWA_Q, SWA_K), 0)
    kpos = ks + lax.broadcasted_iota(jnp.int32, (SWA_Q, SWA_K), 1)
    bias = jnp.where(jnp.abs(kpos - qpos) <= SWA_WINDOW, 0.0, NEG)
    pairs_per_group = 2
    for p in range(q_ref.shape[1] // LANE):
        g = p // pairs_per_group
        if p % pairs_per_group == 0:
            kbd = _block_diag_rows(_dup_group(kk, g).astype(BF16))
            vbd = _block_diag_rows(_dup_group(vv, g).astype(BF16))
            kcbd = _block_diag_rows(_dup_group(kc, g).astype(BF16))
            vcbd = _block_diag_rows(_dup_group(vc, g).astype(BF16))
        cs = slice(p * LANE, (p + 1) * LANE)
        q = _rope(q_ref[:, cs].astype(F32), cos_q, sin_q)
        q = (q * (HEAD_DIM ** -0.5)).astype(BF16)
        s_w = _qk(q, kbd)
        s_c = _qk(q, kcbd)
        o_ref[:, cs] = _pair_softmax_pv(s_w, s_c, [bias, bias], vbd, vcbd, _sink_pair(sink_ref.at[p]),
                                        o_ref.dtype)


def _swa_attention(p3, pc3, q_col0, kv_col0, ckv_col0, cos_tab, sin_tab, sink_rows):
    b, n, _ = p3.shape
    nc = pc3.shape[1]
    nqb = n // SWA_Q
    wb = 4 * LANE
    qc, kc, ckc = q_col0 // wb, kv_col0 // LANE, ckv_col0 // LANE
    return pl.pallas_call(
        functools.partial(_swa_kernel, n=n),
        grid=(b, nqb),
        in_specs=[pl.BlockSpec((None, SWA_Q, wb), lambda bi, i: (bi, i, qc)),
                  pl.BlockSpec((None, n, LANE), lambda bi, i: (bi, 0, kc)),
                  pl.BlockSpec((None, n, LANE), lambda bi, i: (bi, 0, kc + 1)),
                  pl.BlockSpec((None, nc, LANE), lambda bi, i: (bi, 0, ckc)),
                  pl.BlockSpec((None, nc, LANE), lambda bi, i: (bi, 0, ckc + 1)),
                  pl.BlockSpec((n, LANE), lambda bi, i: (0, 0)),
                  pl.BlockSpec((n, LANE), lambda bi, i: (0, 0)),
                  pl.BlockSpec((wb // LANE, 1, LANE), lambda bi, i: (0, 0, 0))],
        out_specs=pl.BlockSpec((None, SWA_Q, wb), lambda bi, i: (bi, i, 0)),
        out_shape=jax.ShapeDtypeStruct((b, n, wb), BF16),
        compiler_params=_params(("parallel", "arbitrary")),
        name="swa_attention",
    )(p3, p3, p3, pc3, pc3, cos_tab, sin_tab, sink_rows)


def _ctx_attn_kernel(q_ref, k_ref, v_ref, sink_ref, o_ref, *, grouped):
    hp = pl.program_id(1)
    q = q_ref[...] * (HEAD_DIM ** -0.5)
    k, v = k_ref[...], v_ref[...]
    if grouped:
        g = hp // 2
        k = _dup_group(k.astype(F32), g).astype(BF16)
        v = _dup_group(v.astype(F32), g).astype(BF16)
    kbd, vbd = _block_diag_rows(k), _block_diag_rows(v)
    s = _qk(q, kbd)
    t = q.shape[0]
    sink = _sink_pair(sink_ref) if grouped else None
    ps, invs = [], []
    for h in range(2):
        sh = s[:, h * t:(h + 1) * t]
        m = jnp.max(sh, axis=-1, keepdims=True)
        if sink is not None:
            m = jnp.maximum(m, sink[h])
        p = jnp.exp(sh - m)
        l = jnp.sum(p, axis=-1, keepdims=True)
        if sink is not None:
            l = l + jnp.exp(sink[h] - m)
        ps.append(p.astype(BF16))
        invs.append(1.0 / l)
    o = jnp.dot(jnp.concatenate(ps, axis=1), vbd, preferred_element_type=F32)
    lane = lax.broadcasted_iota(jnp.int32, (t, LANE), 1)
    inv = jnp.where(lane < HEAD_DIM, jnp.broadcast_to(invs[0], (t, LANE)),
                    jnp.broadcast_to(invs[1], (t, LANE)))
    o_ref[...] = (o * inv).astype(o_ref.dtype)


def _ctx_attention(pc3, q_col0, k_col0, v_col0, sink_rows, grouped):
    b, nc, _ = pc3.shape
    qc, kc, vc = q_col0 // LANE, k_col0 // LANE, v_col0 // LANE
    npair = 4
    kv_blk = (lambda h: 0) if grouped else (lambda h: h)
    return pl.pallas_call(
        functools.partial(_ctx_attn_kernel, grouped=grouped),
        grid=(b, npair),
        in_specs=[pl.BlockSpec((None, nc, LANE), lambda bi, h: (bi, 0, qc + h)),
                  pl.BlockSpec((None, nc, LANE), lambda bi, h: (bi, 0, kc + kv_blk(h))),
                  pl.BlockSpec((None, nc, LANE), lambda bi, h: (bi, 0, vc + kv_blk(h))),
                  pl.BlockSpec((None, 1, LANE), lambda bi, h: (h, 0, 0))],
        out_specs=pl.BlockSpec((None, nc, LANE), lambda bi, h: (bi, 0, h)),
        out_shape=jax.ShapeDtypeStruct((b, nc, npair * LANE), BF16),
        compiler_params=_params(("parallel", "parallel")),
        name="ctx_attention",
    )(pc3, pc3, pc3, sink_rows)


def _fn_channel_kernel(x_ref, cs_ref, o_ref):
    w = x_ref.shape[1]
    ngroups = w // FN_GROUP_DIM
    for g in range(ngroups):
        xg = x_ref[:, g * FN_GROUP_DIM:(g + 1) * FN_GROUP_DIM]
        ab = jnp.dot(xg, cs_ref[...], preferred_element_type=F32)
        o_ref[:, g * FN_GROUP_DIM:(g + 1) * FN_GROUP_DIM] = ab[:, :FN_GROUP_DIM].astype(o_ref.dtype)
        o_ref[:, w + g * FN_GROUP_DIM:w + (g + 1) * FN_GROUP_DIM] = ab[:, FN_GROUP_DIM:].astype(o_ref.dtype)


def _fn_channel(p2, col0, width):
    m = p2.shape[0]
    tm = _pick(m, (1024, 512, 256, 128))
    k = np.arange(FN_GROUP_DIM)
    ang = 2.0 * np.pi * ((k[:, None] * k[None, :]) % FN_GROUP_DIM) / FN_GROUP_DIM
    cs = jnp.asarray(np.concatenate([np.cos(ang), np.sin(ang)], axis=1), BF16)
    cb = col0 // width
    return pl.pallas_call(
        _fn_channel_kernel,
        grid=(m // tm,),
        in_specs=[pl.BlockSpec((tm, width), lambda i: (i, cb)),
                  pl.BlockSpec((FN_GROUP_DIM, 2 * FN_GROUP_DIM), lambda i: (0, 0))],
        out_specs=pl.BlockSpec((tm, 2 * width), lambda i: (i, 0)),
        out_shape=jax.ShapeDtypeStruct((m, 2 * width), BF16),
        compiler_params=_params(("parallel",)),
        name="fn_channel",
    )(p2, cs)


def _fn_position_kernel(ac_ref, as_ref, bc_ref, bs_ref, ab_ref, o_ref, c_ref, s_ref, *, scale):
    @pl.when(pl.program_id(1) == 0)
    def _():
        for j in range(ac_ref.shape[0]):
            ca, sa = ac_ref[j:j + 1, :], as_ref[j:j + 1, :]
            cb, sb = bc_ref[...], bs_ref[...]
            c_ref[j * DFT_ROWS:(j + 1) * DFT_ROWS, :] = (ca * cb - sa * sb).astype(BF16)
            s_ref[j * DFT_ROWS:(j + 1) * DFT_ROWS, :] = (sa * cb + ca * sb).astype(BF16)

    w = o_ref.shape[1]
    y = jnp.dot(c_ref[...], ab_ref[:, :w], preferred_element_type=F32)
    y = y - jnp.dot(s_ref[...], ab_ref[:, w:], preferred_element_type=F32)
    o_ref[...] = (y * scale).astype(o_ref.dtype)


def _fn_position(ab3):
    b, n, w2 = ab3.shape
    w = w2 // 2
    tm = _pick(n, (512, 256, 128, 64))
    jc = tm // DFT_ROWS
    n1 = n // DFT_ROWS
    k = np.arange(n, dtype=np.int64)
    j1 = np.arange(n1, dtype=np.int64)
    j2 = np.arange(DFT_ROWS, dtype=np.int64)
    ang_a = 2.0 * np.pi * ((j1[:, None] * DFT_ROWS * k[None, :]) % n) / n
    ang_b = 2.0 * np.pi * ((j2[:, None] * k[None, :]) % n) / n
    ac, as_ = jnp.asarray(np.cos(ang_a), F32), jnp.asarray(np.sin(ang_a), F32)
    bc, bs = jnp.asarray(np.cos(ang_b), F32), jnp.asarray(np.sin(ang_b), F32)
    scale = 1.0 / math.sqrt(n * FN_GROUP_DIM)
    if n1 % 8 != 0 and jc != n1:
        raise ValueError("unsupported sequence length for the position DFT tiling")
    return pl.pallas_call(
        functools.partial(_fn_position_kernel, scale=scale),
        grid=(n // tm, b),
        in_specs=[pl.BlockSpec((jc, n), lambda i, bi: (i, 0)),
                  pl.BlockSpec((jc, n), lambda i, bi: (i, 0)),
                  pl.BlockSpec((DFT_ROWS, n), lambda i, bi: (0, 0)),
                  pl.BlockSpec((DFT_ROWS, n), lambda i, bi: (0, 0)),
                  pl.BlockSpec((None, n, w2), lambda i, bi: (bi, 0, 0))],
        out_specs=pl.BlockSpec((None, tm, w), lambda i, bi: (bi, i, 0)),
        out_shape=jax.ShapeDtypeStruct((b, n, w), BF16),
        scratch_shapes=[pltpu.VMEM((tm, n), BF16), pltpu.VMEM((tm, n), BF16)],
        compiler_params=_params(("parallel", "arbitrary")),
        name="fn_position",
    )(ac, as_, bc, bs, ab3)


def _residual_ln(o_ref, x_ref, gate_ref, lng_ref, lnb_ref, alpha):
    v = alpha * x_ref[...] + gate_ref[...] * o_ref[...]
    o_ref[...] = _layer_norm(v) * lng_ref[...] + lnb_ref[...]


def _merge_out_kernel(y0, y1, y2, y3, g0, g1, g2, g3, wb_ref, bg_ref, wo_ref, x_ref, gate_ref, lng_ref,
                      lnb_ref, o_ref, *, alpha):
    j = pl.program_id(1)

    @pl.when(j == 0)
    def _():
        o_ref[...] = jnp.zeros_like(o_ref)

    merged = None
    for i, (y, g) in enumerate(((y0, g0), (y1, g1), (y2, g2), (y3, g3))):
        gate = _sigmoid(g[...].astype(F32) + bg_ref[i])
        term = gate * jnp.dot(y[...], wb_ref[i], preferred_element_type=F32)
        merged = term if merged is None else merged + term
    o_ref[...] += jnp.dot(merged.astype(BF16), wo_ref[...], preferred_element_type=F32)

    @pl.when(j == pl.num_programs(1) - 1)
    def _():
        _residual_ln(o_ref, x_ref, gate_ref, lng_ref, lnb_ref, alpha)


def _merge_out(ys, p2, gate_col0, wb, bg3, w_out, x2, mod3, mod_row, gate_chunk, ln_g, ln_b, alpha):
    m = p2.shape[0]
    d = wb.shape[2]
    tm = mod_row.tm
    tn = _pick(d, (1024, 512, 256, 128))
    gb = gate_col0 // tn
    per = d // tn
    bw = ys[0].shape[1]
    y_specs = [pl.BlockSpec((tm, bw), lambda i, j: (i, 0)) for _ in range(N_BRANCH)]
    g_specs = [pl.BlockSpec((tm, tn), functools.partial(lambda i, j, br: (i, gb + br * per + j), br=br))
               for br in range(N_BRANCH)]
    return pl.pallas_call(
        functools.partial(_merge_out_kernel, alpha=alpha),
        grid=(m // tm, per),
        in_specs=y_specs + g_specs + [
            pl.BlockSpec((N_BRANCH, bw, tn), lambda i, j: (0, 0, j)),
            pl.BlockSpec((N_BRANCH, 1, tn), lambda i, j: (0, 0, j)),
            pl.BlockSpec((tn, d), lambda i, j: (j, 0)),
            pl.BlockSpec((tm, d), lambda i, j: (i, 0)),
            pl.BlockSpec((None, 1, d), lambda i, j: (mod_row(i), 0, gate_chunk)),
            pl.BlockSpec((1, d), lambda i, j: (0, 0)),
            pl.BlockSpec((1, d), lambda i, j: (0, 0))],
        out_specs=pl.BlockSpec((tm, d), lambda i, j: (i, 0)),
        out_shape=jax.ShapeDtypeStruct((m, d), F32),
        compiler_params=_params(("parallel", "arbitrary")),
        name="merge_out",
    )(*ys, p2, p2, p2, p2, wb, bg3, w_out, x2, mod3, ln_g.reshape(1, d), ln_b.reshape(1, d))


def _mm_resln_kernel(a_ref, w_ref, x_ref, gate_ref, lng_ref, lnb_ref, o_ref, *, alpha):
    k = pl.program_id(1)

    @pl.when(k == 0)
    def _():
        o_ref[...] = jnp.zeros_like(o_ref)

    o_ref[...] += jnp.dot(a_ref[...], w_ref[...], preferred_element_type=F32)

    @pl.when(k == pl.num_programs(1) - 1)
    def _():
        _residual_ln(o_ref, x_ref, gate_ref, lng_ref, lnb_ref, alpha)


def _mm_resln(a2, w, x2, mod3, mod_row, gate_chunk, ln_g, ln_b, alpha, tk):
    m, kdim = a2.shape
    d = w.shape[1]
    tm = mod_row.tm
    return pl.pallas_call(
        functools.partial(_mm_resln_kernel, alpha=alpha),
        grid=(m // tm, kdim // tk),
        in_specs=[pl.BlockSpec((tm, tk), lambda i, k: (i, k)),
                  pl.BlockSpec((tk, d), lambda i, k: (k, 0)),
                  pl.BlockSpec((tm, d), lambda i, k: (i, 0)),
                  pl.BlockSpec((None, 1, d), lambda i, k: (mod_row(i), 0, gate_chunk)),
                  pl.BlockSpec((1, d), lambda i, k: (0, 0)),
                  pl.BlockSpec((1, d), lambda i, k: (0, 0))],
        out_specs=pl.BlockSpec((tm, d), lambda i, k: (i, 0)),
        out_shape=jax.ShapeDtypeStruct((m, d), F32),
        compiler_params=_params(("parallel", "arbitrary")),
        name="matmul_residual_ln",
    )(a2, w, x2, mod3, ln_g.reshape(1, d), ln_b.reshape(1, d))


HALO_ROWS = 8


def _ffn_up_kernel(x_ref, xp_ref, xn_ref, sh_ref, sc_ref, wg_ref, wu_ref, cw_ref, cb_ref, o_ref, h_ref,
                   *, seq):
    i = pl.program_id(0)
    tm = x_ref.shape[0]

    @pl.when(pl.program_id(1) == 0)
    def _():
        scale, shift = 1.0 + sc_ref[...], sh_ref[...]
        h_ref[0:tm, :] = (_layer_norm(x_ref[...]) * scale + shift).astype(BF16)
        halo = jnp.concatenate([xp_ref[...], xn_ref[...]], axis=0)
        h_ref[tm:tm + 2 * HALO_ROWS, :] = (_layer_norm(halo) * scale + shift).astype(BF16)

    g_all = jnp.dot(h_ref[...], wg_ref[...], preferred_element_type=F32)
    u = jnp.dot(h_ref[0:tm, :], wu_ref[...], preferred_element_type=F32)
    g = g_all[0:tm, :]
    has_prev = jnp.where((i * tm) % seq == 0, 0.0, 1.0)
    has_next = jnp.where(((i + 1) * tm) % seq == 0, 0.0, 1.0)
    g_prev = g_all[tm + HALO_ROWS - 1:tm + HALO_ROWS, :] * has_prev
    g_next = g_all[tm + HALO_ROWS:tm + HALO_ROWS + 1, :] * has_next
    row = lax.broadcasted_iota(jnp.int32, g.shape, 0)
    above = jnp.where(row == 0, g_prev, pltpu.roll(g, 1, 0))
    below = jnp.where(row == tm - 1, g_next, pltpu.roll(g, tm - 1, 0))
    y = above * cw_ref[0:1, :] + g * cw_ref[1:2, :] + below * cw_ref[2:3, :] + cb_ref[...]
    o_ref[...] = (_silu(y) * u).astype(o_ref.dtype)


def _ffn_up(x2, mod3, mod_row, sh_chunk, sc_chunk, w_up, conv_w, conv_b, tn, seq):
    m, d = x2.shape
    dff = w_up.shape[1] // 2
    tm = mod_row.tm
    if conv_w.shape[0] != 3 or seq % tm != 0 or tm % HALO_ROWS != 0:
        raise ValueError("unsupported ConvFFN tiling")
    ub = dff // tn
    hb = tm // HALO_ROWS
    last_hb = m // HALO_ROWS - 1
    return pl.pallas_call(
        functools.partial(_ffn_up_kernel, seq=seq),
        grid=(m // tm, dff // tn),
        in_specs=[pl.BlockSpec((tm, d), lambda i, j: (i, 0)),
                  pl.BlockSpec((HALO_ROWS, d), lambda i, j: (jnp.maximum(i * hb - 1, 0), 0)),
                  pl.BlockSpec((HALO_ROWS, d), lambda i, j: (jnp.minimum((i + 1) * hb, last_hb), 0)),
                  pl.BlockSpec((None, 1, d), lambda i, j: (mod_row(i), 0, sh_chunk)),
                  pl.BlockSpec((None, 1, d), lambda i, j: (mod_row(i), 0, sc_chunk)),
                  pl.BlockSpec((d, tn), lambda i, j: (0, j)),
                  pl.BlockSpec((d, tn), lambda i, j: (0, ub + j)),
                  pl.BlockSpec((3, tn), lambda i, j: (0, j)),
                  pl.BlockSpec((1, tn), lambda i, j: (0, j))],
        out_specs=pl.BlockSpec((tm, tn), lambda i, j: (i, j)),
        out_shape=jax.ShapeDtypeStruct((m, dff), BF16),
        scratch_shapes=[pltpu.VMEM((tm + 2 * HALO_ROWS, d), BF16)],
        compiler_params=_params(("parallel", "arbitrary")),
        name="ffn_up",
    )(x2, x2, x2, mod3, mod3, w_up, w_up, conv_w, conv_b.reshape(1, dff))


def _rope_tables(n):
    t = jnp.arange(n)
    rows = (t // GRID_W).astype(F32)
    cols = (t % GRID_W).astype(F32)
    n_freq = HEAD_DIM // 4
    inv = ROPE_BASE ** (-jnp.arange(n_freq, dtype=F32) / n_freq)
    ar, ac = rows[:, None] * inv, cols[:, None] * inv
    cos = jnp.concatenate([jnp.cos(ar), jnp.cos(ar), jnp.cos(ac), jnp.cos(ac)], axis=1)
    sin = jnp.concatenate([-jnp.sin(ar), jnp.sin(ar), -jnp.sin(ac), jnp.sin(ac)], axis=1)
    return jnp.tile(cos, (1, 2)), jnp.tile(sin, (1, 2))


def _branch_params(conv_w, conv_b, a_log, dt_bias, d_skip, norm_g, rpb, sink, rows, cos_tab, sin_tab):
    pad = jnp.zeros((1, LANE - 2 * SSD_HEADS), F32)
    return dict(
        conv_w=conv_w, conv_b=conv_b,
        alog_row=jnp.concatenate([a_log.reshape(1, -1), pad], axis=1),
        dtb_row=jnp.concatenate([dt_bias.reshape(1, -1), pad], axis=1),
        dskip_row=jnp.repeat(d_skip, HEAD_DIM).reshape(1, SSD_D_INNER),
        g_row=norm_g.reshape(1, SSD_D_INNER),
        sink_rows=jnp.repeat(sink, HEAD_DIM).reshape(4, 1, LANE),
        bias_tab=_na_bias_tables(rpb, rows),
        cos_tab=cos_tab, sin_tab=sin_tab)


def _mixer_branches(p3, pc3, cols, ccols, prm, ctx_out):
    bsz, n, npad = p3.shape
    nctx = pc3.shape[1]
    p2 = p3.reshape(bsz * n, npad)
    pc2 = pc3.reshape(bsz * nctx, pc3.shape[2])
    flat = lambda t: t.reshape(-1, t.shape[-1])

    u3 = _conv_silu(p3, cols["xbc"], SSD_XBC, prm["conv_w"], prm["conv_b"])
    uc3 = _conv_silu(pc3, ccols["xbc"], SSD_XBC, prm["conv_w"], prm["conv_b"])
    h_zero = jnp.zeros((bsz, 2, SSD_STATE, SSD_D_INNER), F32)
    yfc, ybc, h_ctx = _ssd_scan(uc3, pc3, ccols["dt"], prm["alog_row"], prm["dtb_row"], h_zero)
    yf, yb, _ = _ssd_scan(u3, p3, cols["dt"], prm["alog_row"], prm["dtb_row"], h_ctx)
    y_ssd = _ssd_finish(flat(yf), flat(yb), flat(u3), p2, cols["z"], prm["dskip_row"], prm["g_row"])
    y_na = flat(_na_attention(p3, pc3, cols["nq"], cols["nk"], cols["nv"], ccols["nk"], ccols["nv"],
                              prm["bias_tab"]))
    y_swa = flat(_swa_attention(p3, pc3, cols["sq"], cols["skv"], ccols["skv"], prm["cos_tab"],
                                prm["sin_tab"], prm["sink_rows"]))
    y_fn = flat(_fn_position(_fn_channel(p2, cols["fn"], 512).reshape(bsz, n, -1)))
    ys = [y_ssd, y_na, y_swa, y_fn]
    if not ctx_out:
        return ys, None
    yc_ssd = _ssd_finish(flat(yfc), flat(ybc), flat(uc3), pc2, ccols["z"], prm["dskip_row"], prm["g_row"])
    yc_na = flat(_ctx_attention(pc3, ccols["nq"], ccols["nk"], ccols["nv"], prm["sink_rows"], False))
    yc_swa = flat(_ctx_attention(pc3, ccols["sq"], ccols["skv"], ccols["skv"] + LANE, prm["sink_rows"],
                                 True))
    yc_fn = flat(_fn_position(_fn_channel(pc2, ccols["fn"], 512).reshape(bsz, nctx, -1)))
    return ys, [yc_ssd, yc_na, yc_swa, yc_fn]


def kernel(x, c, ctx, c_ctx, w_ada, b_ada, w_in, b_gate, ssd_conv_w, ssd_conv_b, ssd_a_log,
           ssd_dt_bias, ssd_d, ssd_norm_g, na_rpb, swa_sink, w_branch, w_out, ln1_g, ln1_b,
           ln2_g, ln2_b, ffn_w_up, ffn_conv_w, ffn_conv_b, ffn_w_down):
    bsz, n, d = x.shape
    nctx = ctx.shape[1]
    depth = w_ada.shape[0]
    dff = ffn_w_down.shape[1]
    alpha = (2.0 * depth) ** 0.25
    rows = n // GRID_W

    g_col = 0
    z_col = N_BRANCH * d
    fn_col = z_col + 512
    sq_col = fn_col + 512
    nq_col = sq_col + 512
    nk_col = nq_col + 512
    nv_col = nk_col + 512
    xbc_col = nv_col + 512
    skv_col = xbc_col + SSD_XBC
    dt_col = skv_col + 256
    n_used = dt_col + LANE
    n_in = -(-n_used // 256) * 256
    tn_in = _pick(n_in, (1280, 1024, 768, 512, 256))
    cols = dict(z=z_col, fn=fn_col, sq=sq_col, nq=nq_col, nk=nk_col, nv=nv_col, xbc=xbc_col,
                skv=skv_col, dt=dt_col)

    o_z, o_xbc, o_dt = 0, 512, 512 + SSD_XBC
    o_na = o_dt + 2 * SSD_HEADS
    o_sq = o_na + 3 * 512
    o_skv = o_sq + 512
    o_fn = o_skv + 256
    o_g = o_fn + 512

    in_segments = [(o_g, N_BRANCH * d), (o_z, 512), (o_fn, 512), (o_sq, 512), (o_na, 3 * 512),
                   (o_xbc, SSD_XBC), (o_skv, 256), (o_dt, 2 * SSD_HEADS)]

    n_rows = -(-(bsz + 1) // 8) * 8
    cvec = jnp.concatenate([c, c_ctx[None], jnp.zeros((n_rows - bsz - 1, d), F32)], axis=0)
    mods = _ada_mods(cvec, w_ada, b_ada)

    lat_row = _ModRow(_pick(n, (1024, 512, 256)), seq=n)
    ctx_row = _ModRow(_pick(bsz * nctx, (1024, 512, 256)), fixed=bsz)
    lat_row_h = _ModRow(_pick(n, (512, 256)), seq=n)
    ctx_row_h = _ModRow(_pick(bsz * nctx, (512, 256)), fixed=bsz)
    ctx_row_seq = _ModRow(_pick(nctx, (1024, 512, 256)), fixed=bsz)
    tn_up = _pick(dff, (512, 256, 128))
    tk_down = _pick(dff, (512, 256, 128))

    cos_tab, sin_tab = _rope_tables(n)

    x2 = x.reshape(bsz * n, d)
    xc2 = ctx.reshape(bsz * nctx, d)
    for l in range(depth):
        ctx_out = l < depth - 1
        mod3 = mods[l].reshape(n_rows, 1, 6 * d)
        w_in_p = _stage_permuted(w_in, l, in_segments, n_in)
        wb = _stage_weight(w_branch.reshape(depth, N_BRANCH * BRANCH_WIDTH, d), l)
        wb = wb.reshape(N_BRANCH, BRANCH_WIDTH, d)
        bg3 = b_gate[l].reshape(N_BRANCH, 1, d)
        w_out_b = _stage_weight(w_out, l)
        w_up_b = _stage_weight(ffn_w_up, l)
        w_down_b = _stage_weight(ffn_w_down, l)
        prm = _branch_params(ssd_conv_w[l], ssd_conv_b[l], ssd_a_log[l], ssd_dt_bias[l], ssd_d[l],
                             ssd_norm_g[l], na_rpb[l], swa_sink[l], rows, cos_tab, sin_tab)

        p2 = _lnmod_matmul(x2, mod3, lat_row, 0, 1, w_in_p, tn_in)
        p3 = p2.reshape(bsz, n, n_in)
        if ctx_out or nk_col % tn_in != 0:
            w_ctx, ccols = w_in_p, cols
        else:
            w_ctx = w_in_p[:, nk_col:]
            ccols = {k: v - nk_col for k, v in cols.items() if v >= nk_col}
        pc2 = _lnmod_matmul(xc2, mod3, ctx_row, 0, 1, w_ctx, tn_in)
        pc3 = pc2.reshape(bsz, nctx, pc2.shape[1])

        ys, ycs = _mixer_branches(p3, pc3, cols, ccols, prm, ctx_out)

        x_mid = _merge_out(ys, p2, g_col, wb, bg3, w_out_b, x2, mod3, lat_row_h, 2, ln1_g[l], ln1_b[l],
                           alpha)
        act = _ffn_up(x_mid, mod3, lat_row, 3, 4, w_up_b, ffn_conv_w[l], ffn_conv_b[l], tn_up, n)
        x2 = _mm_resln(act, w_down_b, x_mid, mod3, lat_row, 5, ln2_g[l], ln2_b[l], alpha, tk_down)

        if ctx_out:
            xc_mid = _merge_out(ycs, pc2, g_col, wb, bg3, w_out_b, xc2, mod3, ctx_row_h, 2,
                                ln1_g[l], ln1_b[l], alpha)
            act_c = _ffn_up(xc_mid, mod3, ctx_row_seq, 3, 4, w_up_b, ffn_conv_w[l], ffn_conv_b[l],
                            tn_up, nctx)
            xc2 = _mm_resln(act_c, w_down_b, xc_mid, mod3, ctx_row, 5, ln2_g[l], ln2_b[l], alpha,
                            tk_down)

    return x2.reshape(bsz, n, d)
```

```python
import functools
import math

import numpy as np
import jax
import jax.numpy as jnp
from jax import lax
from jax.experimental import pallas as pl
from jax.experimental.pallas import tpu as pltpu

F32 = jnp.float32
BF16 = jnp.bfloat16

GRID_W = 64
HEAD_DIM = 64
SSD_HEADS = 8
SSD_D_INNER = 512
SSD_STATE = 128
SSD_GROUPS = 2
SSD_CHUNK = 128
SSD_XBC = 1024
NA_WIN_ROWS = 8
NA_WIN_COLS = 16
NA_Q_ROWS = 4
NA_K_ROWS = 12
SWA_WINDOW = 128
SWA_Q = 256
SWA_K = 512
FN_GROUP_DIM = 128
N_BRANCH = 4
BRANCH_WIDTH = 512
ROPE_BASE = 10000.0
LN_EPS = 1e-6
NEG = -1e30
DFT_ROWS = 64

LANE = 128
VMEM_LIMIT = 56 * 1024 * 1024


def _pick(dim, prefs):
    for p in prefs:
        if p <= dim and dim % p == 0:
            return p
    return dim


def _params(sem):
    return pltpu.CompilerParams(dimension_semantics=sem, vmem_limit_bytes=VMEM_LIMIT)


def _layer_norm(v):
    mu = jnp.mean(v, axis=-1, keepdims=True)
    vc = v - mu
    var = jnp.mean(vc * vc, axis=-1, keepdims=True)
    return vc * lax.rsqrt(var + LN_EPS)


def _sigmoid(v):
    return 0.5 * jnp.tanh(0.5 * v) + 0.5


def _silu(v):
    return v * _sigmoid(v)


def _cast_kernel(w_ref, o_ref):
    o_ref[...] = w_ref[...].astype(o_ref.dtype)


def _stage_weight(w_stack, l):
    _, r, c = w_stack.shape
    tr = _pick(r, (512, 256, 128))
    tc = _pick(c, (2048, 1408, 1024, 512, 256, 128))
    return pl.pallas_call(
        _cast_kernel,
        grid=(r // tr, c // tc),
        in_specs=[pl.BlockSpec((None, tr, tc), lambda i, j: (l, i, j))],
        out_specs=pl.BlockSpec((tr, tc), lambda i, j: (i, j)),
        out_shape=jax.ShapeDtypeStruct((r, c), BF16),
        compiler_params=_params(("parallel", "parallel")),
        name="stage_weight",
    )(w_stack)


def _permute_kernel(wt_ref, o_ref, *, segments, pad):
    wt = wt_ref[...]
    parts = [wt[a:a + n, :] for a, n in segments]
    if pad:
        parts.append(jnp.zeros((pad, wt.shape[1]), wt.dtype))
    o_ref[...] = jnp.concatenate(parts, axis=0).T.astype(o_ref.dtype)


def _stage_permuted(w_stack, l, segments, n_out):
    _, r, c = w_stack.shape
    tr = _pick(r, (128,))
    pad = n_out - sum(n for _, n in segments)
    return pl.pallas_call(
        functools.partial(_permute_kernel, segments=tuple(segments), pad=pad),
        grid=(r // tr,),
        in_specs=[pl.BlockSpec((None, c, tr), lambda i: (l, 0, i))],
        out_specs=pl.BlockSpec((tr, n_out), lambda i: (i, 0)),
        out_shape=jax.ShapeDtypeStruct((r, n_out), BF16),
        compiler_params=_params(("parallel",)),
        name="stage_permuted",
    )(jnp.swapaxes(w_stack, 1, 2))


def _ada_kernel(c_ref, w_ref, b_ref, o_ref):
    s = _silu(c_ref[...])
    o_ref[...] = jnp.dot(s.astype(BF16), w_ref[...].astype(BF16),
                         preferred_element_type=F32) + b_ref[...]


def _ada_mods(cvec, w_ada, b_ada):
    depth, d, n6 = w_ada.shape
    r = cvec.shape[0]
    tn = _pick(n6, (1024, 512, 256, 128))
    return pl.pallas_call(
        _ada_kernel,
        grid=(depth, n6 // tn),
        in_specs=[pl.BlockSpec((r, d), lambda l, j: (0, 0)),
                  pl.BlockSpec((None, d, tn), lambda l, j: (l, 0, j)),
                  pl.BlockSpec((None, 1, tn), lambda l, j: (l, 0, j))],
        out_specs=pl.BlockSpec((None, r, tn), lambda l, j: (l, 0, j)),
        out_shape=jax.ShapeDtypeStruct((depth, r, n6), F32),
        compiler_params=_params(("parallel", "parallel")),
        name="ada_mods",
    )(cvec, w_ada, b_ada.reshape(depth, 1, n6))


def _lnmod_mm_kernel(x_ref, sh_ref, sc_ref, w_ref, o_ref, h_ref):
    @pl.when(pl.program_id(1) == 0)
    def _():
        y = _layer_norm(x_ref[...])
        h_ref[...] = (y * (1.0 + sc_ref[...]) + sh_ref[...]).astype(BF16)

    o_ref[...] = jnp.dot(h_ref[...], w_ref[...], preferred_element_type=F32).astype(o_ref.dtype)


def _lnmod_matmul(x2, mod3, mod_row, sh_chunk, sc_chunk, w, tn):
    m, d = x2.shape
    n = w.shape[1]
    tm = mod_row.tm
    return pl.pallas_call(
        _lnmod_mm_kernel,
        grid=(m // tm, n // tn),
        in_specs=[pl.BlockSpec((tm, d), lambda i, j: (i, 0)),
                  pl.BlockSpec((None, 1, d), lambda i, j: (mod_row(i), 0, sh_chunk)),
                  pl.BlockSpec((None, 1, d), lambda i, j: (mod_row(i), 0, sc_chunk)),
                  pl.BlockSpec((d, tn), lambda i, j: (0, j))],
        out_specs=pl.BlockSpec((tm, tn), lambda i, j: (i, j)),
        out_shape=jax.ShapeDtypeStruct((m, n), BF16),
        scratch_shapes=[pltpu.VMEM((tm, d), BF16)],
        compiler_params=_params(("parallel", "arbitrary")),
        name="lnmod_matmul",
    )(x2, mod3, mod3, w)


class _ModRow:
    def __init__(self, tm, seq=None, fixed=None):
        self.tm, self.seq, self.fixed = tm, seq, fixed

    def __call__(self, i):
        if self.fixed is not None:
            return self.fixed
        return (i * self.tm) // self.seq


def _dwconv_rows(x, w_ref):
    n = x.shape[0]
    k = w_ref.shape[0]
    row = lax.broadcasted_iota(jnp.int32, x.shape, 0)
    acc = x * w_ref[k // 2:k // 2 + 1, :]
    for t in range(k):
        off = t - k // 2
        if off == 0:
            continue
        shifted = pltpu.roll(x, (-off) % n, 0)
        valid = (row + off >= 0) & (row + off < n)
        acc = acc + jnp.where(valid, shifted, 0.0) * w_ref[t:t + 1, :]
    return acc


def _conv_silu_kernel(x_ref, w_ref, b_ref, o_ref):
    y = _dwconv_rows(x_ref[...].astype(F32), w_ref) + b_ref[...]
    o_ref[...] = _silu(y).astype(o_ref.dtype)


def _conv_silu(src3, col0, width, conv_w, conv_b):
    b, n, _ = src3.shape
    tc = LANE
    c0 = col0 // tc
    kk = conv_w.shape[0]
    return pl.pallas_call(
        _conv_silu_kernel,
        grid=(b, width // tc),
        in_specs=[pl.BlockSpec((None, n, tc), lambda bi, j: (bi, 0, c0 + j)),
                  pl.BlockSpec((kk, tc), lambda bi, j: (0, j)),
                  pl.BlockSpec((1, tc), lambda bi, j: (0, j))],
        out_specs=pl.BlockSpec((None, n, tc), lambda bi, j: (bi, 0, j)),
        out_shape=jax.ShapeDtypeStruct((b, n, width), BF16),
        compiler_params=_params(("parallel", "parallel")),
        name="conv_silu",
    )(src3, conv_w, conv_b.reshape(1, width))


def _softplus(v):
    return jnp.maximum(v, 0.0) + jnp.log(1.0 + jnp.exp(-jnp.abs(v)))


def _ssd_direction(u_ref, dt_ref, a_row, dtb_row, s_ref, y_ref, d, reverse):
    q = SSD_CHUNK
    dt_all = _softplus(dt_ref[...].astype(F32) + dtb_row)
    cum = dt_all * a_row
    row = lax.broadcasted_iota(jnp.int32, (q, LANE), 0)
    s = 1
    while s < q:
        if reverse:
            cum = cum + jnp.where(row < q - s, pltpu.roll(cum, q - s, 0), 0.0)
        else:
            cum = cum + jnp.where(row >= s, pltpu.roll(cum, s, 0), 0.0)
        s *= 2
    tot = cum[0:1, :] if reverse else cum[q - 1:q, :]
    cum_t = cum.T
    dt_t = dt_all.T
    w_end_t = (jnp.exp(tot - cum) * dt_all).T
    e_tot = jnp.exp(tot)

    ri = lax.broadcasted_iota(jnp.int32, (q, q), 0)
    ci = lax.broadcasted_iota(jnp.int32, (q, q), 1)
    tri = (ri <= ci) if reverse else (ri >= ci)
    lane = lax.broadcasted_iota(jnp.int32, (q, LANE), 1)
    lo = lane < HEAD_DIM
    lane_row = lax.broadcasted_iota(jnp.int32, (1, LANE), 1)

    heads_per_group = SSD_HEADS // SSD_GROUPS
    for g in range(SSD_GROUPS):
        b0 = SSD_D_INNER + g * SSD_STATE
        c0 = SSD_D_INNER + SSD_GROUPS * SSD_STATE + g * SSD_STATE
        bg = u_ref[:, b0:b0 + SSD_STATE]
        cg = u_ref[:, c0:c0 + SSD_STATE]
        cb = lax.dot_general(cg, bg, (((1,), (1,)), ((), ())), preferred_element_type=F32)
        bg_t = bg.astype(F32).T
        for hp in range(heads_per_group // 2):
            h0 = g * heads_per_group + 2 * hp
            col0, col1 = d * SSD_HEADS + h0, d * SSD_HEADS + h0 + 1
            l0 = h0 * HEAD_DIM
            xp = u_ref[:, l0:l0 + LANE]
            ws, es, bs = [], [], []
            for col in (col0, col1):
                a_i = jnp.broadcast_to(cum[:, col:col + 1], (q, LANE))
                dec = jnp.exp(jnp.where(tri, a_i - cum_t[col:col + 1, :], NEG))
                ws.append((dec * cb * dt_t[col:col + 1, :]).astype(BF16))
                es.append(jnp.exp(a_i))
                bs.append((bg_t * w_end_t[col:col + 1, :]).astype(BF16))
            w_pair = jnp.concatenate(ws, axis=1)
            zero = jnp.zeros_like(xp)
            x_bd = jnp.concatenate([jnp.where(lo, xp, zero), jnp.where(lo, zero, xp)], axis=0)
            y_diag = jnp.dot(w_pair, x_bd, preferred_element_type=F32)
            st = s_ref[d, :, l0:l0 + LANE]
            y_off = jnp.dot(cg, st.astype(BF16), preferred_element_type=F32) * jnp.where(lo, es[0], es[1])
            y_ref[:, l0:l0 + LANE] = y_diag + y_off
            upd = jnp.dot(jnp.concatenate(bs, axis=1), x_bd, preferred_element_type=F32)
            tot_pair = jnp.where(lane_row < HEAD_DIM,
                                 jnp.broadcast_to(e_tot[:, col0:col0 + 1], (1, LANE)),
                                 jnp.broadcast_to(e_tot[:, col1:col1 + 1], (1, LANE)))
            s_ref[d, :, l0:l0 + LANE] = st * tot_pair + upd


def _ssd_kernel(uf_ref, ub_ref, dtf_ref, dtb_ref, alog_ref, dtbias_ref, h0_ref,
                yf_ref, yb_ref, hT_ref, s_ref):
    t = pl.program_id(1)

    @pl.when(t == 0)
    def _():
        s_ref[...] = h0_ref[...]

    a_row = -jnp.exp(alog_ref[...])
    dtb_row = dtbias_ref[...]
    _ssd_direction(uf_ref, dtf_ref, a_row, dtb_row, s_ref, yf_ref, 0, False)
    _ssd_direction(ub_ref, dtb_ref, a_row, dtb_row, s_ref, yb_ref, 1, True)

    @pl.when(t == pl.num_programs(1) - 1)
    def _():
        hT_ref[...] = s_ref[...]


def _ssd_scan(u3, p3, dt_col0, alog_row, dtbias_row, h0):
    b, n, _ = u3.shape
    q = SSD_CHUNK
    nt = n // q
    dtb = dt_col0 // LANE
    hp = SSD_D_INNER
    return pl.pallas_call(
        _ssd_kernel,
        grid=(b, nt),
        in_specs=[pl.BlockSpec((None, q, SSD_XBC), lambda bi, t: (bi, t, 0)),
                  pl.BlockSpec((None, q, SSD_XBC), lambda bi, t: (bi, nt - 1 - t, 0)),
                  pl.BlockSpec((None, q, LANE), lambda bi, t: (bi, t, dtb)),
                  pl.BlockSpec((None, q, LANE), lambda bi, t: (bi, nt - 1 - t, dtb)),
                  pl.BlockSpec((1, LANE), lambda bi, t: (0, 0)),
                  pl.BlockSpec((1, LANE), lambda bi, t: (0, 0)),
                  pl.BlockSpec((None, 2, SSD_STATE, hp), lambda bi, t: (bi, 0, 0, 0))],
        out_specs=[pl.BlockSpec((None, q, hp), lambda bi, t: (bi, t, 0)),
                   pl.BlockSpec((None, q, hp), lambda bi, t: (bi, nt - 1 - t, 0)),
                   pl.BlockSpec((None, 2, SSD_STATE, hp), lambda bi, t: (bi, 0, 0, 0))],
        out_shape=[jax.ShapeDtypeStruct((b, n, hp), F32),
                   jax.ShapeDtypeStruct((b, n, hp), F32),
                   jax.ShapeDtypeStruct((b, 2, SSD_STATE, hp), F32)],
        scratch_shapes=[pltpu.VMEM((2, SSD_STATE, hp), F32)],
        compiler_params=_params(("parallel", "arbitrary")),
        name="ssd_scan",
    )(u3, u3, p3, p3, alog_row, dtbias_row, h0)


def _ssd_finish_kernel(yf_ref, yb_ref, x_ref, z_ref, dskip_ref, g_ref, o_ref):
    y = dskip_ref[...] * x_ref[...].astype(F32) + yf_ref[...] + yb_ref[...]
    y = y * _silu(z_ref[...].astype(F32))
    r = lax.rsqrt(jnp.mean(y * y, axis=-1, keepdims=True) + LN_EPS)
    o_ref[...] = (y * r * g_ref[...]).astype(o_ref.dtype)


def _ssd_finish(yf2, yb2, u2, p2, z_col0, dskip_row, g_row):
    m, w = yf2.shape
    tm = _pick(m, (1024, 512, 256, 128))
    zb = z_col0 // w
    row = lambda i: (i, 0)
    return pl.pallas_call(
        _ssd_finish_kernel,
        grid=(m // tm,),
        in_specs=[pl.BlockSpec((tm, w), row), pl.BlockSpec((tm, w), row),
                  pl.BlockSpec((tm, w), row),
                  pl.BlockSpec((tm, w), lambda i: (i, zb)),
                  pl.BlockSpec((1, w), lambda i: (0, 0)),
                  pl.BlockSpec((1, w), lambda i: (0, 0))],
        out_specs=pl.BlockSpec((tm, w), row),
        out_shape=jax.ShapeDtypeStruct((m, w), BF16),
        compiler_params=_params(("parallel",)),
        name="ssd_finish",
    )(yf2, yb2, u2, p2, dskip_row, g_row)


def _block_diag_rows(kv):
    lane = lax.broadcasted_iota(jnp.int32, kv.shape, 1)
    lo = lane < HEAD_DIM
    zero = jnp.zeros_like(kv)
    return jnp.concatenate([jnp.where(lo, kv, zero), jnp.where(lo, zero, kv)], axis=0)


def _dup_group(kv, g):
    lane = lax.broadcasted_iota(jnp.int32, kv.shape, 1)
    rolled = pltpu.roll(kv, HEAD_DIM, 1)
    return jnp.where(lane // HEAD_DIM == g, kv, rolled)


def _qk(q, kbd):
    return lax.dot_general(q, kbd, (((1,), (1,)), ((), ())), preferred_element_type=F32)


def _pair_softmax_pv(s_w, s_c, bias, vbd, vcbd, sink, out_dtype):
    nk = s_w.shape[1] // 2
    nc = s_c.shape[1] // 2
    tq = s_w.shape[0]
    pws, pcs, invs = [], [], []
    for h in range(2):
        sw = s_w[:, h * nk:(h + 1) * nk] + bias[h]
        sc = s_c[:, h * nc:(h + 1) * nc]
        m = jnp.maximum(jnp.max(sw, axis=-1, keepdims=True), jnp.max(sc, axis=-1, keepdims=True))
        if sink is not None:
            m = jnp.maximum(m, sink[h])
        pw = jnp.exp(sw - m)
        pc = jnp.exp(sc - m)
        l = jnp.sum(pw, axis=-1, keepdims=True) + jnp.sum(pc, axis=-1, keepdims=True)
        if sink is not None:
            l = l + jnp.exp(sink[h] - m)
        pws.append(pw.astype(BF16))
        pcs.append(pc.astype(BF16))
        invs.append(1.0 / l)
    o = jnp.dot(jnp.concatenate(pws, axis=1), vbd, preferred_element_type=F32)
    o = o + jnp.dot(jnp.concatenate(pcs, axis=1), vcbd, preferred_element_type=F32)
    lane = lax.broadcasted_iota(jnp.int32, (tq, LANE), 1)
    inv = jnp.where(lane < HEAD_DIM, jnp.broadcast_to(invs[0], (tq, LANE)),
                    jnp.broadcast_to(invs[1], (tq, LANE)))
    return (o * inv).astype(out_dtype)


def _na_kernel(q_ref, k_ref, v_ref, kc_ref, vc_ref, bias_ref, o_ref, *, rows):
    qb = pl.program_id(2)
    nk = NA_K_ROWS * GRID_W
    ks = jnp.clip(NA_Q_ROWS * qb - (NA_K_ROWS - NA_Q_ROWS) // 2, 0, rows - NA_K_ROWS) * GRID_W
    ks = pl.multiple_of(ks, GRID_W)
    for p in range(q_ref.shape[1] // LANE):
        cs = slice(p * LANE, (p + 1) * LANE)
        q = q_ref[:, cs] * (HEAD_DIM ** -0.5)
        kbd = _block_diag_rows(k_ref[pl.ds(ks, nk), cs])
        vbd = _block_diag_rows(v_ref[pl.ds(ks, nk), cs])
        kcbd = _block_diag_rows(kc_ref[:, cs])
        vcbd = _block_diag_rows(vc_ref[:, cs])
        s_w = _qk(q, kbd)
        s_c = _qk(q, kcbd)
        o_ref[:, cs] = _pair_softmax_pv(s_w, s_c, [bias_ref[2 * p], bias_ref[2 * p + 1]], vbd, vcbd,
                                        None, o_ref.dtype)


def _na_key_start(qb, rows):
    return min(max(NA_Q_ROWS * qb - (NA_K_ROWS - NA_Q_ROWS) // 2, 0), rows - NA_K_ROWS)


def _na_bias_tables(rpb, rows):
    nqb = rows // NA_Q_ROWS
    variants = [0, 1 if nqb > 2 else 0, nqb - 1]
    nh = rpb.shape[0]
    w = GRID_W
    pad = w - NA_WIN_COLS
    rp = jnp.pad(rpb.astype(F32), ((0, 0), (0, 0), (pad, pad)))
    toep = jnp.stack([rp[:, :, w - 1 - qc:2 * w - 1 - qc] for qc in range(w)], axis=2)
    qc, kc = np.arange(w)[:, None], np.arange(w)[None, :]
    ws = np.clip(qc - NA_WIN_COLS // 2, 0, w - NA_WIN_COLS)
    toep = jnp.where(jnp.asarray((kc >= ws) & (kc < ws + NA_WIN_COLS)), toep, NEG)
    masked = jnp.full((nh, w, w), NEG, F32)
    tabs = []
    for qb in variants:
        block_rows = []
        for qr_l in range(NA_Q_ROWS):
            qr = NA_Q_ROWS * qb + qr_l
            rs = min(max(qr - NA_WIN_ROWS // 2, 0), rows - NA_WIN_ROWS)
            blocks = []
            for kr_l in range(NA_K_ROWS):
                kr = _na_key_start(qb, rows) + kr_l
                inside = rs <= kr < rs + NA_WIN_ROWS
                blocks.append(toep[:, kr - qr + NA_WIN_ROWS - 1] if inside else masked)
            block_rows.append(jnp.concatenate(blocks, axis=-1))
        tabs.append(jnp.concatenate(block_rows, axis=-2))
    return jnp.stack(tabs, axis=0)


def _na_attention(p3, pc3, q_col0, k_col0, v_col0, ck_col0, cv_col0, bias_tab):
    b, n, _ = p3.shape
    nc = pc3.shape[1]
    rows = n // GRID_W
    tq = NA_Q_ROWS * GRID_W
    nk = NA_K_ROWS * GRID_W
    nqb = n // tq
    wb = 4 * LANE
    qc, kc, vc = q_col0 // wb, k_col0 // wb, v_col0 // wb
    ckc, cvc = ck_col0 // wb, cv_col0 // wb
    ngrp = 1

    def variant(qb):
        return jnp.where(qb == 0, 0, jnp.where(qb == nqb - 1, 2, 1))

    return pl.pallas_call(
        functools.partial(_na_kernel, rows=rows),
        grid=(b, ngrp, nqb),
        in_specs=[pl.BlockSpec((None, tq, wb), lambda bi, h, i: (bi, i, qc + h)),
                  pl.BlockSpec((None, n, wb), lambda bi, h, i: (bi, 0, kc + h)),
                  pl.BlockSpec((None, n, wb), lambda bi, h, i: (bi, 0, vc + h)),
                  pl.BlockSpec((None, nc, wb), lambda bi, h, i: (bi, 0, ckc + h)),
                  pl.BlockSpec((None, nc, wb), lambda bi, h, i: (bi, 0, cvc + h)),
                  pl.BlockSpec((None, wb // HEAD_DIM, tq, nk), lambda bi, h, i: (variant(i), h, 0, 0))],
        out_specs=pl.BlockSpec((None, tq, wb), lambda bi, h, i: (bi, i, h)),
        out_shape=jax.ShapeDtypeStruct((b, n, ngrp * wb), BF16),
        compiler_params=_params(("parallel", "parallel", "arbitrary")),
        name="na_attention",
    )(p3, p3, p3, pc3, pc3, bias_tab)


def _rope(x, cos, sin_signed):
    lane = lax.broadcasted_iota(jnp.int32, x.shape, 1)
    first = (lane % 32) < 16
    partner = jnp.where(first, pltpu.roll(x, LANE - 16, 1), pltpu.roll(x, 16, 1))
    return x * cos + partner * sin_signed


def _sink_pair(sink_ref):
    s = sink_ref[...]
    return [s[:, 0:1], s[:, HEAD_DIM:HEAD_DIM + 1]]


def _swa_kernel(q_ref, k_ref, v_ref, kc_ref, vc_ref, cos_ref, sin_ref, sink_ref, o_ref, *, n):
    qb = pl.program_id(1)
    q0 = pl.multiple_of(qb * SWA_Q, SWA_Q)
    ks = pl.multiple_of(jnp.clip(qb * SWA_Q - SWA_WINDOW, 0, n - SWA_K), SWA_WINDOW)
    cos_q, sin_q = cos_ref[pl.ds(q0, SWA_Q), :], sin_ref[pl.ds(q0, SWA_Q), :]
    kk = _rope(k_ref[pl.ds(ks, SWA_K), :].astype(F32), cos_ref[pl.ds(ks, SWA_K), :],
               sin_ref[pl.ds(ks, SWA_K), :])
    vv = v_ref[pl.ds(ks, SWA_K), :].astype(F32)
    kc, vc = kc_ref[...].astype(F32), vc_ref[...].astype(F32)
    qpos = q0 + lax.broadcasted_iota(jnp.int32, (SWA_Q, SWA_K), 0)
    kpos = ks + lax.broadcasted_iota(jnp.int32, (SWA_Q, SWA_K), 1)
    bias = jnp.where(jnp.abs(kpos - qpos) <= SWA_WINDOW, 0.0, NEG)
    pairs_per_group = 2
    for p in range(q_ref.shape[1] // LANE):
        g = p // pairs_per_group
        if p % pairs_per_group == 0:
            kbd = _block_diag_rows(_dup_group(kk, g).astype(BF16))
            vbd = _block_diag_rows(_dup_group(vv, g).astype(BF16))
            kcbd = _block_diag_rows(_dup_group(kc, g).astype(BF16))
            vcbd = _block_diag_rows(_dup_group(vc, g).astype(BF16))
        cs = slice(p * LANE, (p + 1) * LANE)
        q = _rope(q_ref[:, cs].astype(F32), cos_q, sin_q)
        q = (q * (HEAD_DIM ** -0.5)).astype(BF16)
        s_w = _qk(q, kbd)
        s_c = _qk(q, kcbd)
        o_ref[:, cs] = _pair_softmax_pv(s_w, s_c, [bias, bias], vbd, vcbd, _sink_pair(sink_ref.at[p]),
                                        o_ref.dtype)


def _swa_attention(p3, pc3, q_col0, kv_col0, ckv_col0, cos_tab, sin_tab, sink_rows):
    b, n, _ = p3.shape
    nc = pc3.shape[1]
    nqb = n // SWA_Q
    wb = 4 * LANE
    qc, kc, ckc = q_col0 // wb, kv_col0 // LANE, ckv_col0 // LANE
    return pl.pallas_call(
        functools.partial(_swa_kernel, n=n),
        grid=(b, nqb),
        in_specs=[pl.BlockSpec((None, SWA_Q, wb), lambda bi, i: (bi, i, qc)),
                  pl.BlockSpec((None, n, LANE), lambda bi, i: (bi, 0, kc)),
                  pl.BlockSpec((None, n, LANE), lambda bi, i: (bi, 0, kc + 1)),
                  pl.BlockSpec((None, nc, LANE), lambda bi, i: (bi, 0, ckc)),
                  pl.BlockSpec((None, nc, LANE), lambda bi, i: (bi, 0, ckc + 1)),
                  pl.BlockSpec((n, LANE), lambda bi, i: (0, 0)),
                  pl.BlockSpec((n, LANE), lambda bi, i: (0, 0)),
                  pl.BlockSpec((wb // LANE, 1, LANE), lambda bi, i: (0, 0, 0))],
        out_specs=pl.BlockSpec((None, SWA_Q, wb), lambda bi, i: (bi, i, 0)),
        out_shape=jax.ShapeDtypeStruct((b, n, wb), BF16),
        compiler_params=_params(("parallel", "arbitrary")),
        name="swa_attention",
    )(p3, p3, p3, pc3, pc3, cos_tab, sin_tab, sink_rows)


def _ctx_attn_kernel(q_ref, k_ref, v_ref, sink_ref, o_ref, *, grouped):
    hp = pl.program_id(1)
    q = q_ref[...] * (HEAD_DIM ** -0.5)
    k, v = k_ref[...], v_ref[...]
    if grouped:
        g = hp // 2
        k = _dup_group(k.astype(F32), g).astype(BF16)
        v = _dup_group(v.astype(F32), g).astype(BF16)
    kbd, vbd = _block_diag_rows(k), _block_diag_rows(v)
    s = _qk(q, kbd)
    t = q.shape[0]
    sink = _sink_pair(sink_ref) if grouped else None
    ps, invs = [], []
    for h in range(2):
        sh = s[:, h * t:(h + 1) * t]
        m = jnp.max(sh, axis=-1, keepdims=True)
        if sink is not None:
            m = jnp.maximum(m, sink[h])
        p = jnp.exp(sh - m)
        l = jnp.sum(p, axis=-1, keepdims=True)
        if sink is not None:
            l = l + jnp.exp(sink[h] - m)
        ps.append(p.astype(BF16))
        invs.append(1.0 / l)
    o = jnp.dot(jnp.concatenate(ps, axis=1), vbd, preferred_element_type=F32)
    lane = lax.broadcasted_iota(jnp.int32, (t, LANE), 1)
    inv = jnp.where(lane < HEAD_DIM, jnp.broadcast_to(invs[0], (t, LANE)),
                    jnp.broadcast_to(invs[1], (t, LANE)))
    o_ref[...] = (o * inv).astype(o_ref.dtype)


def _ctx_attention(pc3, q_col0, k_col0, v_col0, sink_rows, grouped):
    b, nc, _ = pc3.shape
    qc, kc, vc = q_col0 // LANE, k_col0 // LANE, v_col0 // LANE
    npair = 4
    kv_blk = (lambda h: 0) if grouped else (lambda h: h)
    return pl.pallas_call(
        functools.partial(_ctx_attn_kernel, grouped=grouped),
        grid=(b, npair),
        in_specs=[pl.BlockSpec((None, nc, LANE), lambda bi, h: (bi, 0, qc + h)),
                  pl.BlockSpec((None, nc, LANE), lambda bi, h: (bi, 0, kc + kv_blk(h))),
                  pl.BlockSpec((None, nc, LANE), lambda bi, h: (bi, 0, vc + kv_blk(h))),
                  pl.BlockSpec((None, 1, LANE), lambda bi, h: (h, 0, 0))],
        out_specs=pl.BlockSpec((None, nc, LANE), lambda bi, h: (bi, 0, h)),
        out_shape=jax.ShapeDtypeStruct((b, nc, npair * LANE), BF16),
        compiler_params=_params(("parallel", "parallel")),
        name="ctx_attention",
    )(pc3, pc3, pc3, sink_rows)


def _fn_channel_kernel(x_ref, cs_ref, o_ref):
    w = x_ref.shape[1]
    ngroups = w // FN_GROUP_DIM
    for g in range(ngroups):
        xg = x_ref[:, g * FN_GROUP_DIM:(g + 1) * FN_GROUP_DIM]
        ab = jnp.dot(xg, cs_ref[...], preferred_element_type=F32)
        o_ref[:, g * FN_GROUP_DIM:(g + 1) * FN_GROUP_DIM] = ab[:, :FN_GROUP_DIM].astype(o_ref.dtype)
        o_ref[:, w + g * FN_GROUP_DIM:w + (g + 1) * FN_GROUP_DIM] = ab[:, FN_GROUP_DIM:].astype(o_ref.dtype)


def _fn_channel(p2, col0, width):
    m = p2.shape[0]
    tm = _pick(m, (1024, 512, 256, 128))
    k = np.arange(FN_GROUP_DIM)
    ang = 2.0 * np.pi * ((k[:, None] * k[None, :]) % FN_GROUP_DIM) / FN_GROUP_DIM
    cs = jnp.asarray(np.concatenate([np.cos(ang), np.sin(ang)], axis=1), BF16)
    cb = col0 // width
    return pl.pallas_call(
        _fn_channel_kernel,
        grid=(m // tm,),
        in_specs=[pl.BlockSpec((tm, width), lambda i: (i, cb)),
                  pl.BlockSpec((FN_GROUP_DIM, 2 * FN_GROUP_DIM), lambda i: (0, 0))],
        out_specs=pl.BlockSpec((tm, 2 * width), lambda i: (i, 0)),
        out_shape=jax.ShapeDtypeStruct((m, 2 * width), BF16),
        compiler_params=_params(("parallel",)),
        name="fn_channel",
    )(p2, cs)


def _fn_position_kernel(ac_ref, as_ref, bc_ref, bs_ref, ab_ref, o_ref, c_ref, s_ref, *, scale):
    @pl.when(pl.program_id(1) == 0)
    def _():
        for j in range(ac_ref.shape[0]):
            ca, sa = ac_ref[j:j + 1, :], as_ref[j:j + 1, :]
            cb, sb = bc_ref[...], bs_ref[...]
            c_ref[j * DFT_ROWS:(j + 1) * DFT_ROWS, :] = (ca * cb - sa * sb).astype(BF16)
            s_ref[j * DFT_ROWS:(j + 1) * DFT_ROWS, :] = (sa * cb + ca * sb).astype(BF16)

    w = o_ref.shape[1]
    y = jnp.dot(c_ref[...], ab_ref[:, :w], preferred_element_type=F32)
    y = y - jnp.dot(s_ref[...], ab_ref[:, w:], preferred_element_type=F32)
    o_ref[...] = (y * scale).astype(o_ref.dtype)


def _fn_position(ab3):
    b, n, w2 = ab3.shape
    w = w2 // 2
    tm = _pick(n, (512, 256, 128, 64))
    jc = tm // DFT_ROWS
    n1 = n // DFT_ROWS
    k = np.arange(n, dtype=np.int64)
    j1 = np.arange(n1, dtype=np.int64)
    j2 = np.arange(DFT_ROWS, dtype=np.int64)
    ang_a = 2.0 * np.pi * ((j1[:, None] * DFT_ROWS * k[None, :]) % n) / n
    ang_b = 2.0 * np.pi * ((j2[:, None] * k[None, :]) % n) / n
    ac, as_ = jnp.asarray(np.cos(ang_a), F32), jnp.asarray(np.sin(ang_a), F32)
    bc, bs = jnp.asarray(np.cos(ang_b), F32), jnp.asarray(np.sin(ang_b), F32)
    scale = 1.0 / math.sqrt(n * FN_GROUP_DIM)
    if n1 % 8 != 0 and jc != n1:
        raise ValueError("unsupported sequence length for the position DFT tiling")
    return pl.pallas_call(
        functools.partial(_fn_position_kernel, scale=scale),
        grid=(n // tm, b),
        in_specs=[pl.BlockSpec((jc, n), lambda i, bi: (i, 0)),
                  pl.BlockSpec((jc, n), lambda i, bi: (i, 0)),
                  pl.BlockSpec((DFT_ROWS, n), lambda i, bi: (0, 0)),
                  pl.BlockSpec((DFT_ROWS, n), lambda i, bi: (0, 0)),
                  pl.BlockSpec((None, n, w2), lambda i, bi: (bi, 0, 0))],
        out_specs=pl.BlockSpec((None, tm, w), lambda i, bi: (bi, i, 0)),
        out_shape=jax.ShapeDtypeStruct((b, n, w), BF16),
        scratch_shapes=[pltpu.VMEM((tm, n), BF16), pltpu.VMEM((tm, n), BF16)],
        compiler_params=_params(("parallel", "arbitrary")),
        name="fn_position",
    )(ac, as_, bc, bs, ab3)


def _residual_ln(o_ref, x_ref, gate_ref, lng_ref, lnb_ref, alpha):
    v = alpha * x_ref[...] + gate_ref[...] * o_ref[...]
    o_ref[...] = _layer_norm(v) * lng_ref[...] + lnb_ref[...]


def _merge_out_kernel(y0, y1, y2, y3, g0, g1, g2, g3, wb_ref, bg_ref, wo_ref, x_ref, gate_ref, lng_ref,
                      lnb_ref, o_ref, *, alpha):
    j = pl.program_id(1)

    @pl.when(j == 0)
    def _():
        o_ref[...] = jnp.zeros_like(o_ref)

    merged = None
    for i, (y, g) in enumerate(((y0, g0), (y1, g1), (y2, g2), (y3, g3))):
        gate = _sigmoid(g[...].astype(F32) + bg_ref[i])
        term = gate * jnp.dot(y[...], wb_ref[i], preferred_element_type=F32)
        merged = term if merged is None else merged + term
    o_ref[...] += jnp.dot(merged.astype(BF16), wo_ref[...], preferred_element_type=F32)

    @pl.when(j == pl.num_programs(1) - 1)
    def _():
        _residual_ln(o_ref, x_ref, gate_ref, lng_ref, lnb_ref, alpha)


def _merge_out(ys, p2, gate_col0, wb, bg3, w_out, x2, mod3, mod_row, gate_chunk, ln_g, ln_b, alpha):
    m = p2.shape[0]
    d = wb.shape[2]
    tm = mod_row.tm
    tn = _pick(d, (1024, 512, 256, 128))
    gb = gate_col0 // tn
    per = d // tn
    bw = ys[0].shape[1]
    y_specs = [pl.BlockSpec((tm, bw), lambda i, j: (i, 0)) for _ in range(N_BRANCH)]
    g_specs = [pl.BlockSpec((tm, tn), functools.partial(lambda i, j, br: (i, gb + br * per + j), br=br))
               for br in range(N_BRANCH)]
    return pl.pallas_call(
        functools.partial(_merge_out_kernel, alpha=alpha),
        grid=(m // tm, per),
        in_specs=y_specs + g_specs + [
            pl.BlockSpec((N_BRANCH, bw, tn), lambda i, j: (0, 0, j)),
            pl.BlockSpec((N_BRANCH, 1, tn), lambda i, j: (0, 0, j)),
            pl.BlockSpec((tn, d), lambda i, j: (j, 0)),
            pl.BlockSpec((tm, d), lambda i, j: (i, 0)),
            pl.BlockSpec((None, 1, d), lambda i, j: (mod_row(i), 0, gate_chunk)),
            pl.BlockSpec((1, d), lambda i, j: (0, 0)),
            pl.BlockSpec((1, d), lambda i, j: (0, 0))],
        out_specs=pl.BlockSpec((tm, d), lambda i, j: (i, 0)),
        out_shape=jax.ShapeDtypeStruct((m, d), F32),
        compiler_params=_params(("parallel", "arbitrary")),
        name="merge_out",
    )(*ys, p2, p2, p2, p2, wb, bg3, w_out, x2, mod3, ln_g.reshape(1, d), ln_b.reshape(1, d))


def _mm_resln_kernel(a_ref, w_ref, x_ref, gate_ref, lng_ref, lnb_ref, o_ref, *, alpha):
    k = pl.program_id(1)

    @pl.when(k == 0)
    def _():
        o_ref[...] = jnp.zeros_like(o_ref)

    o_ref[...] += jnp.dot(a_ref[...], w_ref[...], preferred_element_type=F32)

    @pl.when(k == pl.num_programs(1) - 1)
    def _():
        _residual_ln(o_ref, x_ref, gate_ref, lng_ref, lnb_ref, alpha)


def _mm_resln(a2, w, x2, mod3, mod_row, gate_chunk, ln_g, ln_b, alpha, tk):
    m, kdim = a2.shape
    d = w.shape[1]
    tm = mod_row.tm
    return pl.pallas_call(
        functools.partial(_mm_resln_kernel, alpha=alpha),
        grid=(m // tm, kdim // tk),
        in_specs=[pl.BlockSpec((tm, tk), lambda i, k: (i, k)),
                  pl.BlockSpec((tk, d), lambda i, k: (k, 0)),
                  pl.BlockSpec((tm, d), lambda i, k: (i, 0)),
                  pl.BlockSpec((None, 1, d), lambda i, k: (mod_row(i), 0, gate_chunk)),
                  pl.BlockSpec((1, d), lambda i, k: (0, 0)),
                  pl.BlockSpec((1, d), lambda i, k: (0, 0))],
        out_specs=pl.BlockSpec((tm, d), lambda i, k: (i, 0)),
        out_shape=jax.ShapeDtypeStruct((m, d), F32),
        compiler_params=_params(("parallel", "arbitrary")),
        name="matmul_residual_ln",
    )(a2, w, x2, mod3, ln_g.reshape(1, d), ln_b.reshape(1, d))


HALO_ROWS = 8


def _ffn_up_kernel(x_ref, xp_ref, xn_ref, sh_ref, sc_ref, wg_ref, wu_ref, cw_ref, cb_ref, o_ref, h_ref,
                   *, seq):
    i = pl.program_id(0)
    tm = x_ref.shape[0]

    @pl.when(pl.program_id(1) == 0)
    def _():
        scale, shift = 1.0 + sc_ref[...], sh_ref[...]
        h_ref[0:tm, :] = (_layer_norm(x_ref[...]) * scale + shift).astype(BF16)
        halo = jnp.concatenate([xp_ref[...], xn_ref[...]], axis=0)
        h_ref[tm:tm + 2 * HALO_ROWS, :] = (_layer_norm(halo) * scale + shift).astype(BF16)

    g_all = jnp.dot(h_ref[...], wg_ref[...], preferred_element_type=F32)
    u = jnp.dot(h_ref[0:tm, :], wu_ref[...], preferred_element_type=F32)
    g = g_all[0:tm, :]
    has_prev = jnp.where((i * tm) % seq == 0, 0.0, 1.0)
    has_next = jnp.where(((i + 1) * tm) % seq == 0, 0.0, 1.0)
    g_prev = g_all[tm + HALO_ROWS - 1:tm + HALO_ROWS, :] * has_prev
    g_next = g_all[tm + HALO_ROWS:tm + HALO_ROWS + 1, :] * has_next
    row = lax.broadcasted_iota(jnp.int32, g.shape, 0)
    above = jnp.where(row == 0, g_prev, pltpu.roll(g, 1, 0))
    below = jnp.where(row == tm - 1, g_next, pltpu.roll(g, tm - 1, 0))
    y = above * cw_ref[0:1, :] + g * cw_ref[1:2, :] + below * cw_ref[2:3, :] + cb_ref[...]
    o_ref[...] = (_silu(y) * u).astype(o_ref.dtype)


def _ffn_up(x2, mod3, mod_row, sh_chunk, sc_chunk, w_up, conv_w, conv_b, tn, seq):
    m, d = x2.shape
    dff = w_up.shape[1] // 2
    tm = mod_row.tm
    if conv_w.shape[0] != 3 or seq % tm != 0 or tm % HALO_ROWS != 0:
        raise ValueError("unsupported ConvFFN tiling")
    ub = dff // tn
    hb = tm // HALO_ROWS
    last_hb = m // HALO_ROWS - 1
    return pl.pallas_call(
        functools.partial(_ffn_up_kernel, seq=seq),
        grid=(m // tm, dff // tn),
        in_specs=[pl.BlockSpec((tm, d), lambda i, j: (i, 0)),
                  pl.BlockSpec((HALO_ROWS, d), lambda i, j: (jnp.maximum(i * hb - 1, 0), 0)),
                  pl.BlockSpec((HALO_ROWS, d), lambda i, j: (jnp.minimum((i + 1) * hb, last_hb), 0)),
                  pl.BlockSpec((None, 1, d), lambda i, j: (mod_row(i), 0, sh_chunk)),
                  pl.BlockSpec((None, 1, d), lambda i, j: (mod_row(i), 0, sc_chunk)),
                  pl.BlockSpec((d, tn), lambda i, j: (0, j)),
                  pl.BlockSpec((d, tn), lambda i, j: (0, ub + j)),
                  pl.BlockSpec((3, tn), lambda i, j: (0, j)),
                  pl.BlockSpec((1, tn), lambda i, j: (0, j))],
        out_specs=pl.BlockSpec((tm, tn), lambda i, j: (i, j)),
        out_shape=jax.ShapeDtypeStruct((m, dff), BF16),
        scratch_shapes=[pltpu.VMEM((tm + 2 * HALO_ROWS, d), BF16)],
        compiler_params=_params(("parallel", "arbitrary")),
        name="ffn_up",
    )(x2, x2, x2, mod3, mod3, w_up, w_up, conv_w, conv_b.reshape(1, dff))


def _rope_tables(n):
    t = jnp.arange(n)
    rows = (t // GRID_W).astype(F32)
    cols = (t % GRID_W).astype(F32)
    n_freq = HEAD_DIM // 4
    inv = ROPE_BASE ** (-jnp.arange(n_freq, dtype=F32) / n_freq)
    ar, ac = rows[:, None] * inv, cols[:, None] * inv
    cos = jnp.concatenate([jnp.cos(ar), jnp.cos(ar), jnp.cos(ac), jnp.cos(ac)], axis=1)
    sin = jnp.concatenate([-jnp.sin(ar), jnp.sin(ar), -jnp.sin(ac), jnp.sin(ac)], axis=1)
    return jnp.tile(cos, (1, 2)), jnp.tile(sin, (1, 2))


def _branch_params(conv_w, conv_b, a_log, dt_bias, d_skip, norm_g, rpb, sink, rows, cos_tab, sin_tab):
    pad = jnp.zeros((1, LANE - 2 * SSD_HEADS), F32)
    return dict(
        conv_w=conv_w, conv_b=conv_b,
        alog_row=jnp.concatenate([a_log.reshape(1, -1), pad], axis=1),
        dtb_row=jnp.concatenate([dt_bias.reshape(1, -1), pad], axis=1),
        dskip_row=jnp.repeat(d_skip, HEAD_DIM).reshape(1, SSD_D_INNER),
        g_row=norm_g.reshape(1, SSD_D_INNER),
        sink_rows=jnp.repeat(sink, HEAD_DIM).reshape(4, 1, LANE),
        bias_tab=_na_bias_tables(rpb, rows),
        cos_tab=cos_tab, sin_tab=sin_tab)


def _mixer_branches(p3, pc3, cols, ccols, prm, ctx_out):
    bsz, n, npad = p3.shape
    nctx = pc3.shape[1]
    p2 = p3.reshape(bsz * n, npad)
    pc2 = pc3.reshape(bsz * nctx, pc3.shape[2])
    flat = lambda t: t.reshape(-1, t.shape[-1])

    u3 = _conv_silu(p3, cols["xbc"], SSD_XBC, prm["conv_w"], prm["conv_b"])
    uc3 = _conv_silu(pc3, ccols["xbc"], SSD_XBC, prm["conv_w"], prm["conv_b"])
    h_zero = jnp.zeros((bsz, 2, SSD_STATE, SSD_D_INNER), F32)
    yfc, ybc, h_ctx = _ssd_scan(uc3, pc3, ccols["dt"], prm["alog_row"], prm["dtb_row"], h_zero)
    yf, yb, _ = _ssd_scan(u3, p3, cols["dt"], prm["alog_row"], prm["dtb_row"], h_ctx)
    y_ssd = _ssd_finish(flat(yf), flat(yb), flat(u3), p2, cols["z"], prm["dskip_row"], prm["g_row"])
    y_na = flat(_na_attention(p3, pc3, cols["nq"], cols["nk"], cols["nv"], ccols["nk"], ccols["nv"],
                              prm["bias_tab"]))
    y_swa = flat(_swa_attention(p3, pc3, cols["sq"], cols["skv"], ccols["skv"], prm["cos_tab"],
                                prm["sin_tab"], prm["sink_rows"]))
    y_fn = flat(_fn_position(_fn_channel(p2, cols["fn"], 512).reshape(bsz, n, -1)))
    ys = [y_ssd, y_na, y_swa, y_fn]
    if not ctx_out:
        return ys, None
    yc_ssd = _ssd_finish(flat(yfc), flat(ybc), flat(uc3), pc2, ccols["z"], prm["dskip_row"], prm["g_row"])
    yc_na = flat(_ctx_attention(pc3, ccols["nq"], ccols["nk"], ccols["nv"], prm["sink_rows"], False))
    yc_swa = flat(_ctx_attention(pc3, ccols["sq"], ccols["skv"], ccols["skv"] + LANE, prm["sink_rows"],
                                 True))
    yc_fn = flat(_fn_position(_fn_channel(pc2, ccols["fn"], 512).reshape(bsz, nctx, -1)))
    return ys, [yc_ssd, yc_na, yc_swa, yc_fn]


def kernel(x, c, ctx, c_ctx, w_ada, b_ada, w_in, b_gate, ssd_conv_w, ssd_conv_b, ssd_a_log,
           ssd_dt_bias, ssd_d, ssd_norm_g, na_rpb, swa_sink, w_branch, w_out, ln1_g, ln1_b,
           ln2_g, ln2_b, ffn_w_up, ffn_conv_w, ffn_conv_b, ffn_w_down):
    bsz, n, d = x.shape
    nctx = ctx.shape[1]
    depth = w_ada.shape[0]
    dff = ffn_w_down.shape[1]
    alpha = (2.0 * depth) ** 0.25
    rows = n // GRID_W

    g_col = 0
    z_col = N_BRANCH * d
    fn_col = z_col + 512
    sq_col = fn_col + 512
    nq_col = sq_col + 512
    nk_col = nq_col + 512
    nv_col = nk_col + 512
    xbc_col = nv_col + 512
    skv_col = xbc_col + SSD_XBC
    dt_col = skv_col + 256
    n_used = dt_col + LANE
    n_in = -(-n_used // 256) * 256
    tn_in = _pick(n_in, (1280, 1024, 768, 512, 256))
    cols = dict(z=z_col, fn=fn_col, sq=sq_col, nq=nq_col, nk=nk_col, nv=nv_col, xbc=xbc_col,
                skv=skv_col, dt=dt_col)

    o_z, o_xbc, o_dt = 0, 512, 512 + SSD_XBC
    o_na = o_dt + 2 * SSD_HEADS
    o_sq = o_na + 3 * 512
    o_skv = o_sq + 512
    o_fn = o_skv + 256
    o_g = o_fn + 512

    in_segments = [(o_g, N_BRANCH * d), (o_z, 512), (o_fn, 512), (o_sq, 512), (o_na, 3 * 512),
                   (o_xbc, SSD_XBC), (o_skv, 256), (o_dt, 2 * SSD_HEADS)]

    n_rows = -(-(bsz + 1) // 8) * 8
    cvec = jnp.concatenate([c, c_ctx[None], jnp.zeros((n_rows - bsz - 1, d), F32)], axis=0)
    mods = _ada_mods(cvec, w_ada, b_ada)

    lat_row = _ModRow(_pick(n, (1024, 512, 256)), seq=n)
    ctx_row = _ModRow(_pick(bsz * nctx, (1024, 512, 256)), fixed=bsz)
    lat_row_h = _ModRow(_pick(n, (512, 256)), seq=n)
    ctx_row_h = _ModRow(_pick(bsz * nctx, (512, 256)), fixed=bsz)
    ctx_row_seq = _ModRow(_pick(nctx, (1024, 512, 256)), fixed=bsz)
    tn_up = _pick(dff, (512, 256, 128))
    tk_down = _pick(dff, (512, 256, 128))

    cos_tab, sin_tab = _rope_tables(n)

    x2 = x.reshape(bsz * n, d)
    xc2 = ctx.reshape(bsz * nctx, d)
    for l in range(depth):
        ctx_out = l < depth - 1
        mod3 = mods[l].reshape(n_rows, 1, 6 * d)
        w_in_p = _stage_permuted(w_in, l, in_segments, n_in)
        wb = _stage_weight(w_branch.reshape(depth, N_BRANCH * BRANCH_WIDTH, d), l)
        wb = wb.reshape(N_BRANCH, BRANCH_WIDTH, d)
        bg3 = b_gate[l].reshape(N_BRANCH, 1, d)
        w_out_b = _stage_weight(w_out, l)
        w_up_b = _stage_weight(ffn_w_up, l)
        w_down_b = _stage_weight(ffn_w_down, l)
        prm = _branch_params(ssd_conv_w[l], ssd_conv_b[l], ssd_a_log[l], ssd_dt_bias[l], ssd_d[l],
                             ssd_norm_g[l], na_rpb[l], swa_sink[l], rows, cos_tab, sin_tab)

        p2 = _lnmod_matmul(x2, mod3, lat_row, 0, 1, w_in_p, tn_in)
        p3 = p2.reshape(bsz, n, n_in)
        if ctx_out or nk_col % tn_in != 0:
            w_ctx, ccols = w_in_p, cols
        else:
            w_ctx = w_in_p[:, nk_col:]
            ccols = {k: v - nk_col for k, v in cols.items() if v >= nk_col}
        pc2 = _lnmod_matmul(xc2, mod3, ctx_row, 0, 1, w_ctx, tn_in)
        pc3 = pc2.reshape(bsz, nctx, pc2.shape[1])

        ys, ycs = _mixer_branches(p3, pc3, cols, ccols, prm, ctx_out)

        x_mid = _merge_out(ys, p2, g_col, wb, bg3, w_out_b, x2, mod3, lat_row_h, 2, ln1_g[l], ln1_b[l],
                           alpha)
        act = _ffn_up(x_mid, mod3, lat_row, 3, 4, w_up_b, ffn_conv_w[l], ffn_conv_b[l], tn_up, n)
        x2 = _mm_resln(act, w_down_b, x_mid, mod3, lat_row, 5, ln2_g[l], ln2_b[l], alpha, tk_down)

        if ctx_out:
            xc_mid = _merge_out(ycs, pc2, g_col, wb, bg3, w_out_b, xc2, mod3, ctx_row_h, 2,
                                ln1_g[l], ln1_b[l], alpha)
            act_c = _ffn_up(xc_mid, mod3, ctx_row_seq, 3, 4, w_up_b, ffn_conv_w[l], ffn_conv_b[l],
                            tn_up, nctx)
            xc2 = _mm_resln(act_c, w_down_b, xc_mid, mod3, ctx_row, 5, ln2_g[l], ln2_b[l], alpha,
                            tk_down)

    return x2.reshape(bsz, n, d)
```
